```python
import math
import jax, jax.numpy as jnp
from jax import lax
import numpy as np

D_MODEL = 1024
BATCH = 4
SEQ = 4096
DEPTH = 1
DEC_BATCH = 128
DEC_SEQ = 4
PAST_LEN = 2048
PAGE_SIZE = 128

N_HEADS = 8
HEAD_DIM = 64
ROT_DIM = HEAD_DIM // 4
ROPE_THETA = 500000.0
ATTN_W = N_HEADS * 2 * HEAD_DIM
C_CONV = D_MODEL
CONV_WIDTH = 31
N_EXPERTS = 32
TOP_K = 4
D_FF = D_MODEL
SWIGLU_ALPHA = 1.702
SWIGLU_LIMIT = 7.0
RMS_EPS = 1e-5
LN_EPS = 1e-5
Q_BLOCK = 128
N_IN = 2 * C_CONV + 3 * ATTN_W + 2 * D_MODEL

kernel_name = 'conformer_diffattn_moe_gated_hybrid_step'


def rmsnorm(x, g):
    xf = x.astype(jnp.float32)
    y = xf * lax.rsqrt(jnp.mean(xf * xf, axis=-1, keepdims=True) + RMS_EPS)
    return (y * g.astype(jnp.float32)).astype(x.dtype)


def layernorm(x, g, b):
    xf = x.astype(jnp.float32)
    mu = jnp.mean(xf, axis=-1, keepdims=True)
    xc = xf - mu
    y = xc * lax.rsqrt(jnp.mean(xc * xc, axis=-1, keepdims=True) + LN_EPS)
    return (y * g.astype(jnp.float32) + b.astype(jnp.float32)).astype(x.dtype)


def rope_partial(x, pos):
    half = ROT_DIM // 2
    inv = jnp.power(jnp.float32(ROPE_THETA), -jnp.arange(half, dtype=jnp.float32) * (2.0 / ROT_DIM))
    ang = pos.astype(jnp.float32)[:, None] * inv[None, :]
    cos = jnp.cos(ang)[None, :, None, None, :]
    sin = jnp.sin(ang)[None, :, None, None, :]
    xr = x[..., :ROT_DIM].astype(jnp.float32)
    x1, x2 = xr[..., :half], xr[..., half:]
    rot = jnp.concatenate([x1 * cos - x2 * sin, x2 * cos + x1 * sin], axis=-1).astype(x.dtype)
    return jnp.concatenate([rot, x[..., ROT_DIM:]], axis=-1)


def diff_attn_core(q, k, v, q_pos, k_pos, lam):
    s = jnp.einsum('bqhcd,bkhcd->bhcqk', q, k, preferred_element_type=jnp.float32) * (HEAD_DIM ** -0.5)
    mask = k_pos[None, :] <= q_pos[:, None]
    s = jnp.where(mask, s, -jnp.inf)
    p = jax.nn.softmax(s, axis=-1)
    a = p[:, :, 0] - lam * p[:, :, 1]
    return jnp.einsum('bhqk,bkhe->bqhe', a.astype(v.dtype), v)


def attn_prompt(q, k, v, lam):
    B, L = q.shape[0], q.shape[1]
    nb = L // Q_BLOCK
    qb = q.reshape(B, nb, Q_BLOCK, N_HEADS, 2, HEAD_DIM).transpose(1, 0, 2, 3, 4, 5)
    k_pos = jnp.arange(L)

    def one_block(args):
        qi, i = args
        q_pos = i * Q_BLOCK + jnp.arange(Q_BLOCK)
        return diff_attn_core(qi, k, v, q_pos, k_pos, lam)

    o = lax.map(one_block, (qb, jnp.arange(nb)))
    return o.transpose(1, 0, 2, 3, 4).reshape(B, L, N_HEADS, 2 * HEAD_DIM)


def depthwise_causal(ext, w, b):
    y = lax.conv_general_dilated(ext, w[:, None, :].astype(ext.dtype), window_strides=(1,), padding='VALID',
                                 dimension_numbers=('NWC', 'WIO', 'NWC'), feature_group_count=C_CONV)
    return y + b


def moe(h, w_router, b_router, w1, b1, w2, b2):
    Bn, L, D = h.shape
    hf = h.reshape(Bn * L, D)
    T = Bn * L
    logits = (hf @ w_router + b_router).astype(jnp.float32)
    top_v, top_i = lax.top_k(logits, TOP_K)
    top_w = jax.nn.softmax(top_v, axis=-1).astype(h.dtype)
    flat_e = top_i.reshape(-1)
    order = jnp.argsort(flat_e)
    sorted_e = flat_e[order]
    tok = order // TOP_K
    wts = top_w.reshape(-1)[order]
    group_sizes = jnp.bincount(flat_e, length=N_EXPERTS).astype(jnp.int32)
    xs = hf[tok]
    z = lax.ragged_dot(xs, w1, group_sizes) + b1[sorted_e]
    gate = jnp.minimum(z[:, :D_FF], SWIGLU_LIMIT)
    lin = jnp.clip(z[:, D_FF:], -SWIGLU_LIMIT, SWIGLU_LIMIT)
    act = gate * jax.nn.sigmoid(SWIGLU_ALPHA * gate) * (lin + 1.0)
    out = lax.ragged_dot(act, w2, group_sizes) + b2[sorted_e]
    y = jnp.zeros((T, D), h.dtype).at[tok].add(out * wts[:, None])
    return y.reshape(Bn, L, D)


def hybrid_layer(x, pos, conv_left, past_k, past_v, l, g_mix, w_in, b_in, w_dw, b_dw, ln_g, ln_b,
                 w_conv_out, lam_q1, lam_k1, lam_q2, lam_k2, subln_g, w_attn_out, w_o, g_ffn,
                 w_router, b_router, w_moe1, b_moe1, w_moe2, b_moe2):
    Bn, L, _ = x.shape
    h = rmsnorm(x, g_mix)
    z = h @ w_in + b_in
    cuts = np.cumsum([2 * C_CONV, ATTN_W, ATTN_W, ATTN_W]).tolist()
    u, q, k, v, gl = jnp.split(z, cuts, axis=-1)
    a = u[..., :C_CONV] * jax.nn.sigmoid(u[..., C_CONV:])
    ext = jnp.concatenate([conv_left.astype(a.dtype), a], axis=1)
    c = depthwise_causal(ext, w_dw, b_dw)
    c = jax.nn.silu(layernorm(c, ln_g, ln_b))
    conv_out = c @ w_conv_out
    new_conv = ext[:, -(CONV_WIDTH - 1):]
    q = rope_partial(q.reshape(Bn, L, N_HEADS, 2, HEAD_DIM), pos)
    k = rope_partial(k.reshape(Bn, L, N_HEADS, 2, HEAD_DIM), pos)
    v = v.reshape(Bn, L, N_HEADS, 2 * HEAD_DIM)
    k_rows = k.reshape(Bn, L, N_HEADS, 2 * HEAD_DIM)
    lam_init = 0.8 - 0.6 * math.exp(-0.3 * l)
    lam = (jnp.exp(jnp.sum(lam_q1.astype(jnp.float32) * lam_k1.astype(jnp.float32)))
           - jnp.exp(jnp.sum(lam_q2.astype(jnp.float32) * lam_k2.astype(jnp.float32))) + lam_init)
    if past_k is None:
        o = attn_prompt(q, k, v, lam)
    else:
        k_all = jnp.concatenate([past_k.astype(k_rows.dtype), k_rows], axis=1)
        v_all = jnp.concatenate([past_v.astype(v.dtype), v], axis=1)
        Lk = k_all.shape[1]
        o = diff_attn_core(q, k_all.reshape(Bn, Lk, N_HEADS, 2, HEAD_DIM), v_all, pos, jnp.arange(Lk), lam)
    o = rmsnorm(o, subln_g) * (1.0 - lam_init)
    attn_out = o.reshape(Bn, L, ATTN_W) @ w_attn_out
    g = jax.nn.sigmoid(gl)
    merged = g[..., :D_MODEL] * conv_out + g[..., D_MODEL:] * attn_out
    x = x + merged @ w_o
    x = x + moe(rmsnorm(x, g_ffn), w_router, b_router, w_moe1, b_moe1, w_moe2, b_moe2)
    return x, k_rows, v, new_conv


def setup_inputs(seed: int = 0) -> dict:
    key = jax.random.key(seed)
    ks = jax.random.split(key, 32)
    n_pages = PAST_LEN // PAGE_SIZE
    n_pool = (DEC_BATCH * n_pages * 5) // 4
    f32 = jnp.float32
    nrm = lambda k, shape, s: jax.random.normal(k, shape, f32) * s
    page_table = jax.random.permutation(ks[5], n_pool)[:DEC_BATCH * n_pages].reshape(DEC_BATCH, n_pages).astype(jnp.int32)
    return {
        'x_prompt': nrm(ks[0], (BATCH, SEQ, D_MODEL), 1.0),
        'x_sample': nrm(ks[1], (DEC_BATCH, DEC_SEQ, D_MODEL), 1.0),
        'cache_k': nrm(ks[2], (DEPTH, n_pool, PAGE_SIZE, N_HEADS, 2 * HEAD_DIM), 1.0),
        'cache_v': nrm(ks[3], (DEPTH, n_pool, PAGE_SIZE, N_HEADS, 2 * HEAD_DIM), 1.0),
        'state_conv': nrm(ks[4], (DEPTH, DEC_BATCH, CONV_WIDTH - 1, C_CONV), 0.5),
        'page_table': page_table,
        'g_mix': 1.0 + nrm(ks[6], (DEPTH, D_MODEL), 0.01),
        'w_in': nrm(ks[7], (DEPTH, D_MODEL, N_IN), D_MODEL ** -0.5),
        'b_in': nrm(ks[8], (DEPTH, N_IN), 0.01),
        'w_dw': nrm(ks[9], (DEPTH, CONV_WIDTH, C_CONV), CONV_WIDTH ** -0.5),
        'b_dw': nrm(ks[10], (DEPTH, C_CONV), 0.01),
        'ln_g': 1.0 + nrm(ks[11], (DEPTH, C_CONV), 0.01),
        'ln_b': nrm(ks[12], (DEPTH, C_CONV), 0.01),
        'w_conv_out': nrm(ks[13], (DEPTH, C_CONV, D_MODEL), C_CONV ** -0.5),
        'lam_q1': nrm(ks[14], (DEPTH, HEAD_DIM), 0.1),
        'lam_k1': nrm(ks[15], (DEPTH, HEAD_DIM), 0.1),
        'lam_q2': nrm(ks[16], (DEPTH, HEAD_DIM), 0.1),
        'lam_k2': nrm(ks[17], (DEPTH, HEAD_DIM), 0.1),
        'subln_g': 1.0 + nrm(ks[18], (DEPTH, 2 * HEAD_DIM), 0.01),
        'w_attn_out': nrm(ks[19], (DEPTH, ATTN_W, D_MODEL), ATTN_W ** -0.5),
        'w_o': nrm(ks[20], (DEPTH, D_MODEL, D_MODEL), D_MODEL ** -0.5),
        'g_ffn': 1.0 + nrm(ks[21], (DEPTH, D_MODEL), 0.01),
        'w_router': nrm(ks[22], (DEPTH, D_MODEL, N_EXPERTS), D_MODEL ** -0.5),
        'b_router': nrm(ks[23], (DEPTH, N_EXPERTS), 0.01),
        'w_moe1': nrm(ks[24], (DEPTH, N_EXPERTS, D_MODEL, 2 * D_FF), D_MODEL ** -0.5),
        'b_moe1': nrm(ks[25], (DEPTH, N_EXPERTS, 2 * D_FF), 0.01),
        'w_moe2': nrm(ks[26], (DEPTH, N_EXPERTS, D_FF, D_MODEL), D_FF ** -0.5),
        'b_moe2': nrm(ks[27], (DEPTH, N_EXPERTS, D_MODEL), 0.01),
        'g_final': 1.0 + nrm(ks[28], (D_MODEL,), 0.01),
    }


def reference(x_prompt, x_sample, cache_k, cache_v, state_conv, page_table, g_mix, w_in, b_in, w_dw, b_dw,
              ln_g, ln_b, w_conv_out, lam_q1, lam_k1, lam_q2, lam_k2, subln_g, w_attn_out, w_o, g_ffn,
              w_router, b_router, w_moe1, b_moe1, w_moe2, b_moe2, g_final):
    pos_p = jnp.arange(x_prompt.shape[1])
    pos_s = PAST_LEN + jnp.arange(x_sample.shape[1])
    n_past = page_table.shape[1] * PAGE_SIZE
    hp, hs = x_prompt, x_sample
    kp, vp, cp, ksl, vsl, csl = [], [], [], [], [], []
    for l in range(DEPTH):
        lw = (g_mix[l], w_in[l], b_in[l], w_dw[l], b_dw[l], ln_g[l], ln_b[l], w_conv_out[l],
              lam_q1[l], lam_k1[l], lam_q2[l], lam_k2[l], subln_g[l], w_attn_out[l], w_o[l], g_ffn[l],
              w_router[l], b_router[l], w_moe1[l], b_moe1[l], w_moe2[l], b_moe2[l])
        zeros_left = jnp.zeros((hp.shape[0], CONV_WIDTH - 1, C_CONV), hp.dtype)
        hp, k_new, v_new, c_new = hybrid_layer(hp, pos_p, zeros_left, None, None, l, *lw)
        kp.append(k_new); vp.append(v_new); cp.append(c_new)
        past_k = cache_k[l][page_table].reshape(hs.shape[0], n_past, N_HEADS, 2 * HEAD_DIM)
        past_v = cache_v[l][page_table].reshape(hs.shape[0], n_past, N_HEADS, 2 * HEAD_DIM)
        hs, k_new, v_new, c_new = hybrid_layer(hs, pos_s, state_conv[l], past_k, past_v, l, *lw)
        ksl.append(k_new); vsl.append(v_new); csl.append(c_new)
    y_prompt = rmsnorm(hp, g_final)
    y_sample = rmsnorm(hs, g_final)
    return (y_prompt, y_sample, jnp.stack(kp), jnp.stack(vp), jnp.stack(cp), jnp.stack(ksl), jnp.stack(vsl), jnp.stack(csl))
```

```python
import functools
import math

import jax
import jax.numpy as jnp
from jax import lax
from jax.experimental import pallas as pl
from jax.experimental.pallas import tpu as pltpu

F32 = jnp.float32
BF16 = jnp.bfloat16
I32 = jnp.int32
U32 = jnp.uint32

N_HEADS = 8
HEAD_DIM = 64
HEAD_W = 2 * HEAD_DIM
ROT_DIM = HEAD_DIM // 4
ROT_HALF = ROT_DIM // 2
ROPE_THETA = 500000.0
CONV_WIDTH = 31
CONV_HALO = 32
N_EXPERTS = 32
TOP_K = 4
SWIGLU_ALPHA = 1.702
SWIGLU_LIMIT = 7.0
RMS_EPS = 1e-5
LN_EPS = 1e-5
PAGE_SIZE = 128
LANES = 128
NEG_BIG = -1e30

ROW_TILE = 512
ATTN_BLOCK = 512
CONV_ROWS = 1024
CONV_CHUNK = 64
MOE_TILE = 256
POS_TILE = 512
COMBINE_TILE = 256
VMEM_LIMIT = 56 * 1024 * 1024


def _cparams(n_axes, vmem=VMEM_LIMIT):
    return pltpu.CompilerParams(dimension_semantics=("arbitrary",) * n_axes,
                                vmem_limit_bytes=vmem)


def _rms(x, g):
    return x * lax.rsqrt(jnp.mean(x * x, axis=-1, keepdims=True) + RMS_EPS) * g


def _columns(cols):
    rows = cols[0].shape[0]
    lane = lax.broadcasted_iota(I32, (rows, len(cols)), 1)
    out = jnp.zeros((rows, len(cols)), cols[0].dtype)
    for k, col in enumerate(cols):
        out = jnp.where(lane == k, col, out)
    return out


def _lam(lq1, lk1, lq2, lk2, lam_init):
    s1 = jnp.sum(lq1[...] * lk1[...], axis=-1, keepdims=True)
    s2 = jnp.sum(lq2[...] * lk2[...], axis=-1, keepdims=True)
    return jnp.exp(s1) - jnp.exp(s2) + lam_init


def _rope_tables(pos):
    inv = jnp.power(jnp.float32(ROPE_THETA),
                    -jnp.arange(ROT_HALF, dtype=F32) * (2.0 / ROT_DIM))
    ang = pos.astype(F32)[:, None] * inv[None, :]
    cos, sin = jnp.cos(ang), jnp.sin(ang)
    n = pos.shape[0]
    rest = HEAD_DIM - ROT_DIM
    zh = jnp.zeros((n, ROT_HALF), F32)
    c64 = jnp.concatenate([cos, cos, jnp.ones((n, rest), F32)], axis=-1)
    sa64 = jnp.concatenate([-sin, zh, jnp.zeros((n, rest), F32)], axis=-1)
    sb64 = jnp.concatenate([zh, sin, jnp.zeros((n, rest), F32)], axis=-1)
    tile = lambda t: jnp.concatenate([t, t], axis=-1)
    return tile(c64), tile(sa64), tile(sb64)


def _in_proj_body(x_ref, g_ref, w_ref, b_ref, cos_ref, sa_ref, sb_ref,
                  a_ref, q_ref, k_ref, v_ref, gate_ref, *, c_conv, attn_w, d_model):
    h = _rms(x_ref[...], g_ref[...]).astype(BF16)
    cos, sa, sb = cos_ref[...], sa_ref[...], sb_ref[...]
    ch = 512

    def proj(c0):
        return (jnp.dot(h, w_ref[:, c0:c0 + ch], preferred_element_type=F32)
                + b_ref[:, c0:c0 + ch])

    def rope(z):
        outs = []
        for j in range(ch // LANES):
            zj = z[:, j * LANES:(j + 1) * LANES]
            outs.append(zj * cos + pltpu.roll(zj, LANES - ROT_HALF, 1) * sa
                        + pltpu.roll(zj, ROT_HALF, 1) * sb)
        return jnp.concatenate(outs, axis=-1)

    for c in range(0, c_conv, ch):
        a_ref[:, c:c + ch] = proj(c) * jax.nn.sigmoid(proj(c_conv + c))
    base = 2 * c_conv
    for c in range(0, attn_w, ch):
        q_ref[:, c:c + ch] = (rope(proj(base + c)) * (HEAD_DIM ** -0.5)).astype(BF16)
    base += attn_w
    for c in range(0, attn_w, ch):
        k_ref[:, c:c + ch] = rope(proj(base + c))
    base += attn_w
    for c in range(0, attn_w, ch):
        v_ref[:, c:c + ch] = proj(base + c)
    base += attn_w
    for c in range(0, 2 * d_model, ch):
        gate_ref[:, c:c + ch] = jax.nn.sigmoid(proj(base + c))


def _in_proj(x, g_mix, w_in_bf, b_in, tables, n_tab_blocks, c_conv, attn_w):
    t, d = x.shape
    n_in = w_in_bf.shape[1]
    tm = ROW_TILE
    row = lambda i: (i, 0)
    const = lambda i: (0, 0)
    tab = lambda i: (i % n_tab_blocks, 0)
    body = functools.partial(_in_proj_body, c_conv=c_conv, attn_w=attn_w, d_model=d)
    return pl.pallas_call(
        body,
        grid=(t // tm,),
        in_specs=[
            pl.BlockSpec((tm, d), row),
            pl.BlockSpec((1, d), const),
            pl.BlockSpec((d, n_in), const, pipeline_mode=pl.Buffered(1)),
            pl.BlockSpec((1, n_in), const),
            pl.BlockSpec((tm, LANES), tab),
            pl.BlockSpec((tm, LANES), tab),
            pl.BlockSpec((tm, LANES), tab),
        ],
        out_specs=[
            pl.BlockSpec((tm, c_conv), row),
            pl.BlockSpec((tm, attn_w), row),
            pl.BlockSpec((tm, attn_w), row),
            pl.BlockSpec((tm, attn_w), row),
            pl.BlockSpec((tm, 2 * d), row),
        ],
        out_shape=[
            jax.ShapeDtypeStruct((t, c_conv), F32),
            jax.ShapeDtypeStruct((t, attn_w), BF16),
            jax.ShapeDtypeStruct((t, attn_w), F32),
            jax.ShapeDtypeStruct((t, attn_w), F32),
            jax.ShapeDtypeStruct((t, 2 * d), F32),
        ],
        compiler_params=_cparams(1),
        name="in_proj",
    )(x, g_mix, w_in_bf, b_in, *tables)


def _conv_prompt_body(a_ref, halo_ref, w_ref, b_ref, o_ref, ext_ref, *, rows):
    i = pl.program_id(2)
    keep = jnp.where(i > 0, 1.0, 0.0).astype(F32)
    ext_ref[0:CONV_HALO, :] = halo_ref[0] * keep
    ext_ref[CONV_HALO:, :] = a_ref[0]
    w = w_ref[...]
    bias = jnp.broadcast_to(b_ref[...], (CONV_CHUNK, LANES))
    off = CONV_HALO - (CONV_WIDTH - 1)
    for r0 in range(0, rows, CONV_CHUNK):
        acc = bias
        for j in range(CONV_WIDTH):
            acc = acc + w[j:j + 1, :] * ext_ref[r0 + off + j:r0 + off + j + CONV_CHUNK, :]
        o_ref[0, r0:r0 + CONV_CHUNK, :] = acc


def _conv_prompt(a3, w_dw, b_dw):
    bsz, seq, c = a3.shape
    rows = CONV_ROWS
    hb = rows // CONV_HALO
    body = functools.partial(_conv_prompt_body, rows=rows)
    return pl.pallas_call(
        body,
        grid=(bsz, c // LANES, seq // rows),
        in_specs=[
            pl.BlockSpec((1, rows, LANES), lambda b, g, i: (b, i, g)),
            pl.BlockSpec((1, CONV_HALO, LANES),
                         lambda b, g, i: (b, jnp.maximum(i * hb - 1, 0), g)),
            pl.BlockSpec((CONV_WIDTH, LANES), lambda b, g, i: (0, g)),
            pl.BlockSpec((1, LANES), lambda b, g, i: (0, g)),
        ],
        out_specs=pl.BlockSpec((1, rows, LANES), lambda b, g, i: (b, i, g)),
        out_shape=jax.ShapeDtypeStruct((bsz, seq, c), F32),
        scratch_shapes=[pltpu.VMEM((rows + CONV_HALO, LANES), F32)],
        compiler_params=_cparams(3),
        name="conv_prompt",
    )(a3, a3, w_dw, b_dw)


def _conv_decode_body(state_ref, a_ref, w_ref, b_ref, o_ref, ext_ref, *, n_state, n_new):
    ext_ref[:, 0:n_state, :] = state_ref[...]
    ext_ref[:, n_state:n_state + n_new, :] = a_ref[...]
    w = w_ref[...]
    for t in range(n_new):
        win = ext_ref[:, t:t + CONV_WIDTH, :]
        o_ref[:, t:t + 1, :] = (jnp.sum(win * w[None], axis=1, keepdims=True)
                                + b_ref[...][None])


def _conv_decode(state, a3, w_dw, b_dw):
    bsz, n_state, c = state.shape
    n_new = a3.shape[1]
    bb = 8
    body = functools.partial(_conv_decode_body, n_state=n_state, n_new=n_new)
    return pl.pallas_call(
        body,
        grid=(bsz // bb,),
        in_specs=[
            pl.BlockSpec((bb, n_state, c), lambda i: (i, 0, 0)),
            pl.BlockSpec((bb, n_new, c), lambda i: (i, 0, 0)),
            pl.BlockSpec((CONV_WIDTH, c), lambda i: (0, 0)),
            pl.BlockSpec((1, c), lambda i: (0, 0)),
        ],
        out_specs=pl.BlockSpec((bb, n_new, c), lambda i: (i, 0, 0)),
        out_shape=jax.ShapeDtypeStruct((bsz, n_new, c), F32),
        scratch_shapes=[pltpu.VMEM((bb, n_state + n_new + 6, c), F32)],
        compiler_params=_cparams(1),
        name="conv_decode",
    )(state, a3, w_dw, b_dw)


def _attn_prompt_body(lq1, lk1, lq2, lk2, subg_ref, q_ref, k_ref, v_ref, o_ref,
                      kt_ref, vb_ref, acc1, acc2, m1, l1, m2, l2, *, blk, n_blk, lam_init):
    i = pl.program_id(2)

    @pl.when(i == 0)
    def _():
        for c in range(n_blk):
            kt_ref[c] = k_ref[0, c * blk:(c + 1) * blk, :].T.astype(BF16)
            vb_ref[c] = v_ref[0, c * blk:(c + 1) * blk, :].astype(BF16)

    q = q_ref[0]
    q1 = q[:, :HEAD_DIM]
    q2 = q[:, HEAD_DIM:]
    for m_ref, l_ref, acc_ref in ((m1, l1, acc1), (m2, l2, acc2)):
        m_ref[...] = jnp.full(m_ref.shape, NEG_BIG, F32)
        l_ref[...] = jnp.zeros(l_ref.shape, F32)
        acc_ref[...] = jnp.zeros(acc_ref.shape, F32)

    def step(kb, masked):
        kt = kt_ref[kb]
        vv = vb_ref[kb]
        if masked:
            row = lax.broadcasted_iota(I32, (blk, blk), 0)
            col = lax.broadcasted_iota(I32, (blk, blk), 1)
            causal = col <= row
        for qq, kk, m_ref, l_ref, acc_ref in (
                (q1, kt[:HEAD_DIM], m1, l1, acc1), (q2, kt[HEAD_DIM:], m2, l2, acc2)):
            s = jnp.dot(qq, kk, preferred_element_type=F32)
            if masked:
                s = jnp.where(causal, s, NEG_BIG)
            m_old = m_ref[...]
            m_new = jnp.maximum(m_old, jnp.max(s, axis=-1, keepdims=True))
            alpha = jnp.exp(m_old - m_new)
            p = jnp.exp(s - m_new)
            l_ref[...] = alpha * l_ref[...] + jnp.sum(p, axis=-1, keepdims=True)
            acc_ref[...] = alpha * acc_ref[...] + jnp.dot(
                p.astype(BF16), vv, preferred_element_type=F32)
            m_ref[...] = m_new

    def loop_body(kb, carry):
        step(kb, False)
        return carry

    lax.fori_loop(0, i, loop_body, 0)
    step(i, True)

    lam = _lam(lq1, lk1, lq2, lk2, lam_init)
    o = acc1[...] / l1[...] - lam * (acc2[...] / l2[...])
    o_ref[0] = (_rms(o, subg_ref[...]) * (1.0 - lam_init)).astype(BF16)


def _attn_prompt(q3, k3, v3, lams, subln_g, lam_init):
    bsz, seq, _ = q3.shape
    blk = ATTN_BLOCK
    n_blk = seq // blk
    vec = pl.BlockSpec((1, HEAD_DIM), lambda b, h, i: (0, 0))
    body = functools.partial(_attn_prompt_body, blk=blk, n_blk=n_blk, lam_init=lam_init)
    return pl.pallas_call(
        body,
        grid=(bsz, N_HEADS, n_blk),
        in_specs=[
            vec, vec, vec, vec,
            pl.BlockSpec((1, HEAD_W), lambda b, h, i: (0, 0)),
            pl.BlockSpec((1, blk, HEAD_W), lambda b, h, i: (b, i, h)),
            pl.BlockSpec((1, seq, HEAD_W), lambda b, h, i: (b, 0, h)),
            pl.BlockSpec((1, seq, HEAD_W), lambda b, h, i: (b, 0, h)),
        ],
        out_specs=pl.BlockSpec((1, blk, HEAD_W), lambda b, h, i: (b, i, h)),
        out_shape=jax.ShapeDtypeStruct(q3.shape, BF16),
        scratch_shapes=[
            pltpu.VMEM((n_blk, HEAD_W, blk), BF16),
            pltpu.VMEM((n_blk, blk, HEAD_W), BF16),
            pltpu.VMEM((blk, HEAD_W), F32),
            pltpu.VMEM((blk, HEAD_W), F32),
            pltpu.VMEM((blk, 1), F32),
            pltpu.VMEM((blk, 1), F32),
            pltpu.VMEM((blk, 1), F32),
            pltpu.VMEM((blk, 1), F32),
        ],
        compiler_params=_cparams(3),
        name="attn_prompt",
    )(*lams, subln_g, q3, k3, v3)


NEW_PAD = 16
Q_PAD = 8


def _attn_decode_body(pt_ref, lq1, lk1, lq2, lk2, subg_ref, q_ref, kn_ref, vn_ref, *rest,
                      n_pages, n_new, lam_init):
    del pt_ref
    kpages = rest[:n_pages]
    vpages = rest[n_pages:2 * n_pages]
    o_ref, kbuf, vbuf, stage = rest[2 * n_pages:]
    n_past = n_pages * PAGE_SIZE
    width = kbuf.shape[1]

    for j in range(n_pages):
        kbuf[j * PAGE_SIZE:(j + 1) * PAGE_SIZE, :] = kpages[j][0].astype(BF16)
        vbuf[j * PAGE_SIZE:(j + 1) * PAGE_SIZE, :] = vpages[j][0].astype(BF16)

    def padded_rows(rows_f32):
        stage[...] = jnp.zeros(stage.shape, F32)
        stage[0:n_new, :] = rows_f32
        return stage[...].astype(BF16)

    kbuf[n_past:, :] = padded_rows(kn_ref[0])
    vbuf[n_past:, :] = padded_rows(vn_ref[0])

    q16 = padded_rows(q_ref[0])
    sel_r = lax.broadcasted_iota(I32, (NEW_PAD, LANES), 0)
    sel_c = lax.broadcasted_iota(I32, (NEW_PAD, LANES), 1)
    sel = (sel_c % Q_PAD == sel_r).astype(BF16)
    qrep = lax.dot_general(q16, sel, (((0,), (0,)), ((), ())),
                           preferred_element_type=F32)
    rr = lax.broadcasted_iota(I32, (width, LANES), 0)
    cc = lax.broadcasted_iota(I32, (width, LANES), 1)
    qblk = jnp.where(rr // HEAD_DIM == cc // Q_PAD, qrep, 0.0).astype(BF16)

    s = jnp.dot(kbuf[...], qblk, preferred_element_type=F32)
    nk = n_past + NEW_PAD
    key = lax.broadcasted_iota(I32, (nk, LANES), 0)
    slot = lax.broadcasted_iota(I32, (nk, LANES), 1) % Q_PAD
    new_idx = key - n_past
    valid = (key < n_past) | ((new_idx <= slot) & (new_idx < n_new))
    s = jnp.where(valid, s, NEG_BIG)
    m = jnp.max(s, axis=0, keepdims=True)
    e = jnp.exp(s - m)
    p = e / jnp.sum(e, axis=0, keepdims=True)
    lam = _lam(lq1, lk1, lq2, lk2, lam_init)
    a = p - lam * pltpu.roll(p, LANES - Q_PAD, 1)
    lane = lax.broadcasted_iota(I32, (nk, LANES), 1)
    a = jnp.where((lane // Q_PAD) % 2 == 0, a, 0.0).astype(BF16)
    o_full = lax.dot_general(a, vbuf[...], (((0,), (0,)), ((), ())),
                             preferred_element_type=F32)
    outs = []
    for h in range(N_HEADS):
        oh = o_full[h * 2 * Q_PAD:h * 2 * Q_PAD + Q_PAD, h * HEAD_W:(h + 1) * HEAD_W]
        outs.append(_rms(oh, subg_ref[...]) * (1.0 - lam_init))
    o_ref[0] = jnp.concatenate(outs, axis=-1)[:n_new]


def _attn_decode(page_table, q3, kn3, vn3, cache_k, cache_v, lams, subln_g, lam_init):
    bsz, n_new, width = q3.shape
    n_pages = page_table.shape[1]
    vec = pl.BlockSpec((1, HEAD_DIM), lambda b, pt: (0, 0))
    per_b = pl.BlockSpec((1, n_new, width), lambda b, pt: (b, 0, 0))

    def page_spec(j):
        return pl.BlockSpec((1, PAGE_SIZE, width), lambda b, pt: (pt[b, j], 0, 0))

    body = functools.partial(_attn_decode_body, n_pages=n_pages, n_new=n_new,
                             lam_init=lam_init)
    nk = n_pages * PAGE_SIZE + NEW_PAD
    grid_spec = pltpu.PrefetchScalarGridSpec(
        num_scalar_prefetch=1,
        grid=(bsz,),
        in_specs=([vec, vec, vec, vec, pl.BlockSpec((1, HEAD_W), lambda b, pt: (0, 0)),
                   per_b, per_b, per_b]
                  + [page_spec(j) for j in range(n_pages)]
                  + [page_spec(j) for j in range(n_pages)]),
        out_specs=per_b,
        scratch_shapes=[pltpu.VMEM((nk, width), BF16), pltpu.VMEM((nk, width), BF16),
                        pltpu.VMEM((NEW_PAD, width), F32)],
    )
    return pl.pallas_call(
        body,
        grid_spec=grid_spec,
        out_shape=jax.ShapeDtypeStruct(q3.shape, F32),
        compiler_params=_cparams(1),
        name="attn_decode",
    )(page_table, *lams, subln_g, q3, kn3, vn3,
      *([cache_k] * n_pages), *([cache_v] * n_pages))


def _post_body(x_ref, conv_ref, o_ref, gate_ref, lng, lnb, wc, wa, wo, gffn, wr, br,
               x1_ref, hp_ref, ids_ref, tw_ref, *, d_model):
    c = conv_ref[...]
    mu = jnp.mean(c, axis=-1, keepdims=True)
    xc = c - mu
    cn = xc * lax.rsqrt(jnp.mean(xc * xc, axis=-1, keepdims=True) + LN_EPS) * lng[...] + lnb[...]
    cact = (cn * jax.nn.sigmoid(cn)).astype(BF16)
    conv_out = jnp.dot(cact, wc[...], preferred_element_type=F32)
    attn_out = jnp.dot(o_ref[...], wa[...], preferred_element_type=F32)
    g = gate_ref[...]
    merged = (g[:, :d_model] * conv_out + g[:, d_model:] * attn_out).astype(BF16)
    x1 = x_ref[...] + jnp.dot(merged, wo[...], preferred_element_type=F32)
    x1_ref[...] = x1

    hb = _rms(x1, gffn[...]).astype(BF16)
    logits = jnp.dot(hb, wr[...], preferred_element_type=F32) + br[...]
    lane = lax.broadcasted_iota(I32, logits.shape, 1)
    vals, ids = [], []
    cur = logits
    for _ in range(TOP_K):
        mx = jnp.max(cur, axis=-1, keepdims=True)
        idx = jnp.min(jnp.where(cur == mx, lane, N_EXPERTS), axis=-1, keepdims=True)
        vals.append(mx)
        ids.append(idx)
        cur = jnp.where(lane == idx, -jnp.inf, cur)
    es = [jnp.exp(v - vals[0]) for v in vals]
    den = es[0] + es[1] + es[2] + es[3]
    tw_ref[...] = _columns([e / den for e in es])
    ids_ref[...] = _columns(ids)

    bits = pltpu.bitcast(hb.astype(F32), U32)
    half = d_model // 2
    hp_ref[...] = (bits[:, :half] >> 16) | (bits[:, half:] & jnp.uint32(0xFFFF0000))


def _post(x, conv, o, gate, ln_g, ln_b, wc, wa, wo, g_ffn, wr, br):
    t, d = x.shape
    tm = ROW_TILE
    row = lambda i: (i, 0)
    const = lambda i: (0, 0)
    mat = pl.BlockSpec((d, d), const)
    vec = pl.BlockSpec((1, d), const)
    body = functools.partial(_post_body, d_model=d)
    return pl.pallas_call(
        body,
        grid=(t // tm,),
        in_specs=[
            pl.BlockSpec((tm, d), row), pl.BlockSpec((tm, d), row), pl.BlockSpec((tm, d), row),
            pl.BlockSpec((tm, 2 * d), row),
            vec, vec, mat, mat, mat, vec,
            pl.BlockSpec((d, N_EXPERTS), const), pl.BlockSpec((1, N_EXPERTS), const),
        ],
        out_specs=[
            pl.BlockSpec((tm, d), row), pl.BlockSpec((tm, d // 2), row),
            pl.BlockSpec((tm, TOP_K), row), pl.BlockSpec((tm, TOP_K), row),
        ],
        out_shape=[
            jax.ShapeDtypeStruct((t, d), F32), jax.ShapeDtypeStruct((t, d // 2), U32),
            jax.ShapeDtypeStruct((t, TOP_K), I32), jax.ShapeDtypeStruct((t, TOP_K), F32),
        ],
        compiler_params=_cparams(1),
        name="post",
    )(x, conv, o, gate, ln_g, ln_b, wc, wa, wo, g_ffn, wr, br)


def _lane_cumsum(x):
    lane = lax.broadcasted_iota(I32, x.shape, 1)
    s = 1
    while s < LANES:
        x = x + jnp.where(lane >= s, pltpu.roll(x, s, 1), 0.0)
        s *= 2
    return x


def _positions_body(ids_ref, pos_ref, te_ref, count_ref, start_ref, *, n_tiles_pad):
    p = pl.program_id(0)
    i = pl.program_id(1)
    ids = ids_ref[...]
    tt = ids.shape[0]
    lane = lax.broadcasted_iota(I32, (tt, LANES), 1)
    onehots = [ids[:, k:k + 1] == lane for k in range(TOP_K)]
    tile_counts = [jnp.sum(oh.astype(F32), axis=0, keepdims=True) for oh in onehots]
    tile_total = tile_counts[0] + tile_counts[1] + tile_counts[2] + tile_counts[3]

    @pl.when((p == 0) & (i == 0))
    def _():
        count_ref[...] = jnp.zeros(count_ref.shape, F32)

    @pl.when((p == 1) & (i == 0))
    def _():
        counts = count_ref[...]
        padded = jnp.ceil(counts * (1.0 / MOE_TILE)) * MOE_TILE
        ends = _lane_cumsum(padded)
        start_ref[...] = ends - padded
        count_ref[...] = jnp.zeros(count_ref.shape, F32)
        tile_start = (lax.broadcasted_iota(I32, (n_tiles_pad, LANES), 0) * MOE_TILE).astype(F32)
        elane = lax.broadcasted_iota(I32, (n_tiles_pad, LANES), 1)
        done = (ends[0:1, :] <= tile_start) & (elane < N_EXPERTS)
        n_done = jnp.sum(done.astype(F32), axis=-1, keepdims=True)
        te_ref[...] = jnp.broadcast_to(n_done, (n_tiles_pad, LANES)).astype(I32)

    @pl.when(p == 1)
    def _():
        r = lax.broadcasted_iota(I32, (tt, tt), 0)
        c = lax.broadcasted_iota(I32, (tt, tt), 1)
        earlier = (c < r).astype(BF16)
        run = start_ref[0:1, :] + count_ref[0:1, :]
        cols = []
        for k in range(TOP_K):
            within = jnp.dot(earlier, onehots[k].astype(BF16), preferred_element_type=F32)
            cols.append(jnp.sum(jnp.where(onehots[k], within + run, 0.0),
                                axis=-1, keepdims=True))
            run = run + tile_counts[k]
        pos_ref[...] = _columns(cols).astype(I32)

    count_ref[...] = count_ref[...] + tile_total


def _positions(ids, n_tiles_pad):
    t = ids.shape[0]
    tt = POS_TILE
    body = functools.partial(_positions_body, n_tiles_pad=n_tiles_pad)
    return pl.pallas_call(
        body,
        grid=(2, t // tt),
        in_specs=[pl.BlockSpec((tt, TOP_K), lambda p, i: (i, 0))],
        out_specs=[
            pl.BlockSpec((tt, TOP_K), lambda p, i: (i * p, 0)),
            pl.BlockSpec((n_tiles_pad, LANES), lambda p, i: (0, 0)),
        ],
        out_shape=[
            jax.ShapeDtypeStruct((t, TOP_K), I32),
            jax.ShapeDtypeStruct((n_tiles_pad, LANES), I32),
        ],
        scratch_shapes=[pltpu.VMEM((8, LANES), F32), pltpu.VMEM((8, LANES), F32)],
        compiler_params=_cparams(2),
        name="moe_positions",
    )(ids)


def _dispatch_body(pos_ref, hp_ref, xs_in, xs_out, sem):
    del xs_in
    tt = hp_ref.shape[0]

    def row_copy(r, k):
        return pltpu.make_async_copy(hp_ref.at[pl.ds(r, 1)],
                                     xs_out.at[pl.ds(pos_ref[r * TOP_K + k], 1)], sem)

    def issue(r, carry):
        for k in range(TOP_K):
            row_copy(r, k).start()
        return carry

    lax.fori_loop(0, tt, issue, 0)

    def drain(r, carry):
        for k in range(TOP_K):
            row_copy(r, k).wait()
        return carry

    lax.fori_loop(0, tt, drain, 0)


def _dispatch(pos_flat, hp, xs):
    t, half = hp.shape
    tt = ROW_TILE
    return pl.pallas_call(
        _dispatch_body,
        grid=(t // tt,),
        in_specs=[
            pl.BlockSpec((tt * TOP_K,), lambda i: (i,), memory_space=pltpu.SMEM),
            pl.BlockSpec((tt, half), lambda i: (i, 0)),
            pl.BlockSpec(memory_space=pl.ANY),
        ],
        out_specs=pl.BlockSpec(memory_space=pl.ANY),
        out_shape=jax.ShapeDtypeStruct(xs.shape, xs.dtype),
        scratch_shapes=[pltpu.SemaphoreType.DMA],
        input_output_aliases={2: 0},
        compiler_params=_cparams(1),
        name="moe_dispatch",
    )(pos_flat, hp, xs)


def _experts_body(te_ref, nv_ref, xs_ref, w1_ref, b1_ref, w2_ref, b2_ref, ys_ref, *, d_ff):
    del te_ref
    j = pl.program_id(0)

    @pl.when(j < nv_ref[0])
    def _():
        bits = xs_ref[...]
        half = bits.shape[1]
        x_lo = pltpu.bitcast(bits << 16, F32).astype(BF16)
        x_hi = pltpu.bitcast(bits & jnp.uint32(0xFFFF0000), F32).astype(BF16)
        z = (jnp.dot(x_lo, w1_ref[0, :half, :], preferred_element_type=F32)
             + jnp.dot(x_hi, w1_ref[0, half:, :], preferred_element_type=F32)
             + b1_ref[0])
        gate = jnp.minimum(z[:, :d_ff], SWIGLU_LIMIT)
        lin = jnp.clip(z[:, d_ff:], -SWIGLU_LIMIT, SWIGLU_LIMIT)
        act = (gate * jax.nn.sigmoid(SWIGLU_ALPHA * gate) * (lin + 1.0)).astype(BF16)
        ys_ref[...] = jnp.dot(act, w2_ref[0], preferred_element_type=F32) + b2_ref[0]

    @pl.when(j >= nv_ref[0])
    def _():
        ys_ref[...] = jnp.zeros(ys_ref.shape, F32)


def _experts(tile_expert, n_valid, xs, w1, b1, w2, b2):
    n_rows, half = xs.shape
    d = 2 * half
    d_ff = w2.shape[1]
    tm = MOE_TILE
    body = functools.partial(_experts_body, d_ff=d_ff)
    grid_spec = pltpu.PrefetchScalarGridSpec(
        num_scalar_prefetch=2,
        grid=(n_rows // tm,),
        in_specs=[
            pl.BlockSpec((tm, half), lambda j, te, nv: (jnp.minimum(j, nv[0] - 1), 0)),
            pl.BlockSpec((1, d, 2 * d_ff), lambda j, te, nv: (te[j], 0, 0)),
            pl.BlockSpec((1, 1, 2 * d_ff), lambda j, te, nv: (te[j], 0, 0)),
            pl.BlockSpec((1, d_ff, d), lambda j, te, nv: (te[j], 0, 0)),
            pl.BlockSpec((1, 1, d), lambda j, te, nv: (te[j], 0, 0)),
        ],
        out_specs=pl.BlockSpec((tm, d), lambda j, te, nv: (j, 0)),
    )
    return pl.pallas_call(
        body,
        grid_spec=grid_spec,
        out_shape=jax.ShapeDtypeStruct((n_rows, d), F32),
        compiler_params=_cparams(1),
        name="moe_experts",
    )(tile_expert, n_valid, xs, w1, b1, w2, b2)


def _combine_body(pos_ref, tw_ref, x1_ref, gf_ref, ys_hbm, y_ref, rows_ref, sem):
    tt = x1_ref.shape[0]

    def row_copy(r, k):
        return pltpu.make_async_copy(ys_hbm.at[pl.ds(pos_ref[r * TOP_K + k], 1)],
                                     rows_ref.at[k, pl.ds(r, 1)], sem)

    def issue(r, carry):
        for k in range(TOP_K):
            row_copy(r, k).start()
        return carry

    lax.fori_loop(0, tt, issue, 0)

    def drain(r, carry):
        for k in range(TOP_K):
            row_copy(r, k).wait()
        return carry

    lax.fori_loop(0, tt, drain, 0)

    tw = tw_ref[...]
    moe = tw[:, 0:1] * rows_ref[0]
    for k in range(1, TOP_K):
        moe = moe + tw[:, k:k + 1] * rows_ref[k]
    y_ref[...] = _rms(x1_ref[...] + moe, gf_ref[...])


def _combine(pos_flat, tw, x1, g_final, ys):
    t, d = x1.shape
    tt = COMBINE_TILE
    return pl.pallas_call(
        _combine_body,
        grid=(t // tt,),
        in_specs=[
            pl.BlockSpec((tt * TOP_K,), lambda i: (i,), memory_space=pltpu.SMEM),
            pl.BlockSpec((tt, TOP_K), lambda i: (i, 0)),
            pl.BlockSpec((tt, d), lambda i: (i, 0)),
            pl.BlockSpec((1, d), lambda i: (0, 0)),
            pl.BlockSpec(memory_space=pl.ANY),
        ],
        out_specs=pl.BlockSpec((tt, d), lambda i: (i, 0)),
        out_shape=jax.ShapeDtypeStruct((t, d), F32),
        scratch_shapes=[pltpu.VMEM((TOP_K, tt, d), F32), pltpu.SemaphoreType.DMA],
        compiler_params=_cparams(1),
        name="moe_combine",
    )(pos_flat, tw, x1, g_final, ys)


def kernel(x_prompt, x_sample, cache_k, cache_v, state_conv, page_table, g_mix, w_in, b_in,
           w_dw, b_dw, ln_g, ln_b, w_conv_out, lam_q1, lam_k1, lam_q2, lam_k2, subln_g,
           w_attn_out, w_o, g_ffn, w_router, b_router, w_moe1, b_moe1, w_moe2, b_moe2, g_final):
    bsz, seq, d = x_prompt.shape
    dec_b, dec_s, _ = x_sample.shape
    depth = g_mix.shape[0]
    c_conv = w_dw.shape[2]
    attn_w = N_HEADS * HEAD_W
    n_pages = page_table.shape[1]
    past_len = n_pages * PAGE_SIZE
    t_p, t_s = bsz * seq, dec_b * dec_s
    t_all = t_p + t_s
    d_ff = w_moe2.shape[2]
    assert depth == 1, "the combine kernel fuses the final norm, so only one layer is supported"
    assert seq % ROW_TILE == 0 and t_s % ROW_TILE == 0 and seq % CONV_ROWS == 0
    assert seq % ATTN_BLOCK == 0 and t_all % POS_TILE == 0 and dec_b % 8 == 0
    assert t_p % COMBINE_TILE == 0 and t_s % COMBINE_TILE == 0

    tab_p = _rope_tables(jnp.arange(seq))
    tab_s = _rope_tables(jnp.tile(past_len + jnp.arange(dec_s), dec_b))
    n_rows = t_all * TOP_K + N_EXPERTS * MOE_TILE
    n_tiles = n_rows // MOE_TILE
    row2 = lambda v: v.reshape(1, -1)

    hp, hs = x_prompt.reshape(t_p, d), x_sample.reshape(t_s, d)
    outs = [[] for _ in range(6)]
    for l in range(depth):
        lam_init = 0.8 - 0.6 * math.exp(-0.3 * l)
        lams = (row2(lam_q1[l]), row2(lam_k1[l]), row2(lam_q2[l]), row2(lam_k2[l]))
        subg = row2(subln_g[l])
        w_in_bf = w_in[l].astype(BF16)
        wc, wa, wo = (w_conv_out[l].astype(BF16), w_attn_out[l].astype(BF16),
                      w_o[l].astype(BF16))
        wr = w_router[l].astype(BF16)
        w1, w2 = w_moe1[l].astype(BF16), w_moe2[l].astype(BF16)
        b1 = b_moe1[l].reshape(N_EXPERTS, 1, 2 * d_ff)
        b2 = b_moe2[l].reshape(N_EXPERTS, 1, d)
        proj_args = (row2(g_mix[l]), w_in_bf, row2(b_in[l]))
        post_args = (row2(ln_g[l]), row2(ln_b[l]), wc, wa, wo, row2(g_ffn[l]), wr,
                     row2(b_router[l]))

        a_p, q_p, k_p, v_p, gate_p = _in_proj(hp, *proj_args, tab_p, seq // ROW_TILE,
                                              c_conv, attn_w)
        a_p3 = a_p.reshape(bsz, seq, c_conv)
        conv_p = _conv_prompt(a_p3, w_dw[l], row2(b_dw[l]))
        o_p = _attn_prompt(q_p.reshape(bsz, seq, attn_w), k_p.reshape(bsz, seq, attn_w),
                           v_p.reshape(bsz, seq, attn_w), lams, subg, lam_init)
        x1_p, hpk_p, ids_p, tw_p = _post(hp, conv_p.reshape(t_p, c_conv),
                                         o_p.reshape(t_p, attn_w), gate_p, *post_args)

        a_s, q_s, k_s, v_s, gate_s = _in_proj(hs, *proj_args, tab_s, 1, c_conv, attn_w)
        a_s3 = a_s.reshape(dec_b, dec_s, c_conv)
        conv_s = _conv_decode(state_conv[l], a_s3, w_dw[l], row2(b_dw[l]))
        pool = cache_k.shape[1]
        o_s = _attn_decode(page_table, q_s.astype(F32).reshape(dec_b, dec_s, attn_w),
                           k_s.reshape(dec_b, dec_s, attn_w), v_s.reshape(dec_b, dec_s, attn_w),
                           cache_k[l].reshape(pool, PAGE_SIZE, attn_w),
                           cache_v[l].reshape(pool, PAGE_SIZE, attn_w), lams, subg, lam_init)
        x1_s, hpk_s, ids_s, tw_s = _post(hs, conv_s.reshape(t_s, c_conv),
                                         o_s.reshape(t_s, attn_w).astype(BF16), gate_s,
                                         *post_args)

        ids = jnp.concatenate([ids_p, ids_s], axis=0)
        pos, te = _positions(ids, n_tiles)
        pos_flat = pos.reshape(-1)
        tile_expert = jnp.minimum(te[:, 0], N_EXPERTS - 1)
        n_valid = jnp.sum((te[:, 0] < N_EXPERTS).astype(I32)).reshape(1)
        xs = jnp.zeros((n_rows, d // 2), U32)
        xs = _dispatch(pos_flat[:t_p * TOP_K], hpk_p, xs)
        xs = _dispatch(pos_flat[t_p * TOP_K:], hpk_s, xs)
        ys = _experts(tile_expert, n_valid, xs, w1, b1, w2, b2)
        gf = row2(g_final)
        hp = _combine(pos_flat[:t_p * TOP_K], tw_p, x1_p, gf, ys)
        hs = _combine(pos_flat[t_p * TOP_K:], tw_s, x1_s, gf, ys)

        outs[0].append(k_p.reshape(bsz, seq, N_HEADS, HEAD_W))
        outs[1].append(v_p.reshape(bsz, seq, N_HEADS, HEAD_W))
        outs[2].append(a_p3[:, seq - (CONV_WIDTH - 1):])
        outs[3].append(k_s.reshape(dec_b, dec_s, N_HEADS, HEAD_W))
        outs[4].append(v_s.reshape(dec_b, dec_s, N_HEADS, HEAD_W))
        outs[5].append(jnp.concatenate([state_conv[l], a_s3], axis=1)[:, -(CONV_WIDTH - 1):])

    y_prompt = hp.reshape(bsz, seq, d)
    y_sample = hs.reshape(dec_b, dec_s, d)
    return (y_prompt, y_sample) + tuple(jnp.stack(o) for o in outs)
```

```python
import functools
import math

import jax
import jax.numpy as jnp
from jax import lax
from jax.experimental import pallas as pl
from jax.experimental.pallas import tpu as pltpu

F32 = jnp.float32
BF16 = jnp.bfloat16
I32 = jnp.int32
U32 = jnp.uint32

N_HEADS = 8
HEAD_DIM = 64
HEAD_W = 2 * HEAD_DIM
ROT_DIM = HEAD_DIM // 4
ROT_HALF = ROT_DIM // 2
ROPE_THETA = 500000.0
CONV_WIDTH = 31
CONV_HALO = 32
N_EXPERTS = 32
TOP_K = 4
SWIGLU_ALPHA = 1.702
SWIGLU_LIMIT = 7.0
RMS_EPS = 1e-5
LN_EPS = 1e-5
PAGE_SIZE = 128
LANES = 128
NEG_BIG = -1e30

ROW_TILE = 512
ATTN_BLOCK = 512
ATTN_SUB = 128
CONV_ROWS = 1024
CONV_CHUNK = 64
MOE_TILE = 256
POS_TILE = 512
COMBINE_TILE = 256
VMEM_LIMIT = 56 * 1024 * 1024


def _cparams(n_axes, vmem=VMEM_LIMIT):
    return pltpu.CompilerParams(dimension_semantics=("arbitrary",) * n_axes,
                                vmem_limit_bytes=vmem)


def _rms(x, g):
    return x * lax.rsqrt(jnp.mean(x * x, axis=-1, keepdims=True) + RMS_EPS) * g


def _columns(cols):
    rows = cols[0].shape[0]
    lane = lax.broadcasted_iota(I32, (rows, len(cols)), 1)
    out = jnp.zeros((rows, len(cols)), cols[0].dtype)
    for k, col in enumerate(cols):
        out = jnp.where(lane == k, col, out)
    return out


def _lam(lq1, lk1, lq2, lk2, lam_init):
    s1 = jnp.sum(lq1[...] * lk1[...], axis=-1, keepdims=True)
    s2 = jnp.sum(lq2[...] * lk2[...], axis=-1, keepdims=True)
    return jnp.exp(s1) - jnp.exp(s2) + lam_init


def _rope_tables(pos):
    inv = jnp.power(jnp.float32(ROPE_THETA),
                    -jnp.arange(ROT_HALF, dtype=F32) * (2.0 / ROT_DIM))
    ang = pos.astype(F32)[:, None] * inv[None, :]
    cos, sin = jnp.cos(ang), jnp.sin(ang)
    n = pos.shape[0]
    rest = HEAD_DIM - ROT_DIM
    zh = jnp.zeros((n, ROT_HALF), F32)
    c64 = jnp.concatenate([cos, cos, jnp.ones((n, rest), F32)], axis=-1)
    sa64 = jnp.concatenate([-sin, zh, jnp.zeros((n, rest), F32)], axis=-1)
    sb64 = jnp.concatenate([zh, sin, jnp.zeros((n, rest), F32)], axis=-1)
    tile = lambda t: jnp.concatenate([t, t], axis=-1)
    return tile(c64), tile(sa64), tile(sb64)


def _in_proj_body(x_ref, g_ref, w_ref, b_ref, cos_ref, sa_ref, sb_ref,
                  a_ref, q_ref, k_ref, v_ref, gate_ref, *, c_conv, attn_w, d_model):
    h = _rms(x_ref[...], g_ref[...]).astype(BF16)
    cos, sa, sb = cos_ref[...], sa_ref[...], sb_ref[...]
    ch = 512

    def proj(c0):
        return (jnp.dot(h, w_ref[:, c0:c0 + ch], preferred_element_type=F32)
                + b_ref[:, c0:c0 + ch])

    def rope(z):
        outs = []
        for j in range(ch // LANES):
            zj = z[:, j * LANES:(j + 1) * LANES]
            outs.append(zj * cos + pltpu.roll(zj, LANES - ROT_HALF, 1) * sa
                        + pltpu.roll(zj, ROT_HALF, 1) * sb)
        return jnp.concatenate(outs, axis=-1)

    for c in range(0, c_conv, ch):
        a_ref[:, c:c + ch] = proj(c) * jax.nn.sigmoid(proj(c_conv + c))
    base = 2 * c_conv
    for c in range(0, attn_w, ch):
        q_ref[:, c:c + ch] = (rope(proj(base + c)) * (HEAD_DIM ** -0.5)).astype(BF16)
    base += attn_w
    for c in range(0, attn_w, ch):
        k_ref[:, c:c + ch] = rope(proj(base + c))
    base += attn_w
    for c in range(0, attn_w, ch):
        v_ref[:, c:c + ch] = proj(base + c)
    base += attn_w
    for c in range(0, 2 * d_model, ch):
        gate_ref[:, c:c + ch] = jax.nn.sigmoid(proj(base + c))


def _in_proj(x, g_mix, w_in_bf, b_in, tables, n_tab_blocks, c_conv, attn_w):
    t, d = x.shape
    n_in = w_in_bf.shape[1]
    tm = ROW_TILE
    row = lambda i: (i, 0)
    const = lambda i: (0, 0)
    tab = lambda i: (i % n_tab_blocks, 0)
    body = functools.partial(_in_proj_body, c_conv=c_conv, attn_w=attn_w, d_model=d)
    return pl.pallas_call(
        body,
        grid=(t // tm,),
        in_specs=[
            pl.BlockSpec((tm, d), row),
            pl.BlockSpec((1, d), const),
            pl.BlockSpec((d, n_in), const, pipeline_mode=pl.Buffered(1)),
            pl.BlockSpec((1, n_in), const),
            pl.BlockSpec((tm, LANES), tab),
            pl.BlockSpec((tm, LANES), tab),
            pl.BlockSpec((tm, LANES), tab),
        ],
        out_specs=[
            pl.BlockSpec((tm, c_conv), row),
            pl.BlockSpec((tm, attn_w), row),
            pl.BlockSpec((tm, attn_w), row),
            pl.BlockSpec((tm, attn_w), row),
            pl.BlockSpec((tm, 2 * d), row),
        ],
        out_shape=[
            jax.ShapeDtypeStruct((t, c_conv), F32),
            jax.ShapeDtypeStruct((t, attn_w), BF16),
            jax.ShapeDtypeStruct((t, attn_w), F32),
            jax.ShapeDtypeStruct((t, attn_w), F32),
            jax.ShapeDtypeStruct((t, 2 * d), F32),
        ],
        compiler_params=_cparams(1),
        name="in_proj",
    )(x, g_mix, w_in_bf, b_in, *tables)


def _conv_prompt_body(a_ref, halo_ref, w_ref, b_ref, o_ref, ext_ref, *, rows):
    i = pl.program_id(2)
    keep = jnp.where(i > 0, 1.0, 0.0).astype(F32)
    ext_ref[0:CONV_HALO, :] = halo_ref[0] * keep
    ext_ref[CONV_HALO:, :] = a_ref[0]
    w = w_ref[...]
    bias = jnp.broadcast_to(b_ref[...], (CONV_CHUNK, LANES))
    off = CONV_HALO - (CONV_WIDTH - 1)
    for r0 in range(0, rows, CONV_CHUNK):
        acc = bias
        for j in range(CONV_WIDTH):
            acc = acc + w[j:j + 1, :] * ext_ref[r0 + off + j:r0 + off + j + CONV_CHUNK, :]
        o_ref[0, r0:r0 + CONV_CHUNK, :] = acc


def _conv_prompt(a3, w_dw, b_dw):
    bsz, seq, c = a3.shape
    rows = CONV_ROWS
    hb = rows // CONV_HALO
    body = functools.partial(_conv_prompt_body, rows=rows)
    return pl.pallas_call(
        body,
        grid=(bsz, c // LANES, seq // rows),
        in_specs=[
            pl.BlockSpec((1, rows, LANES), lambda b, g, i: (b, i, g)),
            pl.BlockSpec((1, CONV_HALO, LANES),
                         lambda b, g, i: (b, jnp.maximum(i * hb - 1, 0), g)),
            pl.BlockSpec((CONV_WIDTH, LANES), lambda b, g, i: (0, g)),
            pl.BlockSpec((1, LANES), lambda b, g, i: (0, g)),
        ],
        out_specs=pl.BlockSpec((1, rows, LANES), lambda b, g, i: (b, i, g)),
        out_shape=jax.ShapeDtypeStruct((bsz, seq, c), F32),
        scratch_shapes=[pltpu.VMEM((rows + CONV_HALO, LANES), F32)],
        compiler_params=_cparams(3),
        name="conv_prompt",
    )(a3, a3, w_dw, b_dw)


def _conv_decode_body(state_ref, a_ref, w_ref, b_ref, o_ref, ext_ref, *, n_state, n_new):
    ext_ref[:, 0:n_state, :] = state_ref[...]
    ext_ref[:, n_state:n_state + n_new, :] = a_ref[...]
    w = w_ref[...]
    for t in range(n_new):
        win = ext_ref[:, t:t + CONV_WIDTH, :]
        o_ref[:, t:t + 1, :] = (jnp.sum(win * w[None], axis=1, keepdims=True)
                                + b_ref[...][None])


def _conv_decode(state, a3, w_dw, b_dw):
    bsz, n_state, c = state.shape
    n_new = a3.shape[1]
    bb = 8
    body = functools.partial(_conv_decode_body, n_state=n_state, n_new=n_new)
    return pl.pallas_call(
        body,
        grid=(bsz // bb,),
        in_specs=[
            pl.BlockSpec((bb, n_state, c), lambda i: (i, 0, 0)),
            pl.BlockSpec((bb, n_new, c), lambda i: (i, 0, 0)),
            pl.BlockSpec((CONV_WIDTH, c), lambda i: (0, 0)),
            pl.BlockSpec((1, c), lambda i: (0, 0)),
        ],
        out_specs=pl.BlockSpec((bb, n_new, c), lambda i: (i, 0, 0)),
        out_shape=jax.ShapeDtypeStruct((bsz, n_new, c), F32),
        scratch_shapes=[pltpu.VMEM((bb, n_state + n_new + 6, c), F32)],
        compiler_params=_cparams(1),
        name="conv_decode",
    )(state, a3, w_dw, b_dw)


def _attn_prompt_body(lq1, lk1, lq2, lk2, subg_ref, q_ref, k_ref, v_ref, o_ref,
                      kt_ref, vb_ref, acc1, acc2, m1, l1, m2, l2, *, blk, sub, n_blk, lam_init):
    i = pl.program_id(2)

    @pl.when(i == 0)
    def _():
        for c in range(n_blk):
            kt_ref[c] = k_ref[0, c * blk:(c + 1) * blk, :].T.astype(BF16)
            vb_ref[c] = v_ref[0, c * blk:(c + 1) * blk, :].astype(BF16)

    for m_ref, l_ref, acc_ref in ((m1, l1, acc1), (m2, l2, acc2)):
        m_ref[...] = jnp.full(m_ref.shape, NEG_BIG, F32)
        l_ref[...] = jnp.zeros(l_ref.shape, F32)
        acc_ref[...] = jnp.zeros(acc_ref.shape, F32)

    stats = ((m1, l1, acc1), (m2, l2, acc2))
    n_sub = blk // sub

    def scores(kb, sb, masked):
        r0 = sb * sub
        ncol = r0 + sub if masked else blk
        out = []
        for c in range(2):
            qq = q_ref[0, r0:r0 + sub, c * HEAD_DIM:(c + 1) * HEAD_DIM]
            kk = kt_ref[kb, c * HEAD_DIM:(c + 1) * HEAD_DIM, 0:ncol]
            out.append(jnp.dot(qq, kk, preferred_element_type=F32))
        return out

    def softmax_values(kb, sb, masked, s_pair):
        r0 = sb * sub
        rows = slice(r0, r0 + sub)
        ncol = r0 + sub if masked else blk
        ps, alphas = [], []
        for s, (m_ref, l_ref, _) in zip(s_pair, stats):
            if masked:
                row = lax.broadcasted_iota(I32, (sub, ncol), 0) + r0
                col = lax.broadcasted_iota(I32, (sub, ncol), 1)
                s = jnp.where(col <= row, s, NEG_BIG)
            m_old = m_ref[rows, :]
            m_new = jnp.maximum(m_old, jnp.max(s, axis=-1, keepdims=True))
            alpha = jnp.exp(m_old - m_new)
            p = jnp.exp(s - jnp.concatenate([m_new] * (ncol // LANES), axis=1))
            l_ref[rows, :] = alpha * l_ref[rows, :] + jnp.sum(p, axis=-1, keepdims=True)
            m_ref[rows, :] = m_new
            ps.append(p.astype(BF16))
            alphas.append(alpha)
        pv = jnp.dot(jnp.concatenate(ps, axis=0), vb_ref[kb, 0:ncol, :],
                     preferred_element_type=F32)
        for c, (alpha, (_, _, acc_ref)) in enumerate(zip(alphas, stats)):
            acc_ref[rows, :] = alpha * acc_ref[rows, :] + pv[c * sub:(c + 1) * sub]

    def step(kb, masked):
        ahead = 2
        pending = {sb: scores(kb, sb, masked) for sb in range(min(ahead, n_sub))}
        for sb in range(n_sub):
            if sb + ahead < n_sub:
                pending[sb + ahead] = scores(kb, sb + ahead, masked)
            softmax_values(kb, sb, masked, pending.pop(sb))

    def loop_body(kb, carry):
        step(kb, False)
        return carry

    lax.fori_loop(0, i, loop_body, 0)
    step(i, True)

    lam = _lam(lq1, lk1, lq2, lk2, lam_init)
    o = acc1[...] / l1[...] - lam * (acc2[...] / l2[...])
    o_ref[0] = (_rms(o, subg_ref[...]) * (1.0 - lam_init)).astype(BF16)


def _attn_prompt(q3, k3, v3, lams, subln_g, lam_init):
    bsz, seq, _ = q3.shape
    blk = ATTN_BLOCK
    n_blk = seq // blk
    vec = pl.BlockSpec((1, HEAD_DIM), lambda b, h, i: (0, 0))
    body = functools.partial(_attn_prompt_body, blk=blk, sub=ATTN_SUB, n_blk=n_blk,
                             lam_init=lam_init)
    return pl.pallas_call(
        body,
        grid=(bsz, N_HEADS, n_blk),
        in_specs=[
            vec, vec, vec, vec,
            pl.BlockSpec((1, HEAD_W), lambda b, h, i: (0, 0)),
            pl.BlockSpec((1, blk, HEAD_W), lambda b, h, i: (b, i, h)),
            pl.BlockSpec((1, seq, HEAD_W), lambda b, h, i: (b, 0, h)),
            pl.BlockSpec((1, seq, HEAD_W), lambda b, h, i: (b, 0, h)),
        ],
        out_specs=pl.BlockSpec((1, blk, HEAD_W), lambda b, h, i: (b, i, h)),
        out_shape=jax.ShapeDtypeStruct(q3.shape, BF16),
        scratch_shapes=[
            pltpu.VMEM((n_blk, HEAD_W, blk), BF16),
            pltpu.VMEM((n_blk, blk, HEAD_W), BF16),
            pltpu.VMEM((blk, HEAD_W), F32),
            pltpu.VMEM((blk, HEAD_W), F32),
            pltpu.VMEM((blk, LANES), F32),
            pltpu.VMEM((blk, LANES), F32),
            pltpu.VMEM((blk, LANES), F32),
            pltpu.VMEM((blk, LANES), F32),
        ],
        compiler_params=_cparams(3),
        name="attn_prompt",
    )(*lams, subln_g, q3, k3, v3)


NEW_PAD = 16
Q_PAD = 8


def _attn_decode_body(pt_ref, lq1, lk1, lq2, lk2, subg_ref, q_ref, kn_ref, vn_ref, *rest,
                      n_pages, n_new, lam_init):
    del pt_ref
    kpages = rest[:n_pages]
    vpages = rest[n_pages:2 * n_pages]
    o_ref, kbuf, vbuf, stage = rest[2 * n_pages:]
    n_past = n_pages * PAGE_SIZE
    width = kbuf.shape[1]

    for j in range(n_pages):
        for h in range(N_HEADS):
            rows = slice(j * PAGE_SIZE, (j + 1) * PAGE_SIZE)
            cols = slice(h * HEAD_W, (h + 1) * HEAD_W)
            head_keys = pl.ds(h, PAGE_SIZE, stride=N_HEADS)
            kbuf[rows, cols] = kpages[j][head_keys, :].astype(BF16)
            vbuf[rows, cols] = vpages[j][head_keys, :].astype(BF16)

    def padded_rows(rows_f32):
        stage[...] = jnp.zeros(stage.shape, F32)
        stage[0:n_new, :] = rows_f32
        return stage[...].astype(BF16)

    kbuf[n_past:, :] = padded_rows(kn_ref[0])
    vbuf[n_past:, :] = padded_rows(vn_ref[0])

    q16 = padded_rows(q_ref[0])
    sel_r = lax.broadcasted_iota(I32, (NEW_PAD, LANES), 0)
    sel_c = lax.broadcasted_iota(I32, (NEW_PAD, LANES), 1)
    sel = (sel_c % Q_PAD == sel_r).astype(BF16)
    qrep = lax.dot_general(q16, sel, (((0,), (0,)), ((), ())),
                           preferred_element_type=F32)
    rr = lax.broadcasted_iota(I32, (width, LANES), 0)
    cc = lax.broadcasted_iota(I32, (width, LANES), 1)
    qblk = jnp.where(rr // HEAD_DIM == cc // Q_PAD, qrep, 0.0).astype(BF16)

    s = jnp.dot(kbuf[...], qblk, preferred_element_type=F32)
    nk = n_past + NEW_PAD
    key = lax.broadcasted_iota(I32, (nk, LANES), 0)
    slot = lax.broadcasted_iota(I32, (nk, LANES), 1) % Q_PAD
    new_idx = key - n_past
    valid = (key < n_past) | ((new_idx <= slot) & (new_idx < n_new))
    s = jnp.where(valid, s, NEG_BIG)
    m = jnp.max(s, axis=0, keepdims=True)
    e = jnp.exp(s - m)
    p = e / jnp.sum(e, axis=0, keepdims=True)
    lam = _lam(lq1, lk1, lq2, lk2, lam_init)
    a = p - lam * pltpu.roll(p, LANES - Q_PAD, 1)
    lane = lax.broadcasted_iota(I32, (nk, LANES), 1)
    a = jnp.where((lane // Q_PAD) % 2 == 0, a, 0.0).astype(BF16)
    o_full = lax.dot_general(a, vbuf[...], (((0,), (0,)), ((), ())),
                             preferred_element_type=F32)
    outs = []
    for h in range(N_HEADS):
        oh = o_full[h * 2 * Q_PAD:h * 2 * Q_PAD + Q_PAD, h * HEAD_W:(h + 1) * HEAD_W]
        outs.append(_rms(oh, subg_ref[...]) * (1.0 - lam_init))
    o_ref[0] = jnp.concatenate(outs, axis=-1)[:n_new]


def _attn_decode(page_table, q3, kn3, vn3, cache_k, cache_v, lams, subln_g, lam_init):
    bsz, n_new, width = q3.shape
    n_pages = page_table.shape[1]
    vec = pl.BlockSpec((1, HEAD_DIM), lambda b, pt: (0, 0))
    per_b = pl.BlockSpec((1, n_new, width), lambda b, pt: (b, 0, 0))

    def page_spec(j):
        return pl.BlockSpec((PAGE_SIZE * N_HEADS, HEAD_W), lambda b, pt: (pt[b, j], 0))

    body = functools.partial(_attn_decode_body, n_pages=n_pages, n_new=n_new,
                             lam_init=lam_init)
    nk = n_pages * PAGE_SIZE + NEW_PAD
    grid_spec = pltpu.PrefetchScalarGridSpec(
        num_scalar_prefetch=1,
        grid=(bsz,),
        in_specs=([vec, vec, vec, vec, pl.BlockSpec((1, HEAD_W), lambda b, pt: (0, 0)),
                   per_b, per_b, per_b]
                  + [page_spec(j) for j in range(n_pages)]
                  + [page_spec(j) for j in range(n_pages)]),
        out_specs=per_b,
        scratch_shapes=[pltpu.VMEM((nk, width), BF16), pltpu.VMEM((nk, width), BF16),
                        pltpu.VMEM((NEW_PAD, width), F32)],
    )
    return pl.pallas_call(
        body,
        grid_spec=grid_spec,
        out_shape=jax.ShapeDtypeStruct(q3.shape, F32),
        compiler_params=_cparams(1),
        name="attn_decode",
    )(page_table, *lams, subln_g, q3, kn3, vn3,
      *([cache_k] * n_pages), *([cache_v] * n_pages))


def _post_body(x_ref, conv_ref, o_ref, gate_ref, lng, lnb, wc, wa, wo, gffn, wr, br,
               x1_ref, hp_ref, ids_ref, tw_ref, *, d_model):
    c = conv_ref[...]
    mu = jnp.mean(c, axis=-1, keepdims=True)
    xc = c - mu
    cn = xc * lax.rsqrt(jnp.mean(xc * xc, axis=-1, keepdims=True) + LN_EPS) * lng[...] + lnb[...]
    cact = (cn * jax.nn.sigmoid(cn)).astype(BF16)
    conv_out = jnp.dot(cact, wc[...], preferred_element_type=F32)
    attn_out = jnp.dot(o_ref[...], wa[...], preferred_element_type=F32)
    g = gate_ref[...]
    merged = (g[:, :d_model] * conv_out + g[:, d_model:] * attn_out).astype(BF16)
    x1 = x_ref[...] + jnp.dot(merged, wo[...], preferred_element_type=F32)
    x1_ref[...] = x1

    hb = _rms(x1, gffn[...]).astype(BF16)
    logits = jnp.dot(hb, wr[...], preferred_element_type=F32) + br[...]
    lane = lax.broadcasted_iota(I32, logits.shape, 1)
    vals, ids = [], []
    cur = logits
    for _ in range(TOP_K):
        mx = jnp.max(cur, axis=-1, keepdims=True)
        idx = jnp.min(jnp.where(cur == mx, lane, N_EXPERTS), axis=-1, keepdims=True)
        vals.append(mx)
        ids.append(idx)
        cur = jnp.where(lane == idx, -jnp.inf, cur)
    es = [jnp.exp(v - vals[0]) for v in vals]
    den = es[0] + es[1] + es[2] + es[3]
    tw_ref[...] = _columns([e / den for e in es])
    ids_ref[...] = _columns(ids)

    bits = pltpu.bitcast(hb.astype(F32), U32)
    half = d_model // 2
    hp_ref[...] = (bits[:, :half] >> 16) | (bits[:, half:] & jnp.uint32(0xFFFF0000))


def _post(x, conv, o, gate, ln_g, ln_b, wc, wa, wo, g_ffn, wr, br):
    t, d = x.shape
    tm = ROW_TILE
    row = lambda i: (i, 0)
    const = lambda i: (0, 0)
    mat = pl.BlockSpec((d, d), const)
    vec = pl.BlockSpec((1, d), const)
    body = functools.partial(_post_body, d_model=d)
    return pl.pallas_call(
        body,
        grid=(t // tm,),
        in_specs=[
            pl.BlockSpec((tm, d), row), pl.BlockSpec((tm, d), row), pl.BlockSpec((tm, d), row),
            pl.BlockSpec((tm, 2 * d), row),
            vec, vec, mat, mat, mat, vec,
            pl.BlockSpec((d, N_EXPERTS), const), pl.BlockSpec((1, N_EXPERTS), const),
        ],
        out_specs=[
            pl.BlockSpec((tm, d), row), pl.BlockSpec((tm, d // 2), row),
            pl.BlockSpec((tm, TOP_K), row), pl.BlockSpec((tm, TOP_K), row),
        ],
        out_shape=[
            jax.ShapeDtypeStruct((t, d), F32), jax.ShapeDtypeStruct((t, d // 2), U32),
            jax.ShapeDtypeStruct((t, TOP_K), I32), jax.ShapeDtypeStruct((t, TOP_K), F32),
        ],
        compiler_params=_cparams(1),
        name="post",
    )(x, conv, o, gate, ln_g, ln_b, wc, wa, wo, g_ffn, wr, br)


def _lane_cumsum(x):
    lane = lax.broadcasted_iota(I32, x.shape, 1)
    s = 1
    while s < LANES:
        x = x + jnp.where(lane >= s, pltpu.roll(x, s, 1), 0.0)
        s *= 2
    return x


def _positions_body(ids_ref, pos_ref, te_ref, count_ref, start_ref, *, n_tiles_pad):
    p = pl.program_id(0)
    i = pl.program_id(1)
    ids = ids_ref[...]
    tt = ids.shape[0]
    lane = lax.broadcasted_iota(I32, (tt, LANES), 1)
    onehots = [ids[:, k:k + 1] == lane for k in range(TOP_K)]
    tile_counts = [jnp.sum(oh.astype(F32), axis=0, keepdims=True) for oh in onehots]
    tile_total = tile_counts[0] + tile_counts[1] + tile_counts[2] + tile_counts[3]

    @pl.when((p == 0) & (i == 0))
    def _():
        count_ref[...] = jnp.zeros(count_ref.shape, F32)

    @pl.when((p == 1) & (i == 0))
    def _():
        counts = count_ref[...]
        padded = jnp.ceil(counts * (1.0 / MOE_TILE)) * MOE_TILE
        ends = _lane_cumsum(padded)
        start_ref[...] = ends - padded
        count_ref[...] = jnp.zeros(count_ref.shape, F32)
        tile_start = (lax.broadcasted_iota(I32, (n_tiles_pad, LANES), 0) * MOE_TILE).astype(F32)
        elane = lax.broadcasted_iota(I32, (n_tiles_pad, LANES), 1)
        done = (ends[0:1, :] <= tile_start) & (elane < N_EXPERTS)
        n_done = jnp.sum(done.astype(F32), axis=-1, keepdims=True)
        te_ref[...] = jnp.broadcast_to(n_done, (n_tiles_pad, LANES)).astype(I32)

    @pl.when(p == 1)
    def _():
        r = lax.broadcasted_iota(I32, (tt, tt), 0)
        c = lax.broadcasted_iota(I32, (tt, tt), 1)
        earlier = (c < r).astype(BF16)
        run = start_ref[0:1, :] + count_ref[0:1, :]
        cols = []
        for k in range(TOP_K):
            within = jnp.dot(earlier, onehots[k].astype(BF16), preferred_element_type=F32)
            cols.append(jnp.sum(jnp.where(onehots[k], within + run, 0.0),
                                axis=-1, keepdims=True))
            run = run + tile_counts[k]
        pos_ref[...] = _columns(cols).astype(I32)

    count_ref[...] = count_ref[...] + tile_total


def _positions(ids, n_tiles_pad):
    t = ids.shape[0]
    tt = POS_TILE
    body = functools.partial(_positions_body, n_tiles_pad=n_tiles_pad)
    return pl.pallas_call(
        body,
        grid=(2, t // tt),
        in_specs=[pl.BlockSpec((tt, TOP_K), lambda p, i: (i, 0))],
        out_specs=[
            pl.BlockSpec((tt, TOP_K), lambda p, i: (i * p, 0)),
            pl.BlockSpec((n_tiles_pad, LANES), lambda p, i: (0, 0)),
        ],
        out_shape=[
            jax.ShapeDtypeStruct((t, TOP_K), I32),
            jax.ShapeDtypeStruct((n_tiles_pad, LANES), I32),
        ],
        scratch_shapes=[pltpu.VMEM((8, LANES), F32), pltpu.VMEM((8, LANES), F32)],
        compiler_params=_cparams(2),
        name="moe_positions",
    )(ids)


def _dispatch_body(pos_ref, hp_ref, xs_in, xs_out, sem):
    del xs_in
    tt = hp_ref.shape[0]

    def row_copy(r, k):
        return pltpu.make_async_copy(hp_ref.at[pl.ds(r, 1)],
                                     xs_out.at[pl.ds(pos_ref[r * TOP_K + k], 1)], sem)

    def issue(r, carry):
        for k in range(TOP_K):
            row_copy(r, k).start(priority=k % 2)
        return carry

    lax.fori_loop(0, tt, issue, 0)

    def drain(r, carry):
        for k in range(TOP_K):
            row_copy(r, k).wait()
        return carry

    lax.fori_loop(0, tt, drain, 0)


def _dispatch(pos_flat, hp, xs):
    t, half = hp.shape
    tt = ROW_TILE
    return pl.pallas_call(
        _dispatch_body,
        grid=(t // tt,),
        in_specs=[
            pl.BlockSpec((tt * TOP_K,), lambda i: (i,), memory_space=pltpu.SMEM),
            pl.BlockSpec((tt, half), lambda i: (i, 0)),
            pl.BlockSpec(memory_space=pl.ANY),
        ],
        out_specs=pl.BlockSpec(memory_space=pl.ANY),
        out_shape=jax.ShapeDtypeStruct(xs.shape, xs.dtype),
        scratch_shapes=[pltpu.SemaphoreType.DMA],
        input_output_aliases={2: 0},
        compiler_params=_cparams(1),
        name="moe_dispatch",
    )(pos_flat, hp, xs)


def _experts_body(te_ref, nv_ref, xs_ref, w1_ref, b1_ref, w2_ref, b2_ref, ys_ref, *, d_ff):
    del te_ref
    j = pl.program_id(0)

    @pl.when(j < nv_ref[0])
    def _():
        bits = xs_ref[...]
        half = bits.shape[1]
        x_lo = pltpu.bitcast(bits << 16, F32).astype(BF16)
        x_hi = pltpu.bitcast(bits & jnp.uint32(0xFFFF0000), F32).astype(BF16)
        z = (jnp.dot(x_lo, w1_ref[0, :half, :], preferred_element_type=F32)
             + jnp.dot(x_hi, w1_ref[0, half:, :], preferred_element_type=F32)
             + b1_ref[0])
        gate = jnp.minimum(z[:, :d_ff], SWIGLU_LIMIT)
        lin = jnp.clip(z[:, d_ff:], -SWIGLU_LIMIT, SWIGLU_LIMIT)
        act = (gate * jax.nn.sigmoid(SWIGLU_ALPHA * gate) * (lin + 1.0)).astype(BF16)
        ys_ref[...] = jnp.dot(act, w2_ref[0], preferred_element_type=F32) + b2_ref[0]

    @pl.when(j >= nv_ref[0])
    def _():
        ys_ref[...] = jnp.zeros(ys_ref.shape, F32)


def _experts(tile_expert, n_valid, xs, w1, b1, w2, b2):
    n_rows, half = xs.shape
    d = 2 * half
    d_ff = w2.shape[1]
    tm = MOE_TILE
    body = functools.partial(_experts_body, d_ff=d_ff)
    grid_spec = pltpu.PrefetchScalarGridSpec(
        num_scalar_prefetch=2,
        grid=(n_rows // tm,),
        in_specs=[
            pl.BlockSpec((tm, half), lambda j, te, nv: (jnp.minimum(j, nv[0] - 1), 0)),
            pl.BlockSpec((1, d, 2 * d_ff), lambda j, te, nv: (te[j], 0, 0)),
            pl.BlockSpec((1, 1, 2 * d_ff), lambda j, te, nv: (te[j], 0, 0)),
            pl.BlockSpec((1, d_ff, d), lambda j, te, nv: (te[j], 0, 0)),
            pl.BlockSpec((1, 1, d), lambda j, te, nv: (te[j], 0, 0)),
        ],
        out_specs=pl.BlockSpec((tm, d), lambda j, te, nv: (j, 0)),
    )
    return pl.pallas_call(
        body,
        grid_spec=grid_spec,
        out_shape=jax.ShapeDtypeStruct((n_rows, d), F32),
        compiler_params=_cparams(1),
        name="moe_experts",
    )(tile_expert, n_valid, xs, w1, b1, w2, b2)


def _combine_body(pos_ref, tw_ref, x1_ref, gf_ref, ys_hbm, y_ref, rows_ref, sem):
    tt = x1_ref.shape[0]

    def row_copy(r, k):
        return pltpu.make_async_copy(ys_hbm.at[pl.ds(pos_ref[r * TOP_K + k], 1)],
                                     rows_ref.at[k, pl.ds(r, 1)], sem)

    def issue(r, carry):
        for k in range(TOP_K):
            row_copy(r, k).start(priority=k % 2)
        return carry

    lax.fori_loop(0, tt, issue, 0)

    def drain(r, carry):
        for k in range(TOP_K):
            row_copy(r, k).wait()
        return carry

    lax.fori_loop(0, tt, drain, 0)

    tw = tw_ref[...]
    moe = tw[:, 0:1] * rows_ref[0]
    for k in range(1, TOP_K):
        moe = moe + tw[:, k:k + 1] * rows_ref[k]
    y_ref[...] = _rms(x1_ref[...] + moe, gf_ref[...])


def _combine(pos_flat, tw, x1, g_final, ys):
    t, d = x1.shape
    tt = COMBINE_TILE
    return pl.pallas_call(
        _combine_body,
        grid=(t // tt,),
        in_specs=[
            pl.BlockSpec((tt * TOP_K,), lambda i: (i,), memory_space=pltpu.SMEM),
            pl.BlockSpec((tt, TOP_K), lambda i: (i, 0)),
            pl.BlockSpec((tt, d), lambda i: (i, 0)),
            pl.BlockSpec((1, d), lambda i: (0, 0)),
            pl.BlockSpec(memory_space=pl.ANY),
        ],
        out_specs=pl.BlockSpec((tt, d), lambda i: (i, 0)),
        out_shape=jax.ShapeDtypeStruct((t, d), F32),
        scratch_shapes=[pltpu.VMEM((TOP_K, tt, d), F32), pltpu.SemaphoreType.DMA],
        compiler_params=_cparams(1),
        name="moe_combine",
    )(pos_flat, tw, x1, g_final, ys)


def kernel(x_prompt, x_sample, cache_k, cache_v, state_conv, page_table, g_mix, w_in, b_in,
           w_dw, b_dw, ln_g, ln_b, w_conv_out, lam_q1, lam_k1, lam_q2, lam_k2, subln_g,
           w_attn_out, w_o, g_ffn, w_router, b_router, w_moe1, b_moe1, w_moe2, b_moe2, g_final):
    bsz, seq, d = x_prompt.shape
    dec_b, dec_s, _ = x_sample.shape
    depth = g_mix.shape[0]
    c_conv = w_dw.shape[2]
    attn_w = N_HEADS * HEAD_W
    n_pages = page_table.shape[1]
    past_len = n_pages * PAGE_SIZE
    t_p, t_s = bsz * seq, dec_b * dec_s
    t_all = t_p + t_s
    d_ff = w_moe2.shape[2]
    assert depth == 1, "the combine kernel fuses the final norm, so only one layer is supported"
    assert seq % ROW_TILE == 0 and t_s % ROW_TILE == 0 and seq % CONV_ROWS == 0
    assert seq % ATTN_BLOCK == 0 and t_all % POS_TILE == 0 and dec_b % 8 == 0
    assert t_p % COMBINE_TILE == 0 and t_s % COMBINE_TILE == 0

    tab_p = _rope_tables(jnp.arange(seq))
    tab_s = _rope_tables(jnp.tile(past_len + jnp.arange(dec_s), dec_b))
    n_rows = t_all * TOP_K + N_EXPERTS * MOE_TILE
    n_tiles = n_rows // MOE_TILE
    row2 = lambda v: v.reshape(1, -1)

    hp, hs = x_prompt.reshape(t_p, d), x_sample.reshape(t_s, d)
    outs = [[] for _ in range(6)]
    for l in range(depth):
        lam_init = 0.8 - 0.6 * math.exp(-0.3 * l)
        lams = (row2(lam_q1[l]), row2(lam_k1[l]), row2(lam_q2[l]), row2(lam_k2[l]))
        subg = row2(subln_g[l])
        w_in_bf = w_in[l].astype(BF16)
        wc, wa, wo = (w_conv_out[l].astype(BF16), w_attn_out[l].astype(BF16),
                      w_o[l].astype(BF16))
        wr = w_router[l].astype(BF16)
        w1, w2 = w_moe1[l].astype(BF16), w_moe2[l].astype(BF16)
        b1 = b_moe1[l].reshape(N_EXPERTS, 1, 2 * d_ff)
        b2 = b_moe2[l].reshape(N_EXPERTS, 1, d)
        proj_args = (row2(g_mix[l]), w_in_bf, row2(b_in[l]))
        post_args = (row2(ln_g[l]), row2(ln_b[l]), wc, wa, wo, row2(g_ffn[l]), wr,
                     row2(b_router[l]))

        a_p, q_p, k_p, v_p, gate_p = _in_proj(hp, *proj_args, tab_p, seq // ROW_TILE,
                                              c_conv, attn_w)
        a_p3 = a_p.reshape(bsz, seq, c_conv)
        conv_p = _conv_prompt(a_p3, w_dw[l], row2(b_dw[l]))
        o_p = _attn_prompt(q_p.reshape(bsz, seq, attn_w), k_p.reshape(bsz, seq, attn_w),
                           v_p.reshape(bsz, seq, attn_w), lams, subg, lam_init)
        x1_p, hpk_p, ids_p, tw_p = _post(hp, conv_p.reshape(t_p, c_conv),
                                         o_p.reshape(t_p, attn_w), gate_p, *post_args)

        a_s, q_s, k_s, v_s, gate_s = _in_proj(hs, *proj_args, tab_s, 1, c_conv, attn_w)
        a_s3 = a_s.reshape(dec_b, dec_s, c_conv)
        conv_s = _conv_decode(state_conv[l], a_s3, w_dw[l], row2(b_dw[l]))
        pool = cache_k.shape[1]
        o_s = _attn_decode(page_table, q_s.astype(F32).reshape(dec_b, dec_s, attn_w),
                           k_s.reshape(dec_b, dec_s, attn_w), v_s.reshape(dec_b, dec_s, attn_w),
                           cache_k[l].reshape(pool * PAGE_SIZE * N_HEADS, HEAD_W),
                           cache_v[l].reshape(pool * PAGE_SIZE * N_HEADS, HEAD_W),
                           lams, subg, lam_init)
        x1_s, hpk_s, ids_s, tw_s = _post(hs, conv_s.reshape(t_s, c_conv),
                                         o_s.reshape(t_s, attn_w).astype(BF16), gate_s,
                                         *post_args)

        ids = jnp.concatenate([ids_p, ids_s], axis=0)
        pos, te = _positions(ids, n_tiles)
        pos_flat = pos.reshape(-1)
        tile_expert = jnp.minimum(te[:, 0], N_EXPERTS - 1)
        n_valid = jnp.sum((te[:, 0] < N_EXPERTS).astype(I32)).reshape(1)
        xs = jnp.zeros((n_rows, d // 2), U32)
        xs = _dispatch(pos_flat[:t_p * TOP_K], hpk_p, xs)
        xs = _dispatch(pos_flat[t_p * TOP_K:], hpk_s, xs)
        ys = _experts(tile_expert, n_valid, xs, w1, b1, w2, b2)
        gf = row2(g_final)
        hp = _combine(pos_flat[:t_p * TOP_K], tw_p, x1_p, gf, ys)
        hs = _combine(pos_flat[t_p * TOP_K:], tw_s, x1_s, gf, ys)

        outs[0].append(k_p.reshape(bsz, seq, N_HEADS, HEAD_W))
        outs[1].append(v_p.reshape(bsz, seq, N_HEADS, HEAD_W))
        outs[2].append(a_p3[:, seq - (CONV_WIDTH - 1):])
        outs[3].append(k_s.reshape(dec_b, dec_s, N_HEADS, HEAD_W))
        outs[4].append(v_s.reshape(dec_b, dec_s, N_HEADS, HEAD_W))
        outs[5].append(jnp.concatenate([state_conv[l], a_s3], axis=1)[:, -(CONV_WIDTH - 1):])

    y_prompt = hp.reshape(bsz, seq, d)
    y_sample = hs.reshape(dec_b, dec_s, d)
    return (y_prompt, y_sample) + tuple(jnp.stack(o) for o in outs)
```

```python
import functools
import math

import jax
import jax.numpy as jnp
from jax import lax
from jax.experimental import pallas as pl
from jax.experimental.pallas import tpu as pltpu

F32 = jnp.float32
BF16 = jnp.bfloat16
I32 = jnp.int32
U32 = jnp.uint32

N_HEADS = 8
HEAD_DIM = 64
HEAD_W = 2 * HEAD_DIM
ROT_DIM = HEAD_DIM // 4
ROT_HALF = ROT_DIM // 2
ROPE_THETA = 500000.0
CONV_WIDTH = 31
CONV_HALO = 32
N_EXPERTS = 32
TOP_K = 4
SWIGLU_ALPHA = 1.702
SWIGLU_LIMIT = 7.0
RMS_EPS = 1e-5
LN_EPS = 1e-5
PAGE_SIZE = 128
LANES = 128
NEG_BIG = -1e30

ROW_TILE = 512
ATTN_BLOCK = 512
ATTN_SUB = 128
CONV_ROWS = 1024
CONV_CHUNK = 64
MOE_TILE = 256
POS_TILE = 512
COMBINE_TILE = 256
VMEM_LIMIT = 56 * 1024 * 1024


def _cparams(n_axes, vmem=VMEM_LIMIT):
    return pltpu.CompilerParams(dimension_semantics=("arbitrary",) * n_axes,
                                vmem_limit_bytes=vmem)


def _rms(x, g):
    return x * lax.rsqrt(jnp.mean(x * x, axis=-1, keepdims=True) + RMS_EPS) * g


def _columns(cols):
    rows = cols[0].shape[0]
    lane = lax.broadcasted_iota(I32, (rows, len(cols)), 1)
    out = jnp.zeros((rows, len(cols)), cols[0].dtype)
    for k, col in enumerate(cols):
        out = jnp.where(lane == k, col, out)
    return out


def _lam(lq1, lk1, lq2, lk2, lam_init):
    s1 = jnp.sum(lq1[...] * lk1[...], axis=-1, keepdims=True)
    s2 = jnp.sum(lq2[...] * lk2[...], axis=-1, keepdims=True)
    return jnp.exp(s1) - jnp.exp(s2) + lam_init


def _rope_tables(pos):
    inv = jnp.power(jnp.float32(ROPE_THETA),
                    -jnp.arange(ROT_HALF, dtype=F32) * (2.0 / ROT_DIM))
    ang = pos.astype(F32)[:, None] * inv[None, :]
    cos, sin = jnp.cos(ang), jnp.sin(ang)
    n = pos.shape[0]
    rest = HEAD_DIM - ROT_DIM
    zh = jnp.zeros((n, ROT_HALF), F32)
    c64 = jnp.concatenate([cos, cos, jnp.ones((n, rest), F32)], axis=-1)
    sa64 = jnp.concatenate([-sin, zh, jnp.zeros((n, rest), F32)], axis=-1)
    sb64 = jnp.concatenate([zh, sin, jnp.zeros((n, rest), F32)], axis=-1)
    tile = lambda t: jnp.concatenate([t, t], axis=-1)
    return tile(c64), tile(sa64), tile(sb64)


def _in_proj_body(x_ref, g_ref, w_ref, b_ref, cos_ref, sa_ref, sb_ref,
                  a_ref, q_ref, k_ref, v_ref, gate_ref, *, c_conv, attn_w, d_model):
    h = _rms(x_ref[...], g_ref[...]).astype(BF16)
    cos, sa, sb = cos_ref[...], sa_ref[...], sb_ref[...]
    ch = 512

    def proj(c0):
        return (jnp.dot(h, w_ref[:, c0:c0 + ch], preferred_element_type=F32)
                + b_ref[:, c0:c0 + ch])

    def rope(z):
        outs = []
        for j in range(ch // LANES):
            zj = z[:, j * LANES:(j + 1) * LANES]
            outs.append(zj * cos + pltpu.roll(zj, LANES - ROT_HALF, 1) * sa
                        + pltpu.roll(zj, ROT_HALF, 1) * sb)
        return jnp.concatenate(outs, axis=-1)

    for c in range(0, c_conv, ch):
        a_ref[:, c:c + ch] = proj(c) * jax.nn.sigmoid(proj(c_conv + c))
    base = 2 * c_conv
    for c in range(0, attn_w, ch):
        q_ref[:, c:c + ch] = (rope(proj(base + c)) * (HEAD_DIM ** -0.5)).astype(BF16)
    base += attn_w
    for c in range(0, attn_w, ch):
        k_ref[:, c:c + ch] = rope(proj(base + c))
    base += attn_w
    for c in range(0, attn_w, ch):
        v_ref[:, c:c + ch] = proj(base + c)
    base += attn_w
    for c in range(0, 2 * d_model, ch):
        gate_ref[:, c:c + ch] = jax.nn.sigmoid(proj(base + c))


def _in_proj(x, g_mix, w_in_bf, b_in, tables, n_tab_blocks, c_conv, attn_w):
    t, d = x.shape
    n_in = w_in_bf.shape[1]
    tm = ROW_TILE
    row = lambda i: (i, 0)
    const = lambda i: (0, 0)
    tab = lambda i: (i % n_tab_blocks, 0)
    body = functools.partial(_in_proj_body, c_conv=c_conv, attn_w=attn_w, d_model=d)
    return pl.pallas_call(
        body,
        grid=(t // tm,),
        in_specs=[
            pl.BlockSpec((tm, d), row),
            pl.BlockSpec((1, d), const),
            pl.BlockSpec((d, n_in), const, pipeline_mode=pl.Buffered(1)),
            pl.BlockSpec((1, n_in), const),
            pl.BlockSpec((tm, LANES), tab),
            pl.BlockSpec((tm, LANES), tab),
            pl.BlockSpec((tm, LANES), tab),
        ],
        out_specs=[
            pl.BlockSpec((tm, c_conv), row),
            pl.BlockSpec((tm, attn_w), row),
            pl.BlockSpec((tm, attn_w), row),
            pl.BlockSpec((tm, attn_w), row),
            pl.BlockSpec((tm, 2 * d), row),
        ],
        out_shape=[
            jax.ShapeDtypeStruct((t, c_conv), F32),
            jax.ShapeDtypeStruct((t, attn_w), BF16),
            jax.ShapeDtypeStruct((t, attn_w), F32),
            jax.ShapeDtypeStruct((t, attn_w), F32),
            jax.ShapeDtypeStruct((t, 2 * d), F32),
        ],
        compiler_params=_cparams(1),
        name="in_proj",
    )(x, g_mix, w_in_bf, b_in, *tables)


def _conv_prompt_body(a_ref, halo_ref, w_ref, b_ref, o_ref, ext_ref, *, rows):
    i = pl.program_id(2)
    keep = jnp.where(i > 0, 1.0, 0.0).astype(F32)
    ext_ref[0:CONV_HALO, :] = halo_ref[0] * keep
    ext_ref[CONV_HALO:, :] = a_ref[0]
    w = w_ref[...]
    bias = jnp.broadcast_to(b_ref[...], (CONV_CHUNK, LANES))
    off = CONV_HALO - (CONV_WIDTH - 1)
    for r0 in range(0, rows, CONV_CHUNK):
        acc = bias
        for j in range(CONV_WIDTH):
            acc = acc + w[j:j + 1, :] * ext_ref[r0 + off + j:r0 + off + j + CONV_CHUNK, :]
        o_ref[0, r0:r0 + CONV_CHUNK, :] = acc


def _conv_prompt(a3, w_dw, b_dw):
    bsz, seq, c = a3.shape
    rows = CONV_ROWS
    hb = rows // CONV_HALO
    body = functools.partial(_conv_prompt_body, rows=rows)
    return pl.pallas_call(
        body,
        grid=(bsz, c // LANES, seq // rows),
        in_specs=[
            pl.BlockSpec((1, rows, LANES), lambda b, g, i: (b, i, g)),
            pl.BlockSpec((1, CONV_HALO, LANES),
                         lambda b, g, i: (b, jnp.maximum(i * hb - 1, 0), g)),
            pl.BlockSpec((CONV_WIDTH, LANES), lambda b, g, i: (0, g)),
            pl.BlockSpec((1, LANES), lambda b, g, i: (0, g)),
        ],
        out_specs=pl.BlockSpec((1, rows, LANES), lambda b, g, i: (b, i, g)),
        out_shape=jax.ShapeDtypeStruct((bsz, seq, c), F32),
        scratch_shapes=[pltpu.VMEM((rows + CONV_HALO, LANES), F32)],
        compiler_params=_cparams(3),
        name="conv_prompt",
    )(a3, a3, w_dw, b_dw)


def _conv_decode_body(state_ref, a_ref, w_ref, b_ref, o_ref, ext_ref, *, n_state, n_new):
    ext_ref[:, 0:n_state, :] = state_ref[...]
    ext_ref[:, n_state:n_state + n_new, :] = a_ref[...]
    w = w_ref[...]
    for t in range(n_new):
        win = ext_ref[:, t:t + CONV_WIDTH, :]
        o_ref[:, t:t + 1, :] = (jnp.sum(win * w[None], axis=1, keepdims=True)
                                + b_ref[...][None])


def _conv_decode(state, a3, w_dw, b_dw):
    bsz, n_state, c = state.shape
    n_new = a3.shape[1]
    bb = 8
    body = functools.partial(_conv_decode_body, n_state=n_state, n_new=n_new)
    return pl.pallas_call(
        body,
        grid=(bsz // bb,),
        in_specs=[
            pl.BlockSpec((bb, n_state, c), lambda i: (i, 0, 0)),
            pl.BlockSpec((bb, n_new, c), lambda i: (i, 0, 0)),
            pl.BlockSpec((CONV_WIDTH, c), lambda i: (0, 0)),
            pl.BlockSpec((1, c), lambda i: (0, 0)),
        ],
        out_specs=pl.BlockSpec((bb, n_new, c), lambda i: (i, 0, 0)),
        out_shape=jax.ShapeDtypeStruct((bsz, n_new, c), F32),
        scratch_shapes=[pltpu.VMEM((bb, n_state + n_new + 6, c), F32)],
        compiler_params=_cparams(1),
        name="conv_decode",
    )(state, a3, w_dw, b_dw)


def _attn_prompt_body(lq1, lk1, lq2, lk2, subg_ref, q_ref, k_ref, v_ref, o_ref,
                      kt_ref, vb_ref, acc1, acc2, m1, l1, m2, l2, *, blk, sub, n_blk, lam_init):
    i = pl.program_id(2)

    @pl.when(i == 0)
    def _():
        for c in range(n_blk):
            kt_ref[c] = k_ref[0, c * blk:(c + 1) * blk, :].T.astype(BF16)
            vb_ref[c] = v_ref[0, c * blk:(c + 1) * blk, :].astype(BF16)

    for m_ref, l_ref, acc_ref in ((m1, l1, acc1), (m2, l2, acc2)):
        m_ref[...] = jnp.full(m_ref.shape, NEG_BIG, F32)
        l_ref[...] = jnp.zeros(l_ref.shape, F32)
        acc_ref[...] = jnp.zeros(acc_ref.shape, F32)

    stats = ((m1, l1, acc1), (m2, l2, acc2))
    n_sub = blk // sub

    def scores(kb, sb, masked):
        r0 = sb * sub
        ncol = r0 + sub if masked else blk
        out = []
        for c in range(2):
            qq = q_ref[0, r0:r0 + sub, c * HEAD_DIM:(c + 1) * HEAD_DIM]
            kk = kt_ref[kb, c * HEAD_DIM:(c + 1) * HEAD_DIM, 0:ncol]
            out.append(jnp.dot(qq, kk, preferred_element_type=F32))
        return out

    def softmax_values(kb, sb, masked, s_pair):
        r0 = sb * sub
        rows = slice(r0, r0 + sub)
        ncol = r0 + sub if masked else blk
        ps, alphas = [], []
        for s, (m_ref, l_ref, _) in zip(s_pair, stats):
            if masked:
                row = lax.broadcasted_iota(I32, (sub, ncol), 0) + r0
                col = lax.broadcasted_iota(I32, (sub, ncol), 1)
                s = jnp.where(col <= row, s, NEG_BIG)
            m_old = m_ref[rows, :]
            m_new = jnp.maximum(m_old, jnp.max(s, axis=-1, keepdims=True))
            alpha = jnp.exp(m_old - m_new)
            p = jnp.exp(s - jnp.concatenate([m_new] * (ncol // LANES), axis=1))
            l_ref[rows, :] = alpha * l_ref[rows, :] + jnp.sum(p, axis=-1, keepdims=True)
            m_ref[rows, :] = m_new
            ps.append(p.astype(BF16))
            alphas.append(alpha)
        pv = jnp.dot(jnp.concatenate(ps, axis=0), vb_ref[kb, 0:ncol, :],
                     preferred_element_type=F32)
        for c, (alpha, (_, _, acc_ref)) in enumerate(zip(alphas, stats)):
            acc_ref[rows, :] = alpha * acc_ref[rows, :] + pv[c * sub:(c + 1) * sub]

    def run(chains):
        ahead = 2
        pending = {n: scores(*chains[n]) for n in range(min(ahead, len(chains)))}
        for n, chain in enumerate(chains):
            if n + ahead < len(chains):
                pending[n + ahead] = scores(*chains[n + ahead])
            softmax_values(*chain, pending.pop(n))

    def block(kb, masked):
        return [(kb, sb, masked) for sb in range(n_sub)]

    def pair_body(t, carry):
        run(block(2 * t, False) + block(2 * t + 1, False))
        return carry

    lax.fori_loop(0, lax.shift_right_logical(i, 1), pair_body, 0)
    odd = (i & 1) == 1

    @pl.when(odd)
    def _():
        run(block(i - 1, False) + block(i, True))

    @pl.when(jnp.logical_not(odd))
    def _():
        run(block(i, True))

    lam = _lam(lq1, lk1, lq2, lk2, lam_init)
    o = acc1[...] / l1[...] - lam * (acc2[...] / l2[...])
    o_ref[0] = (_rms(o, subg_ref[...]) * (1.0 - lam_init)).astype(BF16)


def _attn_prompt(q3, k3, v3, lams, subln_g, lam_init):
    bsz, seq, _ = q3.shape
    blk = ATTN_BLOCK
    n_blk = seq // blk
    vec = pl.BlockSpec((1, HEAD_DIM), lambda b, h, i: (0, 0))
    body = functools.partial(_attn_prompt_body, blk=blk, sub=ATTN_SUB, n_blk=n_blk,
                             lam_init=lam_init)
    return pl.pallas_call(
        body,
        grid=(bsz, N_HEADS, n_blk),
        in_specs=[
            vec, vec, vec, vec,
            pl.BlockSpec((1, HEAD_W), lambda b, h, i: (0, 0)),
            pl.BlockSpec((1, blk, HEAD_W), lambda b, h, i: (b, i, h)),
            pl.BlockSpec((1, seq, HEAD_W), lambda b, h, i: (b, 0, h)),
            pl.BlockSpec((1, seq, HEAD_W), lambda b, h, i: (b, 0, h)),
        ],
        out_specs=pl.BlockSpec((1, blk, HEAD_W), lambda b, h, i: (b, i, h)),
        out_shape=jax.ShapeDtypeStruct(q3.shape, BF16),
        scratch_shapes=[
            pltpu.VMEM((n_blk, HEAD_W, blk), BF16),
            pltpu.VMEM((n_blk, blk, HEAD_W), BF16),
            pltpu.VMEM((blk, HEAD_W), F32),
            pltpu.VMEM((blk, HEAD_W), F32),
            pltpu.VMEM((blk, LANES), F32),
            pltpu.VMEM((blk, LANES), F32),
            pltpu.VMEM((blk, LANES), F32),
            pltpu.VMEM((blk, LANES), F32),
        ],
        compiler_params=_cparams(3),
        name="attn_prompt",
    )(*lams, subln_g, q3, k3, v3)


NEW_PAD = 16
Q_PAD = 8
DECODE_PAGE_GROUP = 4


def _attn_decode_body(pt_ref, lq1, lk1, lq2, lk2, subg_ref, q_ref, kn_ref, vn_ref, *rest,
                      n_pages, n_new, lam_init):
    del pt_ref
    kpages = rest[:n_pages]
    vpages = rest[n_pages:2 * n_pages]
    o_ref, s_ref, stage = rest[2 * n_pages:]
    n_past = n_pages * PAGE_SIZE
    width = stage.shape[1]

    def padded_rows(rows_f32):
        stage[...] = jnp.zeros(stage.shape, F32)
        stage[0:n_new, :] = rows_f32
        return stage[...].astype(BF16)

    q16 = padded_rows(q_ref[0])
    sel_r = lax.broadcasted_iota(I32, (NEW_PAD, LANES), 0)
    sel_c = lax.broadcasted_iota(I32, (NEW_PAD, LANES), 1)
    sel = (sel_c % Q_PAD == sel_r).astype(BF16)
    qrep = lax.dot_general(q16, sel, (((0,), (0,)), ((), ())),
                           preferred_element_type=F32)
    rr = lax.broadcasted_iota(I32, (width, LANES), 0)
    cc = lax.broadcasted_iota(I32, (width, LANES), 1)
    qblk = jnp.where(rr // HEAD_DIM == cc // Q_PAD, qrep, 0.0).astype(BF16)

    def head_major(page_ref):
        return jnp.concatenate(
            [page_ref[pl.ds(h, PAGE_SIZE, stride=N_HEADS), :].astype(BF16)
             for h in range(N_HEADS)], axis=-1)

    groups = [range(j0, min(j0 + DECODE_PAGE_GROUP, n_pages))
              for j0 in range(0, n_pages, DECODE_PAGE_GROUP)]
    row_slices = [slice(g[0] * PAGE_SIZE, (g[-1] + 1) * PAGE_SIZE) for g in groups]

    for g, rows in zip(groups, row_slices):
        keys = jnp.concatenate([head_major(kpages[j]) for j in g], axis=0)
        s_ref[rows, :] = jnp.dot(keys, qblk, preferred_element_type=F32)
    s_new = jnp.dot(padded_rows(kn_ref[0]), qblk, preferred_element_type=F32)
    new_idx = lax.broadcasted_iota(I32, (NEW_PAD, LANES), 0)
    slot = lax.broadcasted_iota(I32, (NEW_PAD, LANES), 1) % Q_PAD
    s_new = jnp.where((new_idx <= slot) & (new_idx < n_new), s_new, NEG_BIG)
    m = jnp.maximum(jnp.max(s_ref[0:n_past, :], axis=0, keepdims=True),
                    jnp.max(s_new, axis=0, keepdims=True))

    contract0 = (((0,), (0,)), ((), ()))
    e_new = jnp.exp(s_new - m)
    denom = jnp.sum(e_new, axis=0, keepdims=True)
    acc = lax.dot_general(e_new.astype(BF16), padded_rows(vn_ref[0]), contract0,
                          preferred_element_type=F32)
    for g, rows in zip(groups, row_slices):
        e = jnp.exp(s_ref[rows, :] - m)
        denom = denom + jnp.sum(e, axis=0, keepdims=True)
        vals = jnp.concatenate([head_major(vpages[j]) for j in g], axis=0)
        acc = acc + lax.dot_general(e.astype(BF16), vals, contract0,
                                    preferred_element_type=F32)

    r_i = lax.broadcasted_iota(I32, (LANES, LANES), 0)
    c_i = lax.broadcasted_iota(I32, (LANES, LANES), 1)
    denom_rows = jnp.sum(jnp.where(r_i == c_i, jnp.broadcast_to(denom, (LANES, LANES)), 0.0),
                         axis=1, keepdims=True)
    o_norm = acc / denom_rows
    lam = _lam(lq1, lk1, lq2, lk2, lam_init)
    outs = []
    for h in range(N_HEADS):
        r0 = h * 2 * Q_PAD
        cols = slice(h * HEAD_W, (h + 1) * HEAD_W)
        oh = o_norm[r0:r0 + Q_PAD, cols] - lam * o_norm[r0 + Q_PAD:r0 + 2 * Q_PAD, cols]
        outs.append(_rms(oh, subg_ref[...]) * (1.0 - lam_init))
    o_ref[0] = jnp.concatenate(outs, axis=-1)[:n_new]


def _attn_decode(page_table, q3, kn3, vn3, cache_k, cache_v, lams, subln_g, lam_init):
    bsz, n_new, width = q3.shape
    n_pages = page_table.shape[1]
    vec = pl.BlockSpec((1, HEAD_DIM), lambda b, pt: (0, 0))
    per_b = pl.BlockSpec((1, n_new, width), lambda b, pt: (b, 0, 0))

    def page_spec(j):
        return pl.BlockSpec((PAGE_SIZE * N_HEADS, HEAD_W), lambda b, pt: (pt[b, j], 0))

    body = functools.partial(_attn_decode_body, n_pages=n_pages, n_new=n_new,
                             lam_init=lam_init)
    grid_spec = pltpu.PrefetchScalarGridSpec(
        num_scalar_prefetch=1,
        grid=(bsz,),
        in_specs=([vec, vec, vec, vec, pl.BlockSpec((1, HEAD_W), lambda b, pt: (0, 0)),
                   per_b, per_b, per_b]
                  + [page_spec(j) for j in range(n_pages)]
                  + [page_spec(j) for j in range(n_pages)]),
        out_specs=per_b,
        scratch_shapes=[pltpu.VMEM((n_pages * PAGE_SIZE, LANES), F32),
                        pltpu.VMEM((NEW_PAD, width), F32)],
    )
    return pl.pallas_call(
        body,
        grid_spec=grid_spec,
        out_shape=jax.ShapeDtypeStruct(q3.shape, F32),
        compiler_params=_cparams(1),
        name="attn_decode",
    )(page_table, *lams, subln_g, q3, kn3, vn3,
      *([cache_k] * n_pages), *([cache_v] * n_pages))


def _post_body(x_ref, conv_ref, o_ref, gate_ref, lng, lnb, wc, wa, wo, gffn, wr, br,
               x1_ref, hp_ref, ids_ref, tw_ref, *, d_model):
    n_groups = 2
    rows_per = x_ref.shape[0] // n_groups
    groups = [slice(g * rows_per, (g + 1) * rows_per) for g in range(n_groups)]
    attn_outs = [jnp.dot(o_ref[r, :], wa[...], preferred_element_type=F32) for r in groups]
    conv_outs = []
    for r in groups:
        c = conv_ref[r, :]
        mu = jnp.mean(c, axis=-1, keepdims=True)
        xc = c - mu
        cn = (xc * lax.rsqrt(jnp.mean(xc * xc, axis=-1, keepdims=True) + LN_EPS) * lng[...]
              + lnb[...])
        cact = (cn * jax.nn.sigmoid(cn)).astype(BF16)
        conv_outs.append(jnp.dot(cact, wc[...], preferred_element_type=F32))
    hbs = []
    for r, conv_out, attn_out in zip(groups, conv_outs, attn_outs):
        merged = (gate_ref[r, :d_model] * conv_out
                  + gate_ref[r, d_model:] * attn_out).astype(BF16)
        x1 = x_ref[r, :] + jnp.dot(merged, wo[...], preferred_element_type=F32)
        x1_ref[r, :] = x1
        hbs.append(_rms(x1, gffn[...]).astype(BF16))
    for r, hb in zip(groups, hbs):
        logits = jnp.dot(hb, wr[...], preferred_element_type=F32) + br[...]
        lane = lax.broadcasted_iota(I32, logits.shape, 1)
        vals, ids = [], []
        cur = logits
        for _ in range(TOP_K):
            mx = jnp.max(cur, axis=-1, keepdims=True)
            idx = jnp.min(jnp.where(cur == mx, lane, N_EXPERTS), axis=-1, keepdims=True)
            vals.append(mx)
            ids.append(idx)
            cur = jnp.where(lane == idx, -jnp.inf, cur)
        es = [jnp.exp(v - vals[0]) for v in vals]
        den = es[0] + es[1] + es[2] + es[3]
        tw_ref[r, :] = _columns([e / den for e in es])
        ids_ref[r, :] = _columns(ids)

        bits = pltpu.bitcast(hb.astype(F32), U32)
        half = d_model // 2
        hp_ref[r, :] = (bits[:, :half] >> 16) | (bits[:, half:] & jnp.uint32(0xFFFF0000))


def _post(x, conv, o, gate, ln_g, ln_b, wc, wa, wo, g_ffn, wr, br):
    t, d = x.shape
    tm = ROW_TILE
    row = lambda i: (i, 0)
    const = lambda i: (0, 0)
    mat = pl.BlockSpec((d, d), const)
    vec = pl.BlockSpec((1, d), const)
    body = functools.partial(_post_body, d_model=d)
    return pl.pallas_call(
        body,
        grid=(t // tm,),
        in_specs=[
            pl.BlockSpec((tm, d), row), pl.BlockSpec((tm, d), row), pl.BlockSpec((tm, d), row),
            pl.BlockSpec((tm, 2 * d), row),
            vec, vec, mat, mat, mat, vec,
            pl.BlockSpec((d, N_EXPERTS), const), pl.BlockSpec((1, N_EXPERTS), const),
        ],
        out_specs=[
            pl.BlockSpec((tm, d), row), pl.BlockSpec((tm, d // 2), row),
            pl.BlockSpec((tm, TOP_K), row), pl.BlockSpec((tm, TOP_K), row),
        ],
        out_shape=[
            jax.ShapeDtypeStruct((t, d), F32), jax.ShapeDtypeStruct((t, d // 2), U32),
            jax.ShapeDtypeStruct((t, TOP_K), I32), jax.ShapeDtypeStruct((t, TOP_K), F32),
        ],
        compiler_params=_cparams(1),
        name="post",
    )(x, conv, o, gate, ln_g, ln_b, wc, wa, wo, g_ffn, wr, br)


def _lane_cumsum(x):
    lane = lax.broadcasted_iota(I32, x.shape, 1)
    s = 1
    while s < LANES:
        x = x + jnp.where(lane >= s, pltpu.roll(x, s, 1), 0.0)
        s *= 2
    return x


def _positions_body(ids_ref, pos_ref, te_ref, count_ref, start_ref, *, n_tiles_pad):
    p = pl.program_id(0)
    i = pl.program_id(1)
    ids = ids_ref[...]
    tt = ids.shape[0]
    lane = lax.broadcasted_iota(I32, (tt, LANES), 1)
    onehots = [ids[:, k:k + 1] == lane for k in range(TOP_K)]
    tile_counts = [jnp.sum(oh.astype(F32), axis=0, keepdims=True) for oh in onehots]
    tile_total = tile_counts[0] + tile_counts[1] + tile_counts[2] + tile_counts[3]

    @pl.when((p == 0) & (i == 0))
    def _():
        count_ref[...] = jnp.zeros(count_ref.shape, F32)

    @pl.when((p == 1) & (i == 0))
    def _():
        counts = count_ref[...]
        padded = jnp.ceil(counts * (1.0 / MOE_TILE)) * MOE_TILE
        ends = _lane_cumsum(padded)
        start_ref[...] = ends - padded
        count_ref[...] = jnp.zeros(count_ref.shape, F32)
        tile_start = (lax.broadcasted_iota(I32, (n_tiles_pad, LANES), 0) * MOE_TILE).astype(F32)
        elane = lax.broadcasted_iota(I32, (n_tiles_pad, LANES), 1)
        done = (ends[0:1, :] <= tile_start) & (elane < N_EXPERTS)
        n_done = jnp.sum(done.astype(F32), axis=-1, keepdims=True)
        te_ref[...] = jnp.broadcast_to(n_done, (n_tiles_pad, LANES)).astype(I32)

    @pl.when(p == 1)
    def _():
        r = lax.broadcasted_iota(I32, (tt, tt), 0)
        c = lax.broadcasted_iota(I32, (tt, tt), 1)
        earlier = (c < r).astype(BF16)
        run = start_ref[0:1, :] + count_ref[0:1, :]
        cols = []
        for k in range(TOP_K):
            within = jnp.dot(earlier, onehots[k].astype(BF16), preferred_element_type=F32)
            cols.append(jnp.sum(jnp.where(onehots[k], within + run, 0.0),
                                axis=-1, keepdims=True))
            run = run + tile_counts[k]
        pos_ref[...] = _columns(cols).astype(I32)

    count_ref[...] = count_ref[...] + tile_total


def _positions(ids, n_tiles_pad):
    t = ids.shape[0]
    tt = POS_TILE
    body = functools.partial(_positions_body, n_tiles_pad=n_tiles_pad)
    return pl.pallas_call(
        body,
        grid=(2, t // tt),
        in_specs=[pl.BlockSpec((tt, TOP_K), lambda p, i: (i, 0))],
        out_specs=[
            pl.BlockSpec((tt, TOP_K), lambda p, i: (i * p, 0)),
            pl.BlockSpec((n_tiles_pad, LANES), lambda p, i: (0, 0)),
        ],
        out_shape=[
            jax.ShapeDtypeStruct((t, TOP_K), I32),
            jax.ShapeDtypeStruct((n_tiles_pad, LANES), I32),
        ],
        scratch_shapes=[pltpu.VMEM((8, LANES), F32), pltpu.VMEM((8, LANES), F32)],
        compiler_params=_cparams(2),
        name="moe_positions",
    )(ids)


def _dispatch_body(pos_ref, hp_ref, xs_in, xs_out, sem):
    del xs_in
    tt = hp_ref.shape[0]

    def row_copy(r, k):
        return pltpu.make_async_copy(hp_ref.at[pl.ds(r, 1)],
                                     xs_out.at[pl.ds(pos_ref[r * TOP_K + k], 1)], sem)

    def issue(r, carry):
        for k in range(TOP_K):
            row_copy(r, k).start(priority=k % 2)
        return carry

    lax.fori_loop(0, tt, issue, 0)

    def drain(r, carry):
        for k in range(TOP_K):
            row_copy(r, k).wait()
        return carry

    lax.fori_loop(0, tt, drain, 0)


def _dispatch(pos_flat, hp, xs):
    t, half = hp.shape
    tt = ROW_TILE
    return pl.pallas_call(
        _dispatch_body,
        grid=(t // tt,),
        in_specs=[
            pl.BlockSpec((tt * TOP_K,), lambda i: (i,), memory_space=pltpu.SMEM),
            pl.BlockSpec((tt, half), lambda i: (i, 0)),
            pl.BlockSpec(memory_space=pl.ANY),
        ],
        out_specs=pl.BlockSpec(memory_space=pl.ANY),
        out_shape=jax.ShapeDtypeStruct(xs.shape, xs.dtype),
        scratch_shapes=[pltpu.SemaphoreType.DMA],
        input_output_aliases={2: 0},
        compiler_params=_cparams(1),
        name="moe_dispatch",
    )(pos_flat, hp, xs)


def _experts_body(te_ref, nv_ref, xs_ref, w1_ref, b1_ref, w2_ref, b2_ref, ys_ref, *, d_ff):
    del te_ref
    j = pl.program_id(0)

    @pl.when(j < nv_ref[0])
    def _():
        bits = xs_ref[...]
        half = bits.shape[1]
        x_lo = pltpu.bitcast(bits << 16, F32).astype(BF16)
        x_hi = pltpu.bitcast(bits & jnp.uint32(0xFFFF0000), F32).astype(BF16)
        z = (jnp.dot(x_lo, w1_ref[0, :half, :], preferred_element_type=F32)
             + jnp.dot(x_hi, w1_ref[0, half:, :], preferred_element_type=F32)
             + b1_ref[0])
        gate = jnp.minimum(z[:, :d_ff], SWIGLU_LIMIT)
        lin = jnp.clip(z[:, d_ff:], -SWIGLU_LIMIT, SWIGLU_LIMIT)
        act = (gate * jax.nn.sigmoid(SWIGLU_ALPHA * gate) * (lin + 1.0)).astype(BF16)
        ys_ref[...] = jnp.dot(act, w2_ref[0], preferred_element_type=F32) + b2_ref[0]

    @pl.when(j >= nv_ref[0])
    def _():
        ys_ref[...] = jnp.zeros(ys_ref.shape, F32)


def _experts(tile_expert, n_valid, xs, w1, b1, w2, b2):
    n_rows, half = xs.shape
    d = 2 * half
    d_ff = w2.shape[1]
    tm = MOE_TILE
    body = functools.partial(_experts_body, d_ff=d_ff)
    grid_spec = pltpu.PrefetchScalarGridSpec(
        num_scalar_prefetch=2,
        grid=(n_rows // tm,),
        in_specs=[
            pl.BlockSpec((tm, half), lambda j, te, nv: (jnp.minimum(j, nv[0] - 1), 0)),
            pl.BlockSpec((1, d, 2 * d_ff), lambda j, te, nv: (te[j], 0, 0)),
            pl.BlockSpec((1, 1, 2 * d_ff), lambda j, te, nv: (te[j], 0, 0)),
            pl.BlockSpec((1, d_ff, d), lambda j, te, nv: (te[j], 0, 0)),
            pl.BlockSpec((1, 1, d), lambda j, te, nv: (te[j], 0, 0)),
        ],
        out_specs=pl.BlockSpec((tm, d), lambda j, te, nv: (j, 0)),
    )
    return pl.pallas_call(
        body,
        grid_spec=grid_spec,
        out_shape=jax.ShapeDtypeStruct((n_rows, d), F32),
        compiler_params=_cparams(1),
        name="moe_experts",
    )(tile_expert, n_valid, xs, w1, b1, w2, b2)


def _combine_body(pos_ref, tw_ref, x1_ref, gf_ref, ys_hbm, y_ref, rows_ref, sem):
    tt = x1_ref.shape[0]

    def row_copy(r, k):
        return pltpu.make_async_copy(ys_hbm.at[pl.ds(pos_ref[r * TOP_K + k], 1)],
                                     rows_ref.at[k, pl.ds(r, 1)], sem)

    def issue(r, carry):
        for k in range(TOP_K):
            row_copy(r, k).start(priority=k % 2)
        return carry

    lax.fori_loop(0, tt, issue, 0)

    def drain(r, carry):
        for k in range(TOP_K):
            row_copy(r, k).wait()
        return carry

    lax.fori_loop(0, tt, drain, 0)

    tw = tw_ref[...]
    moe = tw[:, 0:1] * rows_ref[0]
    for k in range(1, TOP_K):
        moe = moe + tw[:, k:k + 1] * rows_ref[k]
    y_ref[...] = _rms(x1_ref[...] + moe, gf_ref[...])


def _combine(pos_flat, tw, x1, g_final, ys):
    t, d = x1.shape
    tt = COMBINE_TILE
    return pl.pallas_call(
        _combine_body,
        grid=(t // tt,),
        in_specs=[
            pl.BlockSpec((tt * TOP_K,), lambda i: (i,), memory_space=pltpu.SMEM),
            pl.BlockSpec((tt, TOP_K), lambda i: (i, 0)),
            pl.BlockSpec((tt, d), lambda i: (i, 0)),
            pl.BlockSpec((1, d), lambda i: (0, 0)),
            pl.BlockSpec(memory_space=pl.ANY),
        ],
        out_specs=pl.BlockSpec((tt, d), lambda i: (i, 0)),
        out_shape=jax.ShapeDtypeStruct((t, d), F32),
        scratch_shapes=[pltpu.VMEM((TOP_K, tt, d), F32), pltpu.SemaphoreType.DMA],
        compiler_params=_cparams(1),
        name="moe_combine",
    )(pos_flat, tw, x1, g_final, ys)


def kernel(x_prompt, x_sample, cache_k, cache_v, state_conv, page_table, g_mix, w_in, b_in,
           w_dw, b_dw, ln_g, ln_b, w_conv_out, lam_q1, lam_k1, lam_q2, lam_k2, subln_g,
           w_attn_out, w_o, g_ffn, w_router, b_router, w_moe1, b_moe1, w_moe2, b_moe2, g_final):
    bsz, seq, d = x_prompt.shape
    dec_b, dec_s, _ = x_sample.shape
    depth = g_mix.shape[0]
    c_conv = w_dw.shape[2]
    attn_w = N_HEADS * HEAD_W
    n_pages = page_table.shape[1]
    past_len = n_pages * PAGE_SIZE
    t_p, t_s = bsz * seq, dec_b * dec_s
    t_all = t_p + t_s
    d_ff = w_moe2.shape[2]
    assert depth == 1, "the combine kernel fuses the final norm, so only one layer is supported"
    assert seq % ROW_TILE == 0 and t_s % ROW_TILE == 0 and seq % CONV_ROWS == 0
    assert seq % ATTN_BLOCK == 0 and t_all % POS_TILE == 0 and dec_b % 8 == 0
    assert t_p % COMBINE_TILE == 0 and t_s % COMBINE_TILE == 0

    tab_p = _rope_tables(jnp.arange(seq))
    tab_s = _rope_tables(jnp.tile(past_len + jnp.arange(dec_s), dec_b))
    n_rows = t_all * TOP_K + N_EXPERTS * MOE_TILE
    n_tiles = n_rows // MOE_TILE
    row2 = lambda v: v.reshape(1, -1)

    hp, hs = x_prompt.reshape(t_p, d), x_sample.reshape(t_s, d)
    outs = [[] for _ in range(6)]
    for l in range(depth):
        lam_init = 0.8 - 0.6 * math.exp(-0.3 * l)
        lams = (row2(lam_q1[l]), row2(lam_k1[l]), row2(lam_q2[l]), row2(lam_k2[l]))
        subg = row2(subln_g[l])
        w_in_bf = w_in[l].astype(BF16)
        wc, wa, wo = (w_conv_out[l].astype(BF16), w_attn_out[l].astype(BF16),
                      w_o[l].astype(BF16))
        wr = w_router[l].astype(BF16)
        w1, w2 = w_moe1[l].astype(BF16), w_moe2[l].astype(BF16)
        b1 = b_moe1[l].reshape(N_EXPERTS, 1, 2 * d_ff)
        b2 = b_moe2[l].reshape(N_EXPERTS, 1, d)
        proj_args = (row2(g_mix[l]), w_in_bf, row2(b_in[l]))
        post_args = (row2(ln_g[l]), row2(ln_b[l]), wc, wa, wo, row2(g_ffn[l]), wr,
                     row2(b_router[l]))

        a_p, q_p, k_p, v_p, gate_p = _in_proj(hp, *proj_args, tab_p, seq // ROW_TILE,
                                              c_conv, attn_w)
        a_p3 = a_p.reshape(bsz, seq, c_conv)
        conv_p = _conv_prompt(a_p3, w_dw[l], row2(b_dw[l]))
        o_p = _attn_prompt(q_p.reshape(bsz, seq, attn_w), k_p.reshape(bsz, seq, attn_w),
                           v_p.reshape(bsz, seq, attn_w), lams, subg, lam_init)
        x1_p, hpk_p, ids_p, tw_p = _post(hp, conv_p.reshape(t_p, c_conv),
                                         o_p.reshape(t_p, attn_w), gate_p, *post_args)

        a_s, q_s, k_s, v_s, gate_s = _in_proj(hs, *proj_args, tab_s, 1, c_conv, attn_w)
        a_s3 = a_s.reshape(dec_b, dec_s, c_conv)
        conv_s = _conv_decode(state_conv[l], a_s3, w_dw[l], row2(b_dw[l]))
        pool = cache_k.shape[1]
        o_s = _attn_decode(page_table, q_s.astype(F32).reshape(dec_b, dec_s, attn_w),
                           k_s.reshape(dec_b, dec_s, attn_w), v_s.reshape(dec_b, dec_s, attn_w),
                           cache_k[l].reshape(pool * PAGE_SIZE * N_HEADS, HEAD_W),
                           cache_v[l].reshape(pool * PAGE_SIZE * N_HEADS, HEAD_W),
                           lams, subg, lam_init)
        x1_s, hpk_s, ids_s, tw_s = _post(hs, conv_s.reshape(t_s, c_conv),
                                         o_s.reshape(t_s, attn_w).astype(BF16), gate_s,
                                         *post_args)

        ids = jnp.concatenate([ids_p, ids_s], axis=0)
        pos, te = _positions(ids, n_tiles)
        pos_flat = pos.reshape(-1)
        tile_expert = jnp.minimum(te[:, 0], N_EXPERTS - 1)
        n_valid = jnp.sum((te[:, 0] < N_EXPERTS).astype(I32)).reshape(1)
        xs = jnp.zeros((n_rows, d // 2), U32)
        xs = _dispatch(pos_flat[:t_p * TOP_K], hpk_p, xs)
        xs = _dispatch(pos_flat[t_p * TOP_K:], hpk_s, xs)
        ys = _experts(tile_expert, n_valid, xs, w1, b1, w2, b2)
        gf = row2(g_final)
        hp = _combine(pos_flat[:t_p * TOP_K], tw_p, x1_p, gf, ys)
        hs = _combine(pos_flat[t_p * TOP_K:], tw_s, x1_s, gf, ys)

        outs[0].append(k_p.reshape(bsz, seq, N_HEADS, HEAD_W))
        outs[1].append(v_p.reshape(bsz, seq, N_HEADS, HEAD_W))
        outs[2].append(a_p3[:, seq - (CONV_WIDTH - 1):])
        outs[3].append(k_s.reshape(dec_b, dec_s, N_HEADS, HEAD_W))
        outs[4].append(v_s.reshape(dec_b, dec_s, N_HEADS, HEAD_W))
        outs[5].append(jnp.concatenate([state_conv[l], a_s3], axis=1)[:, -(CONV_WIDTH - 1):])

    y_prompt = hp.reshape(bsz, seq, d)
    y_sample = hs.reshape(dec_b, dec_s, d)
    return (y_prompt, y_sample) + tuple(jnp.stack(o) for o in outs)
```

```python
import functools
import math

import jax
import jax.numpy as jnp
from jax import lax
from jax.experimental import pallas as pl
from jax.experimental.pallas import tpu as pltpu

F32 = jnp.float32
BF16 = jnp.bfloat16
I32 = jnp.int32
U32 = jnp.uint32

N_HEADS = 8
HEAD_DIM = 64
HEAD_W = 2 * HEAD_DIM
ROT_DIM = HEAD_DIM // 4
ROT_HALF = ROT_DIM // 2
ROPE_THETA = 500000.0
CONV_WIDTH = 31
CONV_HALO = 32
N_EXPERTS = 32
TOP_K = 4
SWIGLU_ALPHA = 1.702
SWIGLU_LIMIT = 7.0
RMS_EPS = 1e-5
LN_EPS = 1e-5
PAGE_SIZE = 128
LANES = 128
NEG_BIG = -1e30

ROW_TILE = 512
ATTN_BLOCK = 512
ATTN_SUB = 128
CONV_ROWS = 1024
CONV_CHUNK = 64
MOE_TILE = 256
EXPERT_CHUNK = 256
POS_TILE = 512
COMBINE_TILE = 256
ROUTE_TOKEN_BITS = 15
INVERSE_CHUNK = 11 * 1024
PAD_TOKENS = 64
VMEM_LIMIT = 56 * 1024 * 1024


def _cparams(n_axes, vmem=VMEM_LIMIT):
    return pltpu.CompilerParams(dimension_semantics=("arbitrary",) * n_axes,
                                vmem_limit_bytes=vmem)


def _rms(x, g):
    return x * lax.rsqrt(jnp.mean(x * x, axis=-1, keepdims=True) + RMS_EPS) * g


def _columns(cols):
    rows = cols[0].shape[0]
    lane = lax.broadcasted_iota(I32, (rows, len(cols)), 1)
    out = jnp.zeros((rows, len(cols)), cols[0].dtype)
    for k, col in enumerate(cols):
        out = jnp.where(lane == k, col, out)
    return out


def _lam(lq1, lk1, lq2, lk2, lam_init):
    s1 = jnp.sum(lq1[...] * lk1[...], axis=-1, keepdims=True)
    s2 = jnp.sum(lq2[...] * lk2[...], axis=-1, keepdims=True)
    return jnp.exp(s1) - jnp.exp(s2) + lam_init


def _rope_tables(pos):
    inv = jnp.power(jnp.float32(ROPE_THETA),
                    -jnp.arange(ROT_HALF, dtype=F32) * (2.0 / ROT_DIM))
    ang = pos.astype(F32)[:, None] * inv[None, :]
    cos, sin = jnp.cos(ang), jnp.sin(ang)
    n = pos.shape[0]
    rest = HEAD_DIM - ROT_DIM
    zh = jnp.zeros((n, ROT_HALF), F32)
    c64 = jnp.concatenate([cos, cos, jnp.ones((n, rest), F32)], axis=-1)
    sa64 = jnp.concatenate([-sin, zh, jnp.zeros((n, rest), F32)], axis=-1)
    sb64 = jnp.concatenate([zh, sin, jnp.zeros((n, rest), F32)], axis=-1)
    tile = lambda t: jnp.concatenate([t, t], axis=-1)
    return tile(c64), tile(sa64), tile(sb64)


def _in_proj_body(x_ref, g_ref, w_ref, b_ref, cos_ref, sa_ref, sb_ref,
                  a_ref, q_ref, k_ref, v_ref, gate_ref, *, c_conv, attn_w, d_model):
    h = _rms(x_ref[...], g_ref[...]).astype(BF16)
    cos, sa, sb = cos_ref[...], sa_ref[...], sb_ref[...]
    ch = 512

    def proj(c0):
        return (jnp.dot(h, w_ref[:, c0:c0 + ch], preferred_element_type=F32)
                + b_ref[:, c0:c0 + ch])

    def rope(z):
        outs = []
        for j in range(ch // LANES):
            zj = z[:, j * LANES:(j + 1) * LANES]
            outs.append(zj * cos + pltpu.roll(zj, LANES - ROT_HALF, 1) * sa
                        + pltpu.roll(zj, ROT_HALF, 1) * sb)
        return jnp.concatenate(outs, axis=-1)

    for c in range(0, c_conv, ch):
        a_ref[:, c:c + ch] = proj(c) * jax.nn.sigmoid(proj(c_conv + c))
    base = 2 * c_conv
    for c in range(0, attn_w, ch):
        q_ref[:, c:c + ch] = (rope(proj(base + c)) * (HEAD_DIM ** -0.5)).astype(BF16)
    base += attn_w
    for c in range(0, attn_w, ch):
        k_ref[:, c:c + ch] = rope(proj(base + c))
    base += attn_w
    for c in range(0, attn_w, ch):
        v_ref[:, c:c + ch] = proj(base + c)
    base += attn_w
    for c in range(0, 2 * d_model, ch):
        gate_ref[:, c:c + ch] = jax.nn.sigmoid(proj(base + c))


def _in_proj(x, g_mix, w_in_bf, b_in, tables, n_tab_blocks, c_conv, attn_w):
    t, d = x.shape
    n_in = w_in_bf.shape[1]
    tm = ROW_TILE
    row = lambda i: (i, 0)
    const = lambda i: (0, 0)
    tab = lambda i: (i % n_tab_blocks, 0)
    body = functools.partial(_in_proj_body, c_conv=c_conv, attn_w=attn_w, d_model=d)
    return pl.pallas_call(
        body,
        grid=(t // tm,),
        in_specs=[
            pl.BlockSpec((tm, d), row),
            pl.BlockSpec((1, d), const),
            pl.BlockSpec((d, n_in), const, pipeline_mode=pl.Buffered(1)),
            pl.BlockSpec((1, n_in), const),
            pl.BlockSpec((tm, LANES), tab),
            pl.BlockSpec((tm, LANES), tab),
            pl.BlockSpec((tm, LANES), tab),
        ],
        out_specs=[
            pl.BlockSpec((tm, c_conv), row),
            pl.BlockSpec((tm, attn_w), row),
            pl.BlockSpec((tm, attn_w), row),
            pl.BlockSpec((tm, attn_w), row),
            pl.BlockSpec((tm, 2 * d), row),
        ],
        out_shape=[
            jax.ShapeDtypeStruct((t, c_conv), F32),
            jax.ShapeDtypeStruct((t, attn_w), BF16),
            jax.ShapeDtypeStruct((t, attn_w), F32),
            jax.ShapeDtypeStruct((t, attn_w), F32),
            jax.ShapeDtypeStruct((t, 2 * d), F32),
        ],
        compiler_params=_cparams(1),
        name="in_proj",
    )(x, g_mix, w_in_bf, b_in, *tables)


def _conv_prompt_body(a_ref, halo_ref, w_ref, b_ref, o_ref, ext_ref, *, rows):
    i = pl.program_id(2)
    keep = jnp.where(i > 0, 1.0, 0.0).astype(F32)
    ext_ref[0:CONV_HALO, :] = halo_ref[0] * keep
    ext_ref[CONV_HALO:, :] = a_ref[0]
    w = w_ref[...]
    bias = jnp.broadcast_to(b_ref[...], (CONV_CHUNK, LANES))
    off = CONV_HALO - (CONV_WIDTH - 1)
    for r0 in range(0, rows, CONV_CHUNK):
        acc = bias
        for j in range(CONV_WIDTH):
            acc = acc + w[j:j + 1, :] * ext_ref[r0 + off + j:r0 + off + j + CONV_CHUNK, :]
        o_ref[0, r0:r0 + CONV_CHUNK, :] = acc


def _conv_prompt(a3, w_dw, b_dw):
    bsz, seq, c = a3.shape
    rows = CONV_ROWS
    hb = rows // CONV_HALO
    body = functools.partial(_conv_prompt_body, rows=rows)
    return pl.pallas_call(
        body,
        grid=(bsz, c // LANES, seq // rows),
        in_specs=[
            pl.BlockSpec((1, rows, LANES), lambda b, g, i: (b, i, g)),
            pl.BlockSpec((1, CONV_HALO, LANES),
                         lambda b, g, i: (b, jnp.maximum(i * hb - 1, 0), g)),
            pl.BlockSpec((CONV_WIDTH, LANES), lambda b, g, i: (0, g)),
            pl.BlockSpec((1, LANES), lambda b, g, i: (0, g)),
        ],
        out_specs=pl.BlockSpec((1, rows, LANES), lambda b, g, i: (b, i, g)),
        out_shape=jax.ShapeDtypeStruct((bsz, seq, c), F32),
        scratch_shapes=[pltpu.VMEM((rows + CONV_HALO, LANES), F32)],
        compiler_params=_cparams(3),
        name="conv_prompt",
    )(a3, a3, w_dw, b_dw)


def _conv_decode_body(state_ref, a_ref, w_ref, b_ref, o_ref, ext_ref, *, n_state, n_new):
    ext_ref[:, 0:n_state, :] = state_ref[...]
    ext_ref[:, n_state:n_state + n_new, :] = a_ref[...]
    w = w_ref[...]
    for t in range(n_new):
        win = ext_ref[:, t:t + CONV_WIDTH, :]
        o_ref[:, t:t + 1, :] = (jnp.sum(win * w[None], axis=1, keepdims=True)
                                + b_ref[...][None])


def _conv_decode(state, a3, w_dw, b_dw):
    bsz, n_state, c = state.shape
    n_new = a3.shape[1]
    bb = 8
    body = functools.partial(_conv_decode_body, n_state=n_state, n_new=n_new)
    return pl.pallas_call(
        body,
        grid=(bsz // bb,),
        in_specs=[
            pl.BlockSpec((bb, n_state, c), lambda i: (i, 0, 0)),
            pl.BlockSpec((bb, n_new, c), lambda i: (i, 0, 0)),
            pl.BlockSpec((CONV_WIDTH, c), lambda i: (0, 0)),
            pl.BlockSpec((1, c), lambda i: (0, 0)),
        ],
        out_specs=pl.BlockSpec((bb, n_new, c), lambda i: (i, 0, 0)),
        out_shape=jax.ShapeDtypeStruct((bsz, n_new, c), F32),
        scratch_shapes=[pltpu.VMEM((bb, n_state + n_new + 6, c), F32)],
        compiler_params=_cparams(1),
        name="conv_decode",
    )(state, a3, w_dw, b_dw)


def _attn_prompt_body(lq1, lk1, lq2, lk2, subg_ref, q_ref, k_ref, v_ref, o_ref,
                      kt_ref, vb_ref, acc1, acc2, m1, l1, m2, l2, *, blk, sub, n_blk, lam_init):
    i = pl.program_id(2)

    @pl.when(i == 0)
    def _():
        for c in range(n_blk):
            kt_ref[c] = k_ref[0, c * blk:(c + 1) * blk, :].T.astype(BF16)
            vb_ref[c] = v_ref[0, c * blk:(c + 1) * blk, :].astype(BF16)

    for m_ref, l_ref, acc_ref in ((m1, l1, acc1), (m2, l2, acc2)):
        m_ref[...] = jnp.full(m_ref.shape, NEG_BIG, F32)
        l_ref[...] = jnp.zeros(l_ref.shape, F32)
        acc_ref[...] = jnp.zeros(acc_ref.shape, F32)

    stats = ((m1, l1, acc1), (m2, l2, acc2))
    n_sub = blk // sub

    def scores(kb, sb, masked):
        r0 = sb * sub
        ncol = r0 + sub if masked else blk
        out = []
        for c in range(2):
            qq = q_ref[0, r0:r0 + sub, c * HEAD_DIM:(c + 1) * HEAD_DIM]
            kk = kt_ref[kb, c * HEAD_DIM:(c + 1) * HEAD_DIM, 0:ncol]
            out.append(jnp.dot(qq, kk, preferred_element_type=F32))
        return out

    def softmax_values(kb, sb, masked, s_pair):
        r0 = sb * sub
        rows = slice(r0, r0 + sub)
        ncol = r0 + sub if masked else blk
        ps, alphas = [], []
        for s, (m_ref, l_ref, _) in zip(s_pair, stats):
            if masked:
                row = lax.broadcasted_iota(I32, (sub, ncol), 0) + r0
                col = lax.broadcasted_iota(I32, (sub, ncol), 1)
                s = jnp.where(col <= row, s, NEG_BIG)
            m_old = m_ref[rows, :]
            m_new = jnp.maximum(m_old, jnp.max(s, axis=-1, keepdims=True))
            alpha = jnp.exp(m_old - m_new)
            p = jnp.exp(s - jnp.concatenate([m_new] * (ncol // LANES), axis=1))
            l_ref[rows, :] = alpha * l_ref[rows, :] + jnp.sum(p, axis=-1, keepdims=True)
            m_ref[rows, :] = m_new
            ps.append(p.astype(BF16))
            alphas.append(alpha)
        pv = jnp.dot(jnp.concatenate(ps, axis=0), vb_ref[kb, 0:ncol, :],
                     preferred_element_type=F32)
        for c, (alpha, (_, _, acc_ref)) in enumerate(zip(alphas, stats)):
            acc_ref[rows, :] = alpha * acc_ref[rows, :] + pv[c * sub:(c + 1) * sub]

    def run(chains):
        ahead = 2
        pending = {n: scores(*chains[n]) for n in range(min(ahead, len(chains)))}
        for n, chain in enumerate(chains):
            if n + ahead < len(chains):
                pending[n + ahead] = scores(*chains[n + ahead])
            softmax_values(*chain, pending.pop(n))

    def block(kb, masked):
        return [(kb, sb, masked) for sb in range(n_sub)]

    def pair_body(t, carry):
        run(block(2 * t, False) + block(2 * t + 1, False))
        return carry

    lax.fori_loop(0, lax.shift_right_logical(i, 1), pair_body, 0)
    odd = (i & 1) == 1

    @pl.when(odd)
    def _():
        run(block(i - 1, False) + block(i, True))

    @pl.when(jnp.logical_not(odd))
    def _():
        run(block(i, True))

    lam = _lam(lq1, lk1, lq2, lk2, lam_init)
    o = acc1[...] / l1[...] - lam * (acc2[...] / l2[...])
    o_ref[0] = (_rms(o, subg_ref[...]) * (1.0 - lam_init)).astype(BF16)


def _attn_prompt(q3, k3, v3, lams, subln_g, lam_init):
    bsz, seq, _ = q3.shape
    blk = ATTN_BLOCK
    n_blk = seq // blk
    vec = pl.BlockSpec((1, HEAD_DIM), lambda b, h, i: (0, 0))
    body = functools.partial(_attn_prompt_body, blk=blk, sub=ATTN_SUB, n_blk=n_blk,
                             lam_init=lam_init)
    return pl.pallas_call(
        body,
        grid=(bsz, N_HEADS, n_blk),
        in_specs=[
            vec, vec, vec, vec,
            pl.BlockSpec((1, HEAD_W), lambda b, h, i: (0, 0)),
            pl.BlockSpec((1, blk, HEAD_W), lambda b, h, i: (b, i, h)),
            pl.BlockSpec((1, seq, HEAD_W), lambda b, h, i: (b, 0, h)),
            pl.BlockSpec((1, seq, HEAD_W), lambda b, h, i: (b, 0, h)),
        ],
        out_specs=pl.BlockSpec((1, blk, HEAD_W), lambda b, h, i: (b, i, h)),
        out_shape=jax.ShapeDtypeStruct(q3.shape, BF16),
        scratch_shapes=[
            pltpu.VMEM((n_blk, HEAD_W, blk), BF16),
            pltpu.VMEM((n_blk, blk, HEAD_W), BF16),
            pltpu.VMEM((blk, HEAD_W), F32),
            pltpu.VMEM((blk, HEAD_W), F32),
            pltpu.VMEM((blk, LANES), F32),
            pltpu.VMEM((blk, LANES), F32),
            pltpu.VMEM((blk, LANES), F32),
            pltpu.VMEM((blk, LANES), F32),
        ],
        compiler_params=_cparams(3),
        name="attn_prompt",
    )(*lams, subln_g, q3, k3, v3)


NEW_PAD = 16
Q_PAD = 8
DECODE_PAGE_GROUP = 4


def _attn_decode_body(pt_ref, lq1, lk1, lq2, lk2, subg_ref, q_ref, kn_ref, vn_ref, *rest,
                      n_pages, n_new, lam_init):
    del pt_ref
    kpages = rest[:n_pages]
    vpages = rest[n_pages:2 * n_pages]
    o_ref, s_ref, stage = rest[2 * n_pages:]
    n_past = n_pages * PAGE_SIZE
    width = stage.shape[1]

    def padded_rows(rows_f32):
        stage[...] = jnp.zeros(stage.shape, F32)
        stage[0:n_new, :] = rows_f32
        return stage[...].astype(BF16)

    q16 = padded_rows(q_ref[0])
    sel_r = lax.broadcasted_iota(I32, (NEW_PAD, LANES), 0)
    sel_c = lax.broadcasted_iota(I32, (NEW_PAD, LANES), 1)
    sel = (sel_c % Q_PAD == sel_r).astype(BF16)
    qrep = lax.dot_general(q16, sel, (((0,), (0,)), ((), ())),
                           preferred_element_type=F32)
    rr = lax.broadcasted_iota(I32, (width, LANES), 0)
    cc = lax.broadcasted_iota(I32, (width, LANES), 1)
    qblk = jnp.where(rr // HEAD_DIM == cc // Q_PAD, qrep, 0.0).astype(BF16)

    def head_major(page_ref):
        return jnp.concatenate(
            [page_ref[pl.ds(h, PAGE_SIZE, stride=N_HEADS), :].astype(BF16)
             for h in range(N_HEADS)], axis=-1)

    groups = [range(j0, min(j0 + DECODE_PAGE_GROUP, n_pages))
              for j0 in range(0, n_pages, DECODE_PAGE_GROUP)]
    row_slices = [slice(g[0] * PAGE_SIZE, (g[-1] + 1) * PAGE_SIZE) for g in groups]

    for g, rows in zip(groups, row_slices):
        keys = jnp.concatenate([head_major(kpages[j]) for j in g], axis=0)
        s_ref[rows, :] = jnp.dot(keys, qblk, preferred_element_type=F32)
    s_new = jnp.dot(padded_rows(kn_ref[0]), qblk, preferred_element_type=F32)
    new_idx = lax.broadcasted_iota(I32, (NEW_PAD, LANES), 0)
    slot = lax.broadcasted_iota(I32, (NEW_PAD, LANES), 1) % Q_PAD
    s_new = jnp.where((new_idx <= slot) & (new_idx < n_new), s_new, NEG_BIG)
    m = jnp.maximum(jnp.max(s_ref[0:n_past, :], axis=0, keepdims=True),
                    jnp.max(s_new, axis=0, keepdims=True))

    contract0 = (((0,), (0,)), ((), ()))
    e_new = jnp.exp(s_new - m)
    denom = jnp.sum(e_new, axis=0, keepdims=True)
    acc = lax.dot_general(e_new.astype(BF16), padded_rows(vn_ref[0]), contract0,
                          preferred_element_type=F32)
    for g, rows in zip(groups, row_slices):
        e = jnp.exp(s_ref[rows, :] - m)
        denom = denom + jnp.sum(e, axis=0, keepdims=True)
        vals = jnp.concatenate([head_major(vpages[j]) for j in g], axis=0)
        acc = acc + lax.dot_general(e.astype(BF16), vals, contract0,
                                    preferred_element_type=F32)

    r_i = lax.broadcasted_iota(I32, (LANES, LANES), 0)
    c_i = lax.broadcasted_iota(I32, (LANES, LANES), 1)
    denom_rows = jnp.sum(jnp.where(r_i == c_i, jnp.broadcast_to(denom, (LANES, LANES)), 0.0),
                         axis=1, keepdims=True)
    o_norm = acc / denom_rows
    lam = _lam(lq1, lk1, lq2, lk2, lam_init)
    outs = []
    for h in range(N_HEADS):
        r0 = h * 2 * Q_PAD
        cols = slice(h * HEAD_W, (h + 1) * HEAD_W)
        oh = o_norm[r0:r0 + Q_PAD, cols] - lam * o_norm[r0 + Q_PAD:r0 + 2 * Q_PAD, cols]
        outs.append(_rms(oh, subg_ref[...]) * (1.0 - lam_init))
    o_ref[0] = jnp.concatenate(outs, axis=-1)[:n_new]


def _attn_decode(page_table, q3, kn3, vn3, cache_k, cache_v, lams, subln_g, lam_init):
    bsz, n_new, width = q3.shape
    n_pages = page_table.shape[1]
    vec = pl.BlockSpec((1, HEAD_DIM), lambda b, pt: (0, 0))
    per_b = pl.BlockSpec((1, n_new, width), lambda b, pt: (b, 0, 0))

    def page_spec(j):
        return pl.BlockSpec((PAGE_SIZE * N_HEADS, HEAD_W), lambda b, pt: (pt[b, j], 0))

    body = functools.partial(_attn_decode_body, n_pages=n_pages, n_new=n_new,
                             lam_init=lam_init)
    grid_spec = pltpu.PrefetchScalarGridSpec(
        num_scalar_prefetch=1,
        grid=(bsz,),
        in_specs=([vec, vec, vec, vec, pl.BlockSpec((1, HEAD_W), lambda b, pt: (0, 0)),
                   per_b, per_b, per_b]
                  + [page_spec(j) for j in range(n_pages)]
                  + [page_spec(j) for j in range(n_pages)]),
        out_specs=per_b,
        scratch_shapes=[pltpu.VMEM((n_pages * PAGE_SIZE, LANES), F32),
                        pltpu.VMEM((NEW_PAD, width), F32)],
    )
    return pl.pallas_call(
        body,
        grid_spec=grid_spec,
        out_shape=jax.ShapeDtypeStruct(q3.shape, F32),
        compiler_params=_cparams(1),
        name="attn_decode",
    )(page_table, *lams, subln_g, q3, kn3, vn3,
      *([cache_k] * n_pages), *([cache_v] * n_pages))


def _post_body(x_ref, conv_ref, o_ref, gate_ref, lng, lnb, wc, wa, wo, gffn, wr, br,
               x1_ref, hp_ref, ids_ref, tw_ref, *, d_model):
    n_groups = 2
    rows_per = x_ref.shape[0] // n_groups
    groups = [slice(g * rows_per, (g + 1) * rows_per) for g in range(n_groups)]
    attn_outs = [jnp.dot(o_ref[r, :], wa[...], preferred_element_type=F32) for r in groups]
    conv_outs = []
    for r in groups:
        c = conv_ref[r, :]
        mu = jnp.mean(c, axis=-1, keepdims=True)
        xc = c - mu
        cn = (xc * lax.rsqrt(jnp.mean(xc * xc, axis=-1, keepdims=True) + LN_EPS) * lng[...]
              + lnb[...])
        cact = (cn * jax.nn.sigmoid(cn)).astype(BF16)
        conv_outs.append(jnp.dot(cact, wc[...], preferred_element_type=F32))
    hbs = []
    for r, conv_out, attn_out in zip(groups, conv_outs, attn_outs):
        merged = (gate_ref[r, :d_model] * conv_out
                  + gate_ref[r, d_model:] * attn_out).astype(BF16)
        x1 = x_ref[r, :] + jnp.dot(merged, wo[...], preferred_element_type=F32)
        x1_ref[r, :] = x1
        hbs.append(_rms(x1, gffn[...]).astype(BF16))
    for r, hb in zip(groups, hbs):
        logits = jnp.dot(hb, wr[...], preferred_element_type=F32) + br[...]
        lane = lax.broadcasted_iota(I32, logits.shape, 1)
        vals, ids = [], []
        cur = logits
        for _ in range(TOP_K):
            mx = jnp.max(cur, axis=-1, keepdims=True)
            idx = jnp.min(jnp.where(cur == mx, lane, N_EXPERTS), axis=-1, keepdims=True)
            vals.append(mx)
            ids.append(idx)
            cur = jnp.where(lane == idx, -jnp.inf, cur)
        es = [jnp.exp(v - vals[0]) for v in vals]
        den = es[0] + es[1] + es[2] + es[3]
        tw_ref[r, :] = _columns([e / den for e in es])
        ids_ref[r, :] = _columns(ids)

        bits = pltpu.bitcast(hb.astype(F32), U32)
        half = d_model // 2
        hp_ref[r, :] = (bits[:, :half] >> 16) | (bits[:, half:] & jnp.uint32(0xFFFF0000))


def _post(x, conv, o, gate, ln_g, ln_b, wc, wa, wo, g_ffn, wr, br):
    t, d = x.shape
    tm = ROW_TILE
    row = lambda i: (i, 0)
    const = lambda i: (0, 0)
    mat = pl.BlockSpec((d, d), const)
    vec = pl.BlockSpec((1, d), const)
    body = functools.partial(_post_body, d_model=d)
    return pl.pallas_call(
        body,
        grid=(t // tm,),
        in_specs=[
            pl.BlockSpec((tm, d), row), pl.BlockSpec((tm, d), row), pl.BlockSpec((tm, d), row),
            pl.BlockSpec((tm, 2 * d), row),
            vec, vec, mat, mat, mat, vec,
            pl.BlockSpec((d, N_EXPERTS), const), pl.BlockSpec((1, N_EXPERTS), const),
        ],
        out_specs=[
            pl.BlockSpec((tm, d), row), pl.BlockSpec((tm, d // 2), row),
            pl.BlockSpec((tm, TOP_K), row), pl.BlockSpec((tm, TOP_K), row),
        ],
        out_shape=[
            jax.ShapeDtypeStruct((t, d), F32), jax.ShapeDtypeStruct((t, d // 2), U32),
            jax.ShapeDtypeStruct((t, TOP_K), I32), jax.ShapeDtypeStruct((t, TOP_K), F32),
        ],
        compiler_params=_cparams(1),
        name="post",
    )(x, conv, o, gate, ln_g, ln_b, wc, wa, wo, g_ffn, wr, br)


def _lane_cumsum(x):
    lane = lax.broadcasted_iota(I32, x.shape, 1)
    s = 1
    while s < LANES:
        x = x + jnp.where(lane >= s, pltpu.roll(x, s, 1), 0.0)
        s *= 2
    return x


def _wrap_i32(value):
    return (value + 2 ** 31) % 2 ** 32 - 2 ** 31


def _route_word(token, slot):
    dst = ((token // COMBINE_TILE) * (TOP_K * COMBINE_TILE) + slot * COMBINE_TILE
           + token % COMBINE_TILE)
    return jnp.left_shift(dst, ROUTE_TOKEN_BITS) | token


def _positions_body(ids_ref, pos_ref, word_ref, te_ref, count_ref, start_ref, *, n_tiles_pad):
    p = pl.program_id(0)
    i = pl.program_id(1)
    ids = ids_ref[...]
    tt = ids.shape[0]
    lane = lax.broadcasted_iota(I32, (tt, LANES), 1)
    onehots = [ids[:, k:k + 1] == lane for k in range(TOP_K)]
    tile_counts = [jnp.sum(oh.astype(F32), axis=0, keepdims=True) for oh in onehots]
    tile_total = tile_counts[0] + tile_counts[1] + tile_counts[2] + tile_counts[3]

    @pl.when((p == 0) & (i == 0))
    def _():
        count_ref[...] = jnp.zeros(count_ref.shape, F32)

    @pl.when((p == 1) & (i == 0))
    def _():
        counts = count_ref[...]
        padded = jnp.ceil(counts * (1.0 / MOE_TILE)) * MOE_TILE
        ends = _lane_cumsum(padded)
        start_ref[...] = ends - padded
        count_ref[...] = jnp.zeros(count_ref.shape, F32)
        tile_start = (lax.broadcasted_iota(I32, (n_tiles_pad, LANES), 0) * MOE_TILE).astype(F32)
        elane = lax.broadcasted_iota(I32, (n_tiles_pad, LANES), 1)
        done = (ends[0:1, :] <= tile_start) & (elane < N_EXPERTS)
        n_done = jnp.sum(done.astype(F32), axis=-1, keepdims=True)
        te_ref[...] = jnp.broadcast_to(n_done, (n_tiles_pad, LANES)).astype(I32)

    @pl.when(p == 1)
    def _():
        r = lax.broadcasted_iota(I32, (tt, tt), 0)
        c = lax.broadcasted_iota(I32, (tt, tt), 1)
        earlier = (c < r).astype(BF16)
        run = start_ref[0:1, :] + count_ref[0:1, :]
        cols = []
        for k in range(TOP_K):
            within = jnp.dot(earlier, onehots[k].astype(BF16), preferred_element_type=F32)
            cols.append(jnp.sum(jnp.where(onehots[k], within + run, 0.0),
                                axis=-1, keepdims=True))
            run = run + tile_counts[k]
        pos_ref[...] = _columns(cols).astype(I32)
        token = lax.broadcasted_iota(I32, (tt, TOP_K), 0) + i * tt
        slot = lax.broadcasted_iota(I32, (tt, TOP_K), 1)
        word_ref[...] = _route_word(token, slot)

    count_ref[...] = count_ref[...] + tile_total


def _positions(ids, n_tiles_pad):
    t = ids.shape[0]
    tt = POS_TILE
    body = functools.partial(_positions_body, n_tiles_pad=n_tiles_pad)
    return pl.pallas_call(
        body,
        grid=(2, t // tt),
        in_specs=[pl.BlockSpec((tt, TOP_K), lambda p, i: (i, 0))],
        out_specs=[
            pl.BlockSpec((tt, TOP_K), lambda p, i: (i * p, 0)),
            pl.BlockSpec((tt, TOP_K), lambda p, i: (i * p, 0)),
            pl.BlockSpec((n_tiles_pad, LANES), lambda p, i: (0, 0)),
        ],
        out_shape=[
            jax.ShapeDtypeStruct((t, TOP_K), I32),
            jax.ShapeDtypeStruct((t, TOP_K), I32),
            jax.ShapeDtypeStruct((n_tiles_pad, LANES), I32),
        ],
        scratch_shapes=[pltpu.VMEM((8, LANES), F32), pltpu.VMEM((8, LANES), F32)],
        compiler_params=_cparams(2),
        name="moe_positions",
    )(ids)


def _inverse_body(pos_ref, word_ref, init_hbm, inv_hbm, inv_smem, sem):
    c = pl.program_id(0)

    @pl.when(c == 0)
    def _():
        load = pltpu.make_async_copy(init_hbm, inv_smem, sem)
        load.start()
        load.wait()

    def place(a, carry):
        inv_smem[pos_ref[a]] = word_ref[a]
        return carry

    lax.fori_loop(0, pos_ref.shape[0], place, 0, unroll=8)

    @pl.when(c == pl.num_programs(0) - 1)
    def _():
        store = pltpu.make_async_copy(inv_smem, inv_hbm, sem)
        store.start()
        store.wait()


def _inverse(pos_flat, word_flat, init):
    n = pos_flat.shape[0]
    chunk = INVERSE_CHUNK
    assert n % chunk == 0
    smem_chunk = pl.BlockSpec((chunk,), lambda c: (c,), memory_space=pltpu.SMEM)
    return pl.pallas_call(
        _inverse_body,
        grid=(n // chunk,),
        in_specs=[smem_chunk, smem_chunk, pl.BlockSpec(memory_space=pl.ANY)],
        out_specs=pl.BlockSpec(memory_space=pl.ANY),
        out_shape=jax.ShapeDtypeStruct(init.shape, I32),
        scratch_shapes=[pltpu.SMEM(init.shape, I32), pltpu.SemaphoreType.DMA],
        compiler_params=_cparams(1),
        name="moe_inverse",
    )(pos_flat, word_flat, init)


def _experts_body(te_ref, nv_ref, inv_next, inv_prev, w1_ref, b1_ref, w2_ref, b2_ref, tok_hbm,
                  out_hbm, xs_buf, y_buf, act_buf, w1b, w2b, gsem, ssem, *, n_tok, n_out):
    j = pl.program_id(0)
    nv = nv_ref[0]
    _, tm, half = xs_buf.shape
    d = 2 * half
    d_ff = w2b.shape[0]
    slot = lax.rem(j, 2)
    other = 1 - slot
    token_mask = (1 << ROUTE_TOKEN_BITS) - 1

    def gather_row(word, r, buf_slot):
        return pltpu.make_async_copy(tok_hbm.at[pl.ds(word & token_mask, 1)],
                                     xs_buf.at[buf_slot, pl.ds(r, 1)], gsem)

    def scatter_row(word, r, buf_slot):
        dst = lax.shift_right_logical(word, ROUTE_TOKEN_BITS)
        return pltpu.make_async_copy(y_buf.at[buf_slot, pl.ds(r, 1)],
                                     out_hbm.at[pl.ds(dst, 1)], ssem)

    def wait_gather(buf_slot):
        pltpu.make_async_copy(tok_hbm.at[pl.ds(0, tm)], xs_buf.at[buf_slot], gsem).wait()

    def wait_scatter(buf_slot):
        pltpu.make_async_copy(y_buf.at[buf_slot], out_hbm.at[pl.ds(0, tm)], ssem).wait()

    @pl.when(j == 0)
    def _():
        y_buf[...] = jnp.zeros(y_buf.shape, F32)
        for r in range(tm):
            gather_row(inv_prev[0, 0, r], r, 0).start()

    @pl.when(j < nv)
    def _():
        wait_gather(slot)

        @pl.when((j == 0) | (te_ref[j] != te_ref[jnp.maximum(j - 1, 0)]))
        def _():
            for c in range(0, d, LANES):
                w1b[c:c + LANES, :] = w1_ref[0, c:c + LANES, :].astype(BF16)
            for c in range(0, d_ff, LANES):
                w2b[c:c + LANES, :] = w2_ref[0, c:c + LANES, :].astype(BF16)

        has_prev = j > 0

        def move_rows(r0, r1):
            for r in range(r0, r1):
                gather_row(inv_next[0, 0, r], r, other).start()
                spare = _wrap_i32((n_out + r) << ROUTE_TOKEN_BITS)
                scatter_row(jnp.where(has_prev, inv_prev[0, 0, r], spare), r, other).start()

        bits = xs_buf[slot]
        x_lo = pltpu.bitcast(bits << 16, F32).astype(BF16)
        x_hi = pltpu.bitcast(bits & jnp.uint32(0xFFFF0000), F32).astype(BF16)
        ch = EXPERT_CHUNK
        n_chunks = d_ff // ch + d // ch
        per_chunk = tm // n_chunks
        done = 0

        def proj1(c0):
            return (jnp.dot(x_lo, w1b[:half, c0:c0 + ch], preferred_element_type=F32)
                    + jnp.dot(x_hi, w1b[half:, c0:c0 + ch], preferred_element_type=F32)
                    + b1_ref[0, :, c0:c0 + ch])

        for c in range(0, d_ff, ch):
            gate = jnp.minimum(proj1(c), SWIGLU_LIMIT)
            lin = jnp.clip(proj1(d_ff + c), -SWIGLU_LIMIT, SWIGLU_LIMIT)
            act_buf[:, c:c + ch] = (gate * jax.nn.sigmoid(SWIGLU_ALPHA * gate)
                                    * (lin + 1.0)).astype(BF16)
            move_rows(done, done + per_chunk)
            done += per_chunk
        for c in range(0, d, ch):
            y_buf[slot, :, c:c + ch] = (
                jnp.dot(act_buf[...], w2b[:, c:c + ch], preferred_element_type=F32)
                + b2_ref[0, :, c:c + ch])
            move_rows(done, done + per_chunk)
            done += per_chunk
        move_rows(done, tm)
        wait_scatter(other)

    @pl.when(j == nv)
    def _():
        wait_gather(slot)
        for r in range(tm):
            scatter_row(inv_prev[0, 0, r], r, other).start()
        wait_scatter(other)


def _experts(tile_expert, n_valid, inv3, tokens, w1, b1, w2, b2, n_tok, n_out):
    nt, _, tm = inv3.shape
    half = tokens.shape[1]
    d = 2 * half
    d_ff = w2.shape[1]
    body = functools.partial(_experts_body, n_tok=n_tok, n_out=n_out)
    smem_tile = lambda index_map: pl.BlockSpec((1, 1, tm), index_map, memory_space=pltpu.SMEM)
    grid_spec = pltpu.PrefetchScalarGridSpec(
        num_scalar_prefetch=2,
        grid=(nt,),
        in_specs=[
            smem_tile(lambda j, te, nv: (jnp.minimum(j + 1, nt - 1), 0, 0)),
            smem_tile(lambda j, te, nv: (jnp.maximum(j - 1, 0), 0, 0)),
            pl.BlockSpec((1, d, 2 * d_ff), lambda j, te, nv: (te[j], 0, 0)),
            pl.BlockSpec((1, 1, 2 * d_ff), lambda j, te, nv: (te[j], 0, 0)),
            pl.BlockSpec((1, d_ff, d), lambda j, te, nv: (te[j], 0, 0)),
            pl.BlockSpec((1, 1, d), lambda j, te, nv: (te[j], 0, 0)),
            pl.BlockSpec(memory_space=pl.ANY),
        ],
        out_specs=pl.BlockSpec(memory_space=pl.ANY),
        scratch_shapes=[
            pltpu.VMEM((2, tm, half), U32),
            pltpu.VMEM((2, tm, d), F32),
            pltpu.VMEM((tm, d_ff), BF16),
            pltpu.VMEM((d, 2 * d_ff), BF16),
            pltpu.VMEM((d_ff, d), BF16),
            pltpu.SemaphoreType.DMA,
            pltpu.SemaphoreType.DMA,
        ],
    )
    return pl.pallas_call(
        body,
        grid_spec=grid_spec,
        out_shape=jax.ShapeDtypeStruct((n_out + tm, d), F32),
        compiler_params=_cparams(1),
        name="moe_experts",
    )(tile_expert, n_valid, inv3, inv3, w1, b1, w2, b2, tokens)


def _combine_body(tw_ref, x1_ref, gf_ref, rows_ref, y_ref):
    tt = x1_ref.shape[0]
    tw = tw_ref[...]
    moe = tw[:, 0:1] * rows_ref[0:tt, :]
    for k in range(1, TOP_K):
        moe = moe + tw[:, k:k + 1] * rows_ref[k * tt:(k + 1) * tt, :]
    y_ref[...] = _rms(x1_ref[...] + moe, gf_ref[...])


def _combine(tw, x1, g_final, expert_rows, first_tile):
    t, d = x1.shape
    tt = COMBINE_TILE
    return pl.pallas_call(
        _combine_body,
        grid=(t // tt,),
        in_specs=[
            pl.BlockSpec((tt, TOP_K), lambda i: (i, 0)),
            pl.BlockSpec((tt, d), lambda i: (i, 0)),
            pl.BlockSpec((1, d), lambda i: (0, 0)),
            pl.BlockSpec((TOP_K * tt, d), lambda i: (first_tile + i, 0)),
        ],
        out_specs=pl.BlockSpec((tt, d), lambda i: (i, 0)),
        out_shape=jax.ShapeDtypeStruct((t, d), F32),
        compiler_params=_cparams(1),
        name="moe_combine",
    )(tw, x1, g_final, expert_rows)


def kernel(x_prompt, x_sample, cache_k, cache_v, state_conv, page_table, g_mix, w_in, b_in,
           w_dw, b_dw, ln_g, ln_b, w_conv_out, lam_q1, lam_k1, lam_q2, lam_k2, subln_g,
           w_attn_out, w_o, g_ffn, w_router, b_router, w_moe1, b_moe1, w_moe2, b_moe2, g_final):
    bsz, seq, d = x_prompt.shape
    dec_b, dec_s, _ = x_sample.shape
    depth = g_mix.shape[0]
    c_conv = w_dw.shape[2]
    attn_w = N_HEADS * HEAD_W
    n_pages = page_table.shape[1]
    past_len = n_pages * PAGE_SIZE
    t_p, t_s = bsz * seq, dec_b * dec_s
    t_all = t_p + t_s
    d_ff = w_moe2.shape[2]
    assert depth == 1, "the combine kernel fuses the final norm, so only one layer is supported"
    assert seq % ROW_TILE == 0 and t_s % ROW_TILE == 0 and seq % CONV_ROWS == 0
    assert seq % ATTN_BLOCK == 0 and t_all % POS_TILE == 0 and dec_b % 8 == 0
    assert t_p % COMBINE_TILE == 0 and t_s % COMBINE_TILE == 0

    tab_p = _rope_tables(jnp.arange(seq))
    tab_s = _rope_tables(jnp.tile(past_len + jnp.arange(dec_s), dec_b))
    n_rows = t_all * TOP_K + N_EXPERTS * MOE_TILE
    n_tiles = n_rows // MOE_TILE
    row2 = lambda v: v.reshape(1, -1)

    hp, hs = x_prompt.reshape(t_p, d), x_sample.reshape(t_s, d)
    outs = [[] for _ in range(6)]
    for l in range(depth):
        lam_init = 0.8 - 0.6 * math.exp(-0.3 * l)
        lams = (row2(lam_q1[l]), row2(lam_k1[l]), row2(lam_q2[l]), row2(lam_k2[l]))
        subg = row2(subln_g[l])
        w_in_bf = w_in[l].astype(BF16)
        wc, wa, wo = (w_conv_out[l].astype(BF16), w_attn_out[l].astype(BF16),
                      w_o[l].astype(BF16))
        wr = w_router[l].astype(BF16)
        b1 = b_moe1[l].reshape(N_EXPERTS, 1, 2 * d_ff)
        b2 = b_moe2[l].reshape(N_EXPERTS, 1, d)
        proj_args = (row2(g_mix[l]), w_in_bf, row2(b_in[l]))
        post_args = (row2(ln_g[l]), row2(ln_b[l]), wc, wa, wo, row2(g_ffn[l]), wr,
                     row2(b_router[l]))

        a_p, q_p, k_p, v_p, gate_p = _in_proj(hp, *proj_args, tab_p, seq // ROW_TILE,
                                              c_conv, attn_w)
        a_p3 = a_p.reshape(bsz, seq, c_conv)
        conv_p = _conv_prompt(a_p3, w_dw[l], row2(b_dw[l]))
        o_p = _attn_prompt(q_p.reshape(bsz, seq, attn_w), k_p.reshape(bsz, seq, attn_w),
                           v_p.reshape(bsz, seq, attn_w), lams, subg, lam_init)
        x1_p, hpk_p, ids_p, tw_p = _post(hp, conv_p.reshape(t_p, c_conv),
                                         o_p.reshape(t_p, attn_w), gate_p, *post_args)

        a_s, q_s, k_s, v_s, gate_s = _in_proj(hs, *proj_args, tab_s, 1, c_conv, attn_w)
        a_s3 = a_s.reshape(dec_b, dec_s, c_conv)
        conv_s = _conv_decode(state_conv[l], a_s3, w_dw[l], row2(b_dw[l]))
        pool = cache_k.shape[1]
        o_s = _attn_decode(page_table, q_s.astype(F32).reshape(dec_b, dec_s, attn_w),
                           k_s.reshape(dec_b, dec_s, attn_w), v_s.reshape(dec_b, dec_s, attn_w),
                           cache_k[l].reshape(pool * PAGE_SIZE * N_HEADS, HEAD_W),
                           cache_v[l].reshape(pool * PAGE_SIZE * N_HEADS, HEAD_W),
                           lams, subg, lam_init)
        x1_s, hpk_s, ids_s, tw_s = _post(hs, conv_s.reshape(t_s, c_conv),
                                         o_s.reshape(t_s, attn_w).astype(BF16), gate_s,
                                         *post_args)

        ids = jnp.concatenate([ids_p, ids_s], axis=0)
        pos, word, te = _positions(ids, n_tiles)
        tile_expert = jnp.minimum(te[:, 0], N_EXPERTS - 1)
        n_valid = jnp.sum((te[:, 0] < N_EXPERTS).astype(I32)).reshape(1)
        slot_idx = jnp.arange(n_rows, dtype=U32)
        n_out = t_all * TOP_K
        unused = (((n_out + slot_idx % MOE_TILE) << ROUTE_TOKEN_BITS)
                  | (t_all + slot_idx % PAD_TOKENS))
        inv = _inverse(pos.reshape(-1), word.reshape(-1), lax.bitcast_convert_type(unused, I32))
        tokens = jnp.concatenate([hpk_p, hpk_s, jnp.zeros((PAD_TOKENS, d // 2), U32)], axis=0)
        expert_rows = _experts(tile_expert, n_valid, inv.reshape(n_tiles, 1, MOE_TILE), tokens,
                               w_moe1[l], b1, w_moe2[l], b2, t_all, n_out)
        gf = row2(g_final)
        hp = _combine(tw_p, x1_p, gf, expert_rows, 0)
        hs = _combine(tw_s, x1_s, gf, expert_rows, t_p // COMBINE_TILE)

        outs[0].append(k_p.reshape(bsz, seq, N_HEADS, HEAD_W))
        outs[1].append(v_p.reshape(bsz, seq, N_HEADS, HEAD_W))
        outs[2].append(a_p3[:, seq - (CONV_WIDTH - 1):])
        outs[3].append(k_s.reshape(dec_b, dec_s, N_HEADS, HEAD_W))
        outs[4].append(v_s.reshape(dec_b, dec_s, N_HEADS, HEAD_W))
        outs[5].append(jnp.concatenate([state_conv[l], a_s3], axis=1)[:, -(CONV_WIDTH - 1):])

    y_prompt = hp.reshape(bsz, seq, d)
    y_sample = hs.reshape(dec_b, dec_s, d)
    return (y_prompt, y_sample) + tuple(jnp.stack(o) for o in outs)
```

```python
import functools
import math

import jax
import jax.numpy as jnp
from jax import lax
from jax.experimental import pallas as pl
from jax.experimental.pallas import tpu as pltpu

F32 = jnp.float32
BF16 = jnp.bfloat16
I32 = jnp.int32
U32 = jnp.uint32

N_HEADS = 8
HEAD_DIM = 64
HEAD_W = 2 * HEAD_DIM
ROT_DIM = HEAD_DIM // 4
ROT_HALF = ROT_DIM // 2
ROPE_THETA = 500000.0
CONV_WIDTH = 31
CONV_HALO = 32
N_EXPERTS = 32
TOP_K = 4
SWIGLU_ALPHA = 1.702
SWIGLU_LIMIT = 7.0
RMS_EPS = 1e-5
LN_EPS = 1e-5
PAGE_SIZE = 128
LANES = 128
NEG_BIG = -1e30

ROW_TILE = 512
ATTN_BLOCK = 512
ATTN_SUB = 128
CONV_ROWS = 1024
CONV_CHUNK = 64
MOE_TILE = 256
EXPERT_CHUNK = 256
POS_TILE = 512
COMBINE_TILE = 256
ROUTE_TOKEN_BITS = 15
INVERSE_CHUNK = 11 * 1024
PAD_TOKENS = 64
VMEM_LIMIT = 56 * 1024 * 1024


def _cparams(n_axes, vmem=VMEM_LIMIT):
    return pltpu.CompilerParams(dimension_semantics=("arbitrary",) * n_axes,
                                vmem_limit_bytes=vmem)


def _rms(x, g):
    return x * lax.rsqrt(jnp.mean(x * x, axis=-1, keepdims=True) + RMS_EPS) * g


def _columns(cols):
    rows = cols[0].shape[0]
    lane = lax.broadcasted_iota(I32, (rows, len(cols)), 1)
    out = jnp.zeros((rows, len(cols)), cols[0].dtype)
    for k, col in enumerate(cols):
        out = jnp.where(lane == k, col, out)
    return out


def _lam(lq1, lk1, lq2, lk2, lam_init):
    s1 = jnp.sum(lq1[...] * lk1[...], axis=-1, keepdims=True)
    s2 = jnp.sum(lq2[...] * lk2[...], axis=-1, keepdims=True)
    return jnp.exp(s1) - jnp.exp(s2) + lam_init


def _rope_tables(pos):
    inv = jnp.power(jnp.float32(ROPE_THETA),
                    -jnp.arange(ROT_HALF, dtype=F32) * (2.0 / ROT_DIM))
    ang = pos.astype(F32)[:, None] * inv[None, :]
    cos, sin = jnp.cos(ang), jnp.sin(ang)
    n = pos.shape[0]
    rest = HEAD_DIM - ROT_DIM
    zh = jnp.zeros((n, ROT_HALF), F32)
    c64 = jnp.concatenate([cos, cos, jnp.ones((n, rest), F32)], axis=-1)
    sa64 = jnp.concatenate([-sin, zh, jnp.zeros((n, rest), F32)], axis=-1)
    sb64 = jnp.concatenate([zh, sin, jnp.zeros((n, rest), F32)], axis=-1)
    tile = lambda t: jnp.concatenate([t, t], axis=-1)
    return tile(c64), tile(sa64), tile(sb64)


def _in_proj_body(x_ref, g_ref, w_ref, b_ref, cos_ref, sa_ref, sb_ref,
                  a_ref, q_ref, k_ref, v_ref, gate_ref, *, c_conv, attn_w, d_model):
    h = _rms(x_ref[...], g_ref[...]).astype(BF16)
    cos, sa, sb = cos_ref[...], sa_ref[...], sb_ref[...]
    ch = 512

    def proj(c0):
        return (jnp.dot(h, w_ref[:, c0:c0 + ch], preferred_element_type=F32)
                + b_ref[:, c0:c0 + ch])

    def rope(z):
        outs = []
        for j in range(ch // LANES):
            zj = z[:, j * LANES:(j + 1) * LANES]
            outs.append(zj * cos + pltpu.roll(zj, LANES - ROT_HALF, 1) * sa
                        + pltpu.roll(zj, ROT_HALF, 1) * sb)
        return jnp.concatenate(outs, axis=-1)

    for c in range(0, c_conv, ch):
        a_ref[:, c:c + ch] = proj(c) * jax.nn.sigmoid(proj(c_conv + c))
    base = 2 * c_conv
    for c in range(0, attn_w, ch):
        q_ref[:, c:c + ch] = (rope(proj(base + c)) * (HEAD_DIM ** -0.5)).astype(BF16)
    base += attn_w
    for c in range(0, attn_w, ch):
        k_ref[:, c:c + ch] = rope(proj(base + c))
    base += attn_w
    for c in range(0, attn_w, ch):
        v_ref[:, c:c + ch] = proj(base + c)
    base += attn_w
    for c in range(0, 2 * d_model, ch):
        gate_ref[:, c:c + ch] = jax.nn.sigmoid(proj(base + c))


def _in_proj(x, g_mix, w_in_bf, b_in, tables, n_tab_blocks, c_conv, attn_w):
    t, d = x.shape
    n_in = w_in_bf.shape[1]
    tm = ROW_TILE
    row = lambda i: (i, 0)
    const = lambda i: (0, 0)
    tab = lambda i: (i % n_tab_blocks, 0)
    body = functools.partial(_in_proj_body, c_conv=c_conv, attn_w=attn_w, d_model=d)
    return pl.pallas_call(
        body,
        grid=(t // tm,),
        in_specs=[
            pl.BlockSpec((tm, d), row),
            pl.BlockSpec((1, d), const),
            pl.BlockSpec((d, n_in), const, pipeline_mode=pl.Buffered(1)),
            pl.BlockSpec((1, n_in), const),
            pl.BlockSpec((tm, LANES), tab),
            pl.BlockSpec((tm, LANES), tab),
            pl.BlockSpec((tm, LANES), tab),
        ],
        out_specs=[
            pl.BlockSpec((tm, c_conv), row),
            pl.BlockSpec((tm, attn_w), row),
            pl.BlockSpec((tm, attn_w), row),
            pl.BlockSpec((tm, attn_w), row),
            pl.BlockSpec((tm, 2 * d), row),
        ],
        out_shape=[
            jax.ShapeDtypeStruct((t, c_conv), F32),
            jax.ShapeDtypeStruct((t, attn_w), BF16),
            jax.ShapeDtypeStruct((t, attn_w), F32),
            jax.ShapeDtypeStruct((t, attn_w), F32),
            jax.ShapeDtypeStruct((t, 2 * d), F32),
        ],
        compiler_params=_cparams(1),
        name="in_proj",
    )(x, g_mix, w_in_bf, b_in, *tables)


def _conv_prompt_body(a_ref, halo_ref, w_ref, b_ref, o_ref, ext_ref, *, rows):
    i = pl.program_id(2)
    keep = jnp.where(i > 0, 1.0, 0.0).astype(F32)
    ext_ref[0:CONV_HALO, :] = halo_ref[0] * keep
    ext_ref[CONV_HALO:, :] = a_ref[0]
    w = w_ref[...]
    bias = jnp.broadcast_to(b_ref[...], (CONV_CHUNK, LANES))
    off = CONV_HALO - (CONV_WIDTH - 1)
    for r0 in range(0, rows, CONV_CHUNK):
        acc = bias
        for j in range(CONV_WIDTH):
            acc = acc + w[j:j + 1, :] * ext_ref[r0 + off + j:r0 + off + j + CONV_CHUNK, :]
        o_ref[0, r0:r0 + CONV_CHUNK, :] = acc


def _conv_prompt(a3, w_dw, b_dw):
    bsz, seq, c = a3.shape
    rows = CONV_ROWS
    hb = rows // CONV_HALO
    body = functools.partial(_conv_prompt_body, rows=rows)
    return pl.pallas_call(
        body,
        grid=(bsz, c // LANES, seq // rows),
        in_specs=[
            pl.BlockSpec((1, rows, LANES), lambda b, g, i: (b, i, g)),
            pl.BlockSpec((1, CONV_HALO, LANES),
                         lambda b, g, i: (b, jnp.maximum(i * hb - 1, 0), g)),
            pl.BlockSpec((CONV_WIDTH, LANES), lambda b, g, i: (0, g)),
            pl.BlockSpec((1, LANES), lambda b, g, i: (0, g)),
        ],
        out_specs=pl.BlockSpec((1, rows, LANES), lambda b, g, i: (b, i, g)),
        out_shape=jax.ShapeDtypeStruct((bsz, seq, c), F32),
        scratch_shapes=[pltpu.VMEM((rows + CONV_HALO, LANES), F32)],
        compiler_params=_cparams(3),
        name="conv_prompt",
    )(a3, a3, w_dw, b_dw)


def _conv_decode_body(state_ref, a_ref, w_ref, b_ref, o_ref, ext_ref, *, n_state, n_new):
    ext_ref[:, 0:n_state, :] = state_ref[...]
    ext_ref[:, n_state:n_state + n_new, :] = a_ref[...]
    w = w_ref[...]
    for t in range(n_new):
        win = ext_ref[:, t:t + CONV_WIDTH, :]
        o_ref[:, t:t + 1, :] = (jnp.sum(win * w[None], axis=1, keepdims=True)
                                + b_ref[...][None])


def _conv_decode(state, a3, w_dw, b_dw):
    bsz, n_state, c = state.shape
    n_new = a3.shape[1]
    bb = 8
    body = functools.partial(_conv_decode_body, n_state=n_state, n_new=n_new)
    return pl.pallas_call(
        body,
        grid=(bsz // bb,),
        in_specs=[
            pl.BlockSpec((bb, n_state, c), lambda i: (i, 0, 0)),
            pl.BlockSpec((bb, n_new, c), lambda i: (i, 0, 0)),
            pl.BlockSpec((CONV_WIDTH, c), lambda i: (0, 0)),
            pl.BlockSpec((1, c), lambda i: (0, 0)),
        ],
        out_specs=pl.BlockSpec((bb, n_new, c), lambda i: (i, 0, 0)),
        out_shape=jax.ShapeDtypeStruct((bsz, n_new, c), F32),
        scratch_shapes=[pltpu.VMEM((bb, n_state + n_new + 6, c), F32)],
        compiler_params=_cparams(1),
        name="conv_decode",
    )(state, a3, w_dw, b_dw)


def _attn_prompt_body(lq1, lk1, lq2, lk2, subg_ref, q_ref, k_ref, v_ref, o_ref,
                      kt_ref, vb_ref, acc1, acc2, m1, l1, m2, l2, *, blk, sub, n_blk, lam_init):
    i = pl.program_id(2)

    @pl.when(i == 0)
    def _():
        for c in range(n_blk):
            kt_ref[c] = k_ref[0, c * blk:(c + 1) * blk, :].T.astype(BF16)
            vb_ref[c] = v_ref[0, c * blk:(c + 1) * blk, :].astype(BF16)

    for m_ref, l_ref, acc_ref in ((m1, l1, acc1), (m2, l2, acc2)):
        m_ref[...] = jnp.full(m_ref.shape, NEG_BIG, F32)
        l_ref[...] = jnp.zeros(l_ref.shape, F32)
        acc_ref[...] = jnp.zeros(acc_ref.shape, F32)

    stats = ((m1, l1, acc1), (m2, l2, acc2))
    n_sub = blk // sub

    def scores(kb, sb, masked):
        r0 = sb * sub
        ncol = r0 + sub if masked else blk
        out = []
        for c in range(2):
            qq = q_ref[0, r0:r0 + sub, c * HEAD_DIM:(c + 1) * HEAD_DIM]
            kk = kt_ref[kb, c * HEAD_DIM:(c + 1) * HEAD_DIM, 0:ncol]
            out.append(jnp.dot(qq, kk, preferred_element_type=F32))
        return out

    def softmax_values(kb, sb, masked, s_pair):
        r0 = sb * sub
        rows = slice(r0, r0 + sub)
        ncol = r0 + sub if masked else blk
        ps, alphas = [], []
        for s, (m_ref, l_ref, _) in zip(s_pair, stats):
            if masked:
                row = lax.broadcasted_iota(I32, (sub, ncol), 0) + r0
                col = lax.broadcasted_iota(I32, (sub, ncol), 1)
                s = jnp.where(col <= row, s, NEG_BIG)
            m_old = m_ref[rows, :]
            m_new = jnp.maximum(m_old, jnp.max(s, axis=-1, keepdims=True))
            alpha = jnp.exp(m_old - m_new)
            p = jnp.exp(s - jnp.concatenate([m_new] * (ncol // LANES), axis=1))
            l_ref[rows, :] = alpha * l_ref[rows, :] + jnp.sum(p, axis=-1, keepdims=True)
            m_ref[rows, :] = m_new
            ps.append(p.astype(BF16))
            alphas.append(alpha)
        pv = jnp.dot(jnp.concatenate(ps, axis=0), vb_ref[kb, 0:ncol, :],
                     preferred_element_type=F32)
        for c, (alpha, (_, _, acc_ref)) in enumerate(zip(alphas, stats)):
            acc_ref[rows, :] = alpha * acc_ref[rows, :] + pv[c * sub:(c + 1) * sub]

    def run(chains):
        ahead = 2
        pending = {n: scores(*chains[n]) for n in range(min(ahead, len(chains)))}
        for n, chain in enumerate(chains):
            if n + ahead < len(chains):
                pending[n + ahead] = scores(*chains[n + ahead])
            softmax_values(*chain, pending.pop(n))

    def block(kb, masked):
        return [(kb, sb, masked) for sb in range(n_sub)]

    def pair_body(t, carry):
        run(block(2 * t, False) + block(2 * t + 1, False))
        return carry

    lax.fori_loop(0, lax.shift_right_logical(i, 1), pair_body, 0)
    odd = (i & 1) == 1

    @pl.when(odd)
    def _():
        run(block(i - 1, False) + block(i, True))

    @pl.when(jnp.logical_not(odd))
    def _():
        run(block(i, True))

    lam = _lam(lq1, lk1, lq2, lk2, lam_init)
    o = acc1[...] / l1[...] - lam * (acc2[...] / l2[...])
    o_ref[0] = (_rms(o, subg_ref[...]) * (1.0 - lam_init)).astype(BF16)


def _attn_prompt(q3, k3, v3, lams, subln_g, lam_init):
    bsz, seq, _ = q3.shape
    blk = ATTN_BLOCK
    n_blk = seq // blk
    vec = pl.BlockSpec((1, HEAD_DIM), lambda b, h, i: (0, 0))
    body = functools.partial(_attn_prompt_body, blk=blk, sub=ATTN_SUB, n_blk=n_blk,
                             lam_init=lam_init)
    return pl.pallas_call(
        body,
        grid=(bsz, N_HEADS, n_blk),
        in_specs=[
            vec, vec, vec, vec,
            pl.BlockSpec((1, HEAD_W), lambda b, h, i: (0, 0)),
            pl.BlockSpec((1, blk, HEAD_W), lambda b, h, i: (b, i, h)),
            pl.BlockSpec((1, seq, HEAD_W), lambda b, h, i: (b, 0, h)),
            pl.BlockSpec((1, seq, HEAD_W), lambda b, h, i: (b, 0, h)),
        ],
        out_specs=pl.BlockSpec((1, blk, HEAD_W), lambda b, h, i: (b, i, h)),
        out_shape=jax.ShapeDtypeStruct(q3.shape, BF16),
        scratch_shapes=[
            pltpu.VMEM((n_blk, HEAD_W, blk), BF16),
            pltpu.VMEM((n_blk, blk, HEAD_W), BF16),
            pltpu.VMEM((blk, HEAD_W), F32),
            pltpu.VMEM((blk, HEAD_W), F32),
            pltpu.VMEM((blk, LANES), F32),
            pltpu.VMEM((blk, LANES), F32),
            pltpu.VMEM((blk, LANES), F32),
            pltpu.VMEM((blk, LANES), F32),
        ],
        compiler_params=_cparams(3),
        name="attn_prompt",
    )(*lams, subln_g, q3, k3, v3)


NEW_PAD = 16
Q_PAD = 8
DECODE_PAGE_GROUP = 4


def _attn_decode_body(pt_ref, lq1, lk1, lq2, lk2, subg_ref, q_ref, kn_ref, vn_ref, *rest,
                      n_pages, n_new, lam_init):
    del pt_ref
    kpages = rest[:n_pages]
    vpages = rest[n_pages:2 * n_pages]
    o_ref, s_ref, stage = rest[2 * n_pages:]
    n_past = n_pages * PAGE_SIZE
    width = stage.shape[1]

    def padded_rows(rows_f32):
        stage[...] = jnp.zeros(stage.shape, F32)
        stage[0:n_new, :] = rows_f32
        return stage[...].astype(BF16)

    q16 = padded_rows(q_ref[0])
    sel_r = lax.broadcasted_iota(I32, (NEW_PAD, LANES), 0)
    sel_c = lax.broadcasted_iota(I32, (NEW_PAD, LANES), 1)
    sel = (sel_c % Q_PAD == sel_r).astype(BF16)
    qrep = lax.dot_general(q16, sel, (((0,), (0,)), ((), ())),
                           preferred_element_type=F32)
    rr = lax.broadcasted_iota(I32, (width, LANES), 0)
    cc = lax.broadcasted_iota(I32, (width, LANES), 1)
    qblk = jnp.where(rr // HEAD_DIM == cc // Q_PAD, qrep, 0.0).astype(BF16)

    def head_major(page_ref):
        return jnp.concatenate(
            [page_ref[pl.ds(h, PAGE_SIZE, stride=N_HEADS), :].astype(BF16)
             for h in range(N_HEADS)], axis=-1)

    groups = [range(j0, min(j0 + DECODE_PAGE_GROUP, n_pages))
              for j0 in range(0, n_pages, DECODE_PAGE_GROUP)]
    row_slices = [slice(g[0] * PAGE_SIZE, (g[-1] + 1) * PAGE_SIZE) for g in groups]

    for g, rows in zip(groups, row_slices):
        keys = jnp.concatenate([head_major(kpages[j]) for j in g], axis=0)
        s_ref[rows, :] = jnp.dot(keys, qblk, preferred_element_type=F32)
    s_new = jnp.dot(padded_rows(kn_ref[0]), qblk, preferred_element_type=F32)
    new_idx = lax.broadcasted_iota(I32, (NEW_PAD, LANES), 0)
    slot = lax.broadcasted_iota(I32, (NEW_PAD, LANES), 1) % Q_PAD
    s_new = jnp.where((new_idx <= slot) & (new_idx < n_new), s_new, NEG_BIG)
    m = jnp.maximum(jnp.max(s_ref[0:n_past, :], axis=0, keepdims=True),
                    jnp.max(s_new, axis=0, keepdims=True))

    contract0 = (((0,), (0,)), ((), ()))
    e_new = jnp.exp(s_new - m)
    denom = jnp.sum(e_new, axis=0, keepdims=True)
    acc = lax.dot_general(e_new.astype(BF16), padded_rows(vn_ref[0]), contract0,
                          preferred_element_type=F32)
    for g, rows in zip(groups, row_slices):
        e = jnp.exp(s_ref[rows, :] - m)
        denom = denom + jnp.sum(e, axis=0, keepdims=True)
        vals = jnp.concatenate([head_major(vpages[j]) for j in g], axis=0)
        acc = acc + lax.dot_general(e.astype(BF16), vals, contract0,
                                    preferred_element_type=F32)

    r_i = lax.broadcasted_iota(I32, (LANES, LANES), 0)
    c_i = lax.broadcasted_iota(I32, (LANES, LANES), 1)
    denom_rows = jnp.sum(jnp.where(r_i == c_i, jnp.broadcast_to(denom, (LANES, LANES)), 0.0),
                         axis=1, keepdims=True)
    o_norm = acc / denom_rows
    lam = _lam(lq1, lk1, lq2, lk2, lam_init)
    outs = []
    for h in range(N_HEADS):
        r0 = h * 2 * Q_PAD
        cols = slice(h * HEAD_W, (h + 1) * HEAD_W)
        oh = o_norm[r0:r0 + Q_PAD, cols] - lam * o_norm[r0 + Q_PAD:r0 + 2 * Q_PAD, cols]
        outs.append(_rms(oh, subg_ref[...]) * (1.0 - lam_init))
    o_ref[0] = jnp.concatenate(outs, axis=-1)[:n_new]


def _attn_decode(page_table, q3, kn3, vn3, cache_k, cache_v, lams, subln_g, lam_init):
    bsz, n_new, width = q3.shape
    n_pages = page_table.shape[1]
    vec = pl.BlockSpec((1, HEAD_DIM), lambda b, pt: (0, 0))
    per_b = pl.BlockSpec((1, n_new, width), lambda b, pt: (b, 0, 0))

    def page_spec(j):
        return pl.BlockSpec((PAGE_SIZE * N_HEADS, HEAD_W), lambda b, pt: (pt[b, j], 0))

    body = functools.partial(_attn_decode_body, n_pages=n_pages, n_new=n_new,
                             lam_init=lam_init)
    grid_spec = pltpu.PrefetchScalarGridSpec(
        num_scalar_prefetch=1,
        grid=(bsz,),
        in_specs=([vec, vec, vec, vec, pl.BlockSpec((1, HEAD_W), lambda b, pt: (0, 0)),
                   per_b, per_b, per_b]
                  + [page_spec(j) for j in range(n_pages)]
                  + [page_spec(j) for j in range(n_pages)]),
        out_specs=per_b,
        scratch_shapes=[pltpu.VMEM((n_pages * PAGE_SIZE, LANES), F32),
                        pltpu.VMEM((NEW_PAD, width), F32)],
    )
    return pl.pallas_call(
        body,
        grid_spec=grid_spec,
        out_shape=jax.ShapeDtypeStruct(q3.shape, F32),
        compiler_params=_cparams(1),
        name="attn_decode",
    )(page_table, *lams, subln_g, q3, kn3, vn3,
      *([cache_k] * n_pages), *([cache_v] * n_pages))


def _post_body(x_ref, conv_ref, o_ref, gate_ref, lng, lnb, wc, wa, wo, gffn, wr, br,
               x1_ref, hp_ref, ids_ref, tw_ref, *, d_model):
    n_groups = 2
    rows_per = x_ref.shape[0] // n_groups
    groups = [slice(g * rows_per, (g + 1) * rows_per) for g in range(n_groups)]
    attn_outs = [jnp.dot(o_ref[r, :], wa[...], preferred_element_type=F32) for r in groups]
    conv_outs = []
    for r in groups:
        c = conv_ref[r, :]
        mu = jnp.mean(c, axis=-1, keepdims=True)
        xc = c - mu
        cn = (xc * lax.rsqrt(jnp.mean(xc * xc, axis=-1, keepdims=True) + LN_EPS) * lng[...]
              + lnb[...])
        cact = (cn * jax.nn.sigmoid(cn)).astype(BF16)
        conv_outs.append(jnp.dot(cact, wc[...], preferred_element_type=F32))
    hbs = []
    for r, conv_out, attn_out in zip(groups, conv_outs, attn_outs):
        merged = (gate_ref[r, :d_model] * conv_out
                  + gate_ref[r, d_model:] * attn_out).astype(BF16)
        x1 = x_ref[r, :] + jnp.dot(merged, wo[...], preferred_element_type=F32)
        x1_ref[r, :] = x1
        hbs.append(_rms(x1, gffn[...]).astype(BF16))
    for r, hb in zip(groups, hbs):
        logits = jnp.dot(hb, wr[...], preferred_element_type=F32) + br[...]
        lane = lax.broadcasted_iota(I32, logits.shape, 1)
        vals, ids = [], []
        cur = logits
        for _ in range(TOP_K):
            mx = jnp.max(cur, axis=-1, keepdims=True)
            idx = jnp.min(jnp.where(cur == mx, lane, N_EXPERTS), axis=-1, keepdims=True)
            vals.append(mx)
            ids.append(idx)
            cur = jnp.where(lane == idx, -jnp.inf, cur)
        es = [jnp.exp(v - vals[0]) for v in vals]
        den = es[0] + es[1] + es[2] + es[3]
        tw_ref[r, :] = _columns([e / den for e in es])
        ids_ref[r, :] = _columns(ids)

        bits = pltpu.bitcast(hb.astype(F32), U32)
        half = d_model // 2
        words = (bits[:, :half] >> 16) | (bits[:, half:] & jnp.uint32(0xFFFF0000))
        pieces = half // LANES
        for c in range(pieces):
            hp_ref[pl.ds(r.start * pieces + c, rows_per, stride=pieces), :] = (
                words[:, c * LANES:(c + 1) * LANES])


def _post(x, conv, o, gate, ln_g, ln_b, wc, wa, wo, g_ffn, wr, br):
    t, d = x.shape
    tm = ROW_TILE
    row = lambda i: (i, 0)
    const = lambda i: (0, 0)
    mat = pl.BlockSpec((d, d), const)
    vec = pl.BlockSpec((1, d), const)
    body = functools.partial(_post_body, d_model=d)
    return pl.pallas_call(
        body,
        grid=(t // tm,),
        in_specs=[
            pl.BlockSpec((tm, d), row), pl.BlockSpec((tm, d), row), pl.BlockSpec((tm, d), row),
            pl.BlockSpec((tm, 2 * d), row),
            vec, vec, mat, mat, mat, vec,
            pl.BlockSpec((d, N_EXPERTS), const), pl.BlockSpec((1, N_EXPERTS), const),
        ],
        out_specs=[
            pl.BlockSpec((tm, d), row), pl.BlockSpec((tm * (d // 2 // LANES), LANES), row),
            pl.BlockSpec((tm, TOP_K), row), pl.BlockSpec((tm, TOP_K), row),
        ],
        out_shape=[
            jax.ShapeDtypeStruct((t, d), F32),
            jax.ShapeDtypeStruct((t * (d // 2 // LANES), LANES), U32),
            jax.ShapeDtypeStruct((t, TOP_K), I32), jax.ShapeDtypeStruct((t, TOP_K), F32),
        ],
        compiler_params=_cparams(1),
        name="post",
    )(x, conv, o, gate, ln_g, ln_b, wc, wa, wo, g_ffn, wr, br)


def _lane_cumsum(x):
    lane = lax.broadcasted_iota(I32, x.shape, 1)
    s = 1
    while s < LANES:
        x = x + jnp.where(lane >= s, pltpu.roll(x, s, 1), 0.0)
        s *= 2
    return x


def _wrap_i32(value):
    return (value + 2 ** 31) % 2 ** 32 - 2 ** 31


def _route_word(token, slot):
    dst = ((token // COMBINE_TILE) * (TOP_K * COMBINE_TILE) + slot * COMBINE_TILE
           + token % COMBINE_TILE)
    return jnp.left_shift(dst, ROUTE_TOKEN_BITS) | token


def _positions_body(ids_ref, pos_ref, word_ref, te_ref, count_ref, start_ref, *, n_tiles_pad):
    p = pl.program_id(0)
    i = pl.program_id(1)
    ids = ids_ref[...]
    tt = ids.shape[0]
    lane = lax.broadcasted_iota(I32, (tt, LANES), 1)
    onehots = [ids[:, k:k + 1] == lane for k in range(TOP_K)]
    tile_counts = [jnp.sum(oh.astype(F32), axis=0, keepdims=True) for oh in onehots]
    tile_total = tile_counts[0] + tile_counts[1] + tile_counts[2] + tile_counts[3]

    @pl.when((p == 0) & (i == 0))
    def _():
        count_ref[...] = jnp.zeros(count_ref.shape, F32)

    @pl.when((p == 1) & (i == 0))
    def _():
        counts = count_ref[...]
        padded = jnp.ceil(counts * (1.0 / MOE_TILE)) * MOE_TILE
        ends = _lane_cumsum(padded)
        start_ref[...] = ends - padded
        count_ref[...] = jnp.zeros(count_ref.shape, F32)
        tile_start = (lax.broadcasted_iota(I32, (n_tiles_pad, LANES), 0) * MOE_TILE).astype(F32)
        elane = lax.broadcasted_iota(I32, (n_tiles_pad, LANES), 1)
        done = (ends[0:1, :] <= tile_start) & (elane < N_EXPERTS)
        n_done = jnp.sum(done.astype(F32), axis=-1, keepdims=True)
        te_ref[...] = jnp.broadcast_to(n_done, (n_tiles_pad, LANES)).astype(I32)

    @pl.when(p == 1)
    def _():
        r = lax.broadcasted_iota(I32, (tt, tt), 0)
        c = lax.broadcasted_iota(I32, (tt, tt), 1)
        earlier = (c < r).astype(BF16)
        run = start_ref[0:1, :] + count_ref[0:1, :]
        cols = []
        for k in range(TOP_K):
            within = jnp.dot(earlier, onehots[k].astype(BF16), preferred_element_type=F32)
            cols.append(jnp.sum(jnp.where(onehots[k], within + run, 0.0),
                                axis=-1, keepdims=True))
            run = run + tile_counts[k]
        pos_ref[...] = _columns(cols).astype(I32)
        token = lax.broadcasted_iota(I32, (tt, TOP_K), 0) + i * tt
        slot = lax.broadcasted_iota(I32, (tt, TOP_K), 1)
        word_ref[...] = _route_word(token, slot)

    count_ref[...] = count_ref[...] + tile_total


def _positions(ids, n_tiles_pad):
    t = ids.shape[0]
    tt = POS_TILE
    body = functools.partial(_positions_body, n_tiles_pad=n_tiles_pad)
    return pl.pallas_call(
        body,
        grid=(2, t // tt),
        in_specs=[pl.BlockSpec((tt, TOP_K), lambda p, i: (i, 0))],
        out_specs=[
            pl.BlockSpec((tt, TOP_K), lambda p, i: (i * p, 0)),
            pl.BlockSpec((tt, TOP_K), lambda p, i: (i * p, 0)),
            pl.BlockSpec((n_tiles_pad, LANES), lambda p, i: (0, 0)),
        ],
        out_shape=[
            jax.ShapeDtypeStruct((t, TOP_K), I32),
            jax.ShapeDtypeStruct((t, TOP_K), I32),
            jax.ShapeDtypeStruct((n_tiles_pad, LANES), I32),
        ],
        scratch_shapes=[pltpu.VMEM((8, LANES), F32), pltpu.VMEM((8, LANES), F32)],
        compiler_params=_cparams(2),
        name="moe_positions",
    )(ids)


def _inverse_body(pos_ref, word_ref, init_hbm, inv_hbm, inv_smem, sem):
    c = pl.program_id(0)

    @pl.when(c == 0)
    def _():
        load = pltpu.make_async_copy(init_hbm, inv_smem, sem)
        load.start()
        load.wait()

    def place(a, carry):
        inv_smem[pos_ref[a]] = word_ref[a]
        return carry

    lax.fori_loop(0, pos_ref.shape[0], place, 0, unroll=8)

    @pl.when(c == pl.num_programs(0) - 1)
    def _():
        store = pltpu.make_async_copy(inv_smem, inv_hbm, sem)
        store.start()
        store.wait()


def _inverse(pos_flat, word_flat, init):
    n = pos_flat.shape[0]
    chunk = INVERSE_CHUNK
    assert n % chunk == 0
    smem_chunk = pl.BlockSpec((chunk,), lambda c: (c,), memory_space=pltpu.SMEM)
    return pl.pallas_call(
        _inverse_body,
        grid=(n // chunk,),
        in_specs=[smem_chunk, smem_chunk, pl.BlockSpec(memory_space=pl.ANY)],
        out_specs=pl.BlockSpec(memory_space=pl.ANY),
        out_shape=jax.ShapeDtypeStruct(init.shape, I32),
        scratch_shapes=[pltpu.SMEM(init.shape, I32), pltpu.SemaphoreType.DMA],
        compiler_params=_cparams(1),
        name="moe_inverse",
    )(pos_flat, word_flat, init)


def _experts_body(te_ref, nv_ref, inv_next, inv_prev, w1_ref, b1_ref, w2_ref, b2_ref, tok_hbm,
                  out_hbm, xs_buf, y_buf, act_buf, w1b, w2b, gsem, ssem, *, n_tok, n_out):
    j = pl.program_id(0)
    nv = nv_ref[0]
    d_ff, d = w2b.shape
    half = d // 2
    xp, yp = half // LANES, d // LANES
    tm = xs_buf.shape[1] // xp
    slot = lax.rem(j, 2)
    other = 1 - slot
    token_mask = (1 << ROUTE_TOKEN_BITS) - 1

    def gather_row(word, r, buf_slot):
        tok = pl.multiple_of((word & token_mask) * xp, xp)
        return pltpu.make_async_copy(tok_hbm.at[pl.ds(tok, xp)],
                                     xs_buf.at[buf_slot, pl.ds(r * xp, xp)], gsem)

    def scatter_row(word, r, buf_slot):
        dst = pl.multiple_of(lax.shift_right_logical(word, ROUTE_TOKEN_BITS) * yp, yp)
        return pltpu.make_async_copy(y_buf.at[buf_slot, pl.ds(r * yp, yp)],
                                     out_hbm.at[pl.ds(dst, yp)], ssem)

    def wait_gather(buf_slot):
        pltpu.make_async_copy(tok_hbm.at[pl.ds(0, tm * xp)], xs_buf.at[buf_slot], gsem).wait()

    def wait_scatter(buf_slot):
        pltpu.make_async_copy(y_buf.at[buf_slot], out_hbm.at[pl.ds(0, tm * yp)], ssem).wait()

    @pl.when(j == 0)
    def _():
        y_buf[...] = jnp.zeros(y_buf.shape, F32)
        for r in range(tm):
            gather_row(inv_prev[0, 0, r], r, 0).start()

    @pl.when(j < nv)
    def _():
        wait_gather(slot)

        @pl.when((j == 0) | (te_ref[j] != te_ref[jnp.maximum(j - 1, 0)]))
        def _():
            for c in range(0, d, LANES):
                w1b[c:c + LANES, :] = w1_ref[0, c:c + LANES, :].astype(BF16)
            for c in range(0, d_ff, LANES):
                w2b[c:c + LANES, :] = w2_ref[0, c:c + LANES, :].astype(BF16)

        has_prev = j > 0

        def move_rows(r0, r1):
            for r in range(r0, r1):
                gather_row(inv_next[0, 0, r], r, other).start()
                spare = _wrap_i32((n_out + r) << ROUTE_TOKEN_BITS)
                scatter_row(jnp.where(has_prev, inv_prev[0, 0, r], spare), r, other).start()

        xs_now = xs_buf.at[slot]
        bits = jnp.concatenate(
            [xs_now[pl.ds(c, tm, stride=xp), :] for c in range(xp)], axis=-1)
        x_lo = pltpu.bitcast(bits << 16, F32).astype(BF16)
        x_hi = pltpu.bitcast(bits & jnp.uint32(0xFFFF0000), F32).astype(BF16)
        y_now = y_buf.at[slot]
        ch = EXPERT_CHUNK
        n_chunks = d_ff // ch + d // ch
        per_chunk = tm // n_chunks
        done = 0

        def proj1(c0):
            return (jnp.dot(x_lo, w1b[:half, c0:c0 + ch], preferred_element_type=F32)
                    + jnp.dot(x_hi, w1b[half:, c0:c0 + ch], preferred_element_type=F32)
                    + b1_ref[0, :, c0:c0 + ch])

        for c in range(0, d_ff, ch):
            gate = jnp.minimum(proj1(c), SWIGLU_LIMIT)
            lin = jnp.clip(proj1(d_ff + c), -SWIGLU_LIMIT, SWIGLU_LIMIT)
            act_buf[:, c:c + ch] = (gate * jax.nn.sigmoid(SWIGLU_ALPHA * gate)
                                    * (lin + 1.0)).astype(BF16)
            move_rows(done, done + per_chunk)
            done += per_chunk
        for c in range(0, d, ch):
            y_cols = (jnp.dot(act_buf[...], w2b[:, c:c + ch], preferred_element_type=F32)
                      + b2_ref[0, :, c:c + ch])
            for g in range(ch // LANES):
                y_now[pl.ds(c // LANES + g, tm, stride=yp), :] = (
                    y_cols[:, g * LANES:(g + 1) * LANES])
            move_rows(done, done + per_chunk)
            done += per_chunk
        move_rows(done, tm)
        wait_scatter(other)

    @pl.when(j == nv)
    def _():
        wait_gather(slot)
        for r in range(tm):
            scatter_row(inv_prev[0, 0, r], r, other).start()
        wait_scatter(other)


def _experts(tile_expert, n_valid, inv3, tokens, w1, b1, w2, b2, n_tok, n_out):
    nt, _, tm = inv3.shape
    d_ff, d = w2.shape[1:]
    half = d // 2
    xp, yp = half // LANES, d // LANES
    body = functools.partial(_experts_body, n_tok=n_tok, n_out=n_out)
    smem_tile = lambda index_map: pl.BlockSpec((1, 1, tm), index_map, memory_space=pltpu.SMEM)
    grid_spec = pltpu.PrefetchScalarGridSpec(
        num_scalar_prefetch=2,
        grid=(nt,),
        in_specs=[
            smem_tile(lambda j, te, nv: (jnp.minimum(j + 1, nt - 1), 0, 0)),
            smem_tile(lambda j, te, nv: (jnp.maximum(j - 1, 0), 0, 0)),
            pl.BlockSpec((1, d, 2 * d_ff), lambda j, te, nv: (te[j], 0, 0)),
            pl.BlockSpec((1, 1, 2 * d_ff), lambda j, te, nv: (te[j], 0, 0)),
            pl.BlockSpec((1, d_ff, d), lambda j, te, nv: (te[j], 0, 0)),
            pl.BlockSpec((1, 1, d), lambda j, te, nv: (te[j], 0, 0)),
            pl.BlockSpec(memory_space=pl.ANY),
        ],
        out_specs=pl.BlockSpec(memory_space=pl.ANY),
        scratch_shapes=[
            pltpu.VMEM((2, tm * xp, LANES), U32),
            pltpu.VMEM((2, tm * yp, LANES), F32),
            pltpu.VMEM((tm, d_ff), BF16),
            pltpu.VMEM((d, 2 * d_ff), BF16),
            pltpu.VMEM((d_ff, d), BF16),
            pltpu.SemaphoreType.DMA,
            pltpu.SemaphoreType.DMA,
        ],
    )
    return pl.pallas_call(
        body,
        grid_spec=grid_spec,
        out_shape=jax.ShapeDtypeStruct(((n_out + tm) * yp, LANES), F32),
        compiler_params=_cparams(1),
        name="moe_experts",
    )(tile_expert, n_valid, inv3, inv3, w1, b1, w2, b2, tokens)


def _combine_body(tw_ref, x1_ref, gf_ref, rows_ref, y_ref):
    tt, d = x1_ref.shape
    yp = d // LANES
    tw = tw_ref[...]
    cols = []
    for c in range(yp):
        acc = None
        for k in range(TOP_K):
            piece = rows_ref[pl.ds(k * tt * yp + c, tt, stride=yp), :]
            term = tw[:, k:k + 1] * piece
            acc = term if acc is None else acc + term
        cols.append(acc)
    y_ref[...] = _rms(x1_ref[...] + jnp.concatenate(cols, axis=-1), gf_ref[...])


def _combine(tw, x1, g_final, expert_rows, first_tile):
    t, d = x1.shape
    tt = COMBINE_TILE
    return pl.pallas_call(
        _combine_body,
        grid=(t // tt,),
        in_specs=[
            pl.BlockSpec((tt, TOP_K), lambda i: (i, 0)),
            pl.BlockSpec((tt, d), lambda i: (i, 0)),
            pl.BlockSpec((1, d), lambda i: (0, 0)),
            pl.BlockSpec((TOP_K * tt * (d // LANES), LANES), lambda i: (first_tile + i, 0)),
        ],
        out_specs=pl.BlockSpec((tt, d), lambda i: (i, 0)),
        out_shape=jax.ShapeDtypeStruct((t, d), F32),
        compiler_params=_cparams(1),
        name="moe_combine",
    )(tw, x1, g_final, expert_rows)


def kernel(x_prompt, x_sample, cache_k, cache_v, state_conv, page_table, g_mix, w_in, b_in,
           w_dw, b_dw, ln_g, ln_b, w_conv_out, lam_q1, lam_k1, lam_q2, lam_k2, subln_g,
           w_attn_out, w_o, g_ffn, w_router, b_router, w_moe1, b_moe1, w_moe2, b_moe2, g_final):
    bsz, seq, d = x_prompt.shape
    dec_b, dec_s, _ = x_sample.shape
    depth = g_mix.shape[0]
    c_conv = w_dw.shape[2]
    attn_w = N_HEADS * HEAD_W
    n_pages = page_table.shape[1]
    past_len = n_pages * PAGE_SIZE
    t_p, t_s = bsz * seq, dec_b * dec_s
    t_all = t_p + t_s
    d_ff = w_moe2.shape[2]
    assert depth == 1, "the combine kernel fuses the final norm, so only one layer is supported"
    assert seq % ROW_TILE == 0 and t_s % ROW_TILE == 0 and seq % CONV_ROWS == 0
    assert seq % ATTN_BLOCK == 0 and t_all % POS_TILE == 0 and dec_b % 8 == 0
    assert t_p % COMBINE_TILE == 0 and t_s % COMBINE_TILE == 0

    tab_p = _rope_tables(jnp.arange(seq))
    tab_s = _rope_tables(jnp.tile(past_len + jnp.arange(dec_s), dec_b))
    n_rows = t_all * TOP_K + N_EXPERTS * MOE_TILE
    n_tiles = n_rows // MOE_TILE
    row2 = lambda v: v.reshape(1, -1)

    hp, hs = x_prompt.reshape(t_p, d), x_sample.reshape(t_s, d)
    outs = [[] for _ in range(6)]
    for l in range(depth):
        lam_init = 0.8 - 0.6 * math.exp(-0.3 * l)
        lams = (row2(lam_q1[l]), row2(lam_k1[l]), row2(lam_q2[l]), row2(lam_k2[l]))
        subg = row2(subln_g[l])
        w_in_bf = w_in[l].astype(BF16)
        wc, wa, wo = (w_conv_out[l].astype(BF16), w_attn_out[l].astype(BF16),
                      w_o[l].astype(BF16))
        wr = w_router[l].astype(BF16)
        b1 = b_moe1[l].reshape(N_EXPERTS, 1, 2 * d_ff)
        b2 = b_moe2[l].reshape(N_EXPERTS, 1, d)
        proj_args = (row2(g_mix[l]), w_in_bf, row2(b_in[l]))
        post_args = (row2(ln_g[l]), row2(ln_b[l]), wc, wa, wo, row2(g_ffn[l]), wr,
                     row2(b_router[l]))

        a_p, q_p, k_p, v_p, gate_p = _in_proj(hp, *proj_args, tab_p, seq // ROW_TILE,
                                              c_conv, attn_w)
        a_p3 = a_p.reshape(bsz, seq, c_conv)
        conv_p = _conv_prompt(a_p3, w_dw[l], row2(b_dw[l]))
        o_p = _attn_prompt(q_p.reshape(bsz, seq, attn_w), k_p.reshape(bsz, seq, attn_w),
                           v_p.reshape(bsz, seq, attn_w), lams, subg, lam_init)
        x1_p, hpk_p, ids_p, tw_p = _post(hp, conv_p.reshape(t_p, c_conv),
                                         o_p.reshape(t_p, attn_w), gate_p, *post_args)

        a_s, q_s, k_s, v_s, gate_s = _in_proj(hs, *proj_args, tab_s, 1, c_conv, attn_w)
        a_s3 = a_s.reshape(dec_b, dec_s, c_conv)
        conv_s = _conv_decode(state_conv[l], a_s3, w_dw[l], row2(b_dw[l]))
        pool = cache_k.shape[1]
        o_s = _attn_decode(page_table, q_s.astype(F32).reshape(dec_b, dec_s, attn_w),
                           k_s.reshape(dec_b, dec_s, attn_w), v_s.reshape(dec_b, dec_s, attn_w),
                           cache_k[l].reshape(pool * PAGE_SIZE * N_HEADS, HEAD_W),
                           cache_v[l].reshape(pool * PAGE_SIZE * N_HEADS, HEAD_W),
                           lams, subg, lam_init)
        x1_s, hpk_s, ids_s, tw_s = _post(hs, conv_s.reshape(t_s, c_conv),
                                         o_s.reshape(t_s, attn_w).astype(BF16), gate_s,
                                         *post_args)

        ids = jnp.concatenate([ids_p, ids_s], axis=0)
        pos, word, te = _positions(ids, n_tiles)
        tile_expert = jnp.minimum(te[:, 0], N_EXPERTS - 1)
        n_valid = jnp.sum((te[:, 0] < N_EXPERTS).astype(I32)).reshape(1)
        slot_idx = jnp.arange(n_rows, dtype=U32)
        n_out = t_all * TOP_K
        unused = (((n_out + slot_idx % MOE_TILE) << ROUTE_TOKEN_BITS)
                  | (t_all + slot_idx % PAD_TOKENS))
        inv = _inverse(pos.reshape(-1), word.reshape(-1), lax.bitcast_convert_type(unused, I32))
        tokens = jnp.concatenate(
            [hpk_p, hpk_s, jnp.zeros((PAD_TOKENS * (d // 2 // LANES), LANES), U32)], axis=0)
        expert_rows = _experts(tile_expert, n_valid, inv.reshape(n_tiles, 1, MOE_TILE), tokens,
                               w_moe1[l], b1, w_moe2[l], b2, t_all, n_out)
        gf = row2(g_final)
        hp = _combine(tw_p, x1_p, gf, expert_rows, 0)
        hs = _combine(tw_s, x1_s, gf, expert_rows, t_p // COMBINE_TILE)

        outs[0].append(k_p.reshape(bsz, seq, N_HEADS, HEAD_W))
        outs[1].append(v_p.reshape(bsz, seq, N_HEADS, HEAD_W))
        outs[2].append(a_p3[:, seq - (CONV_WIDTH - 1):])
        outs[3].append(k_s.reshape(dec_b, dec_s, N_HEADS, HEAD_W))
        outs[4].append(v_s.reshape(dec_b, dec_s, N_HEADS, HEAD_W))
        outs[5].append(jnp.concatenate([state_conv[l], a_s3], axis=1)[:, -(CONV_WIDTH - 1):])

    y_prompt = hp.reshape(bsz, seq, d)
    y_sample = hs.reshape(dec_b, dec_s, d)
    return (y_prompt, y_sample) + tuple(jnp.stack(o) for o in outs)
```

```python
import functools
import math

import jax
import jax.numpy as jnp
from jax import lax
from jax.experimental import pallas as pl
from jax.experimental.pallas import tpu as pltpu

F32 = jnp.float32
BF16 = jnp.bfloat16
I32 = jnp.int32
U32 = jnp.uint32

N_HEADS = 8
HEAD_DIM = 64
HEAD_W = 2 * HEAD_DIM
ROT_DIM = HEAD_DIM // 4
ROT_HALF = ROT_DIM // 2
ROPE_THETA = 500000.0
CONV_WIDTH = 31
CONV_HALO = 32
N_EXPERTS = 32
TOP_K = 4
SWIGLU_ALPHA = 1.702
SWIGLU_LIMIT = 7.0
RMS_EPS = 1e-5
LN_EPS = 1e-5
PAGE_SIZE = 128
LANES = 128
NEG_BIG = -1e30

ROW_TILE = 512
ATTN_BLOCK = 512
ATTN_SUB = 128
CONV_ROWS = 1024
CONV_CHUNK = 64
MOE_TILE = 256
EXPERT_CHUNK = 256
POS_TILE = 512
COMBINE_TILE = 256
ROUTE_TOKEN_BITS = 15
INVERSE_CHUNK = 11 * 1024
PAD_TOKENS = 64
VMEM_LIMIT = 56 * 1024 * 1024


def _cparams(n_axes, vmem=VMEM_LIMIT):
    return pltpu.CompilerParams(dimension_semantics=("arbitrary",) * n_axes,
                                vmem_limit_bytes=vmem)


def _rms(x, g):
    return x * lax.rsqrt(jnp.mean(x * x, axis=-1, keepdims=True) + RMS_EPS) * g


def _columns(cols):
    rows = cols[0].shape[0]
    lane = lax.broadcasted_iota(I32, (rows, len(cols)), 1)
    out = jnp.zeros((rows, len(cols)), cols[0].dtype)
    for k, col in enumerate(cols):
        out = jnp.where(lane == k, col, out)
    return out


def _lam(lq1, lk1, lq2, lk2, lam_init):
    s1 = jnp.sum(lq1[...] * lk1[...], axis=-1, keepdims=True)
    s2 = jnp.sum(lq2[...] * lk2[...], axis=-1, keepdims=True)
    return jnp.exp(s1) - jnp.exp(s2) + lam_init


def _rope_tables(pos):
    inv = jnp.power(jnp.float32(ROPE_THETA),
                    -jnp.arange(ROT_HALF, dtype=F32) * (2.0 / ROT_DIM))
    ang = pos.astype(F32)[:, None] * inv[None, :]
    cos, sin = jnp.cos(ang), jnp.sin(ang)
    n = pos.shape[0]
    rest = HEAD_DIM - ROT_DIM
    zh = jnp.zeros((n, ROT_HALF), F32)
    c64 = jnp.concatenate([cos, cos, jnp.ones((n, rest), F32)], axis=-1)
    sa64 = jnp.concatenate([-sin, zh, jnp.zeros((n, rest), F32)], axis=-1)
    sb64 = jnp.concatenate([zh, sin, jnp.zeros((n, rest), F32)], axis=-1)
    tile = lambda t: jnp.concatenate([t, t], axis=-1)
    return tile(c64), tile(sa64), tile(sb64)


def _in_proj_body(x_ref, g_ref, w_ref, b_ref, cos_ref, sa_ref, sb_ref,
                  a_ref, q_ref, k_ref, v_ref, gate_ref, *, c_conv, attn_w, d_model):
    h = _rms(x_ref[...], g_ref[...]).astype(BF16)
    cos, sa, sb = cos_ref[...], sa_ref[...], sb_ref[...]
    ch = 512

    def proj(c0):
        return (jnp.dot(h, w_ref[:, c0:c0 + ch], preferred_element_type=F32)
                + b_ref[:, c0:c0 + ch])

    def rope(z):
        outs = []
        for j in range(ch // LANES):
            zj = z[:, j * LANES:(j + 1) * LANES]
            outs.append(zj * cos + pltpu.roll(zj, LANES - ROT_HALF, 1) * sa
                        + pltpu.roll(zj, ROT_HALF, 1) * sb)
        return jnp.concatenate(outs, axis=-1)

    for c in range(0, c_conv, ch):
        a_ref[:, c:c + ch] = proj(c) * jax.nn.sigmoid(proj(c_conv + c))
    base = 2 * c_conv
    for c in range(0, attn_w, ch):
        q_ref[:, c:c + ch] = (rope(proj(base + c)) * (HEAD_DIM ** -0.5)).astype(BF16)
    base += attn_w
    for c in range(0, attn_w, ch):
        k_ref[:, c:c + ch] = rope(proj(base + c))
    base += attn_w
    for c in range(0, attn_w, ch):
        v_ref[:, c:c + ch] = proj(base + c)
    base += attn_w
    for c in range(0, 2 * d_model, ch):
        gate_ref[:, c:c + ch] = jax.nn.sigmoid(proj(base + c))


def _in_proj(x, g_mix, w_in_bf, b_in, tables, n_tab_blocks, c_conv, attn_w):
    t, d = x.shape
    n_in = w_in_bf.shape[1]
    tm = ROW_TILE
    row = lambda i: (i, 0)
    const = lambda i: (0, 0)
    tab = lambda i: (i % n_tab_blocks, 0)
    body = functools.partial(_in_proj_body, c_conv=c_conv, attn_w=attn_w, d_model=d)
    return pl.pallas_call(
        body,
        grid=(t // tm,),
        in_specs=[
            pl.BlockSpec((tm, d), row),
            pl.BlockSpec((1, d), const),
            pl.BlockSpec((d, n_in), const, pipeline_mode=pl.Buffered(1)),
            pl.BlockSpec((1, n_in), const),
            pl.BlockSpec((tm, LANES), tab),
            pl.BlockSpec((tm, LANES), tab),
            pl.BlockSpec((tm, LANES), tab),
        ],
        out_specs=[
            pl.BlockSpec((tm, c_conv), row),
            pl.BlockSpec((tm, attn_w), row),
            pl.BlockSpec((tm, attn_w), row),
            pl.BlockSpec((tm, attn_w), row),
            pl.BlockSpec((tm, 2 * d), row),
        ],
        out_shape=[
            jax.ShapeDtypeStruct((t, c_conv), F32),
            jax.ShapeDtypeStruct((t, attn_w), BF16),
            jax.ShapeDtypeStruct((t, attn_w), F32),
            jax.ShapeDtypeStruct((t, attn_w), F32),
            jax.ShapeDtypeStruct((t, 2 * d), F32),
        ],
        compiler_params=_cparams(1),
        name="in_proj",
    )(x, g_mix, w_in_bf, b_in, *tables)


def _conv_prompt_body(a_ref, halo_ref, w_ref, b_ref, o_ref, ext_ref, *, rows):
    i = pl.program_id(2)
    keep = jnp.where(i > 0, 1.0, 0.0).astype(F32)
    ext_ref[0:CONV_HALO, :] = halo_ref[0] * keep
    ext_ref[CONV_HALO:, :] = a_ref[0]
    w = w_ref[...]
    bias = jnp.broadcast_to(b_ref[...], (CONV_CHUNK, LANES))
    off = CONV_HALO - (CONV_WIDTH - 1)
    for r0 in range(0, rows, CONV_CHUNK):
        acc = bias
        for j in range(CONV_WIDTH):
            acc = acc + w[j:j + 1, :] * ext_ref[r0 + off + j:r0 + off + j + CONV_CHUNK, :]
        o_ref[0, r0:r0 + CONV_CHUNK, :] = acc


def _conv_prompt(a3, w_dw, b_dw):
    bsz, seq, c = a3.shape
    rows = CONV_ROWS
    hb = rows // CONV_HALO
    body = functools.partial(_conv_prompt_body, rows=rows)
    return pl.pallas_call(
        body,
        grid=(bsz, c // LANES, seq // rows),
        in_specs=[
            pl.BlockSpec((1, rows, LANES), lambda b, g, i: (b, i, g)),
            pl.BlockSpec((1, CONV_HALO, LANES),
                         lambda b, g, i: (b, jnp.maximum(i * hb - 1, 0), g)),
            pl.BlockSpec((CONV_WIDTH, LANES), lambda b, g, i: (0, g)),
            pl.BlockSpec((1, LANES), lambda b, g, i: (0, g)),
        ],
        out_specs=pl.BlockSpec((1, rows, LANES), lambda b, g, i: (b, i, g)),
        out_shape=jax.ShapeDtypeStruct((bsz, seq, c), F32),
        scratch_shapes=[pltpu.VMEM((rows + CONV_HALO, LANES), F32)],
        compiler_params=_cparams(3),
        name="conv_prompt",
    )(a3, a3, w_dw, b_dw)


def _conv_decode_body(state_ref, a_ref, w_ref, b_ref, o_ref, ext_ref, *, n_state, n_new):
    ext_ref[:, 0:n_state, :] = state_ref[...]
    ext_ref[:, n_state:n_state + n_new, :] = a_ref[...]
    w = w_ref[...]
    for t in range(n_new):
        win = ext_ref[:, t:t + CONV_WIDTH, :]
        o_ref[:, t:t + 1, :] = (jnp.sum(win * w[None], axis=1, keepdims=True)
                                + b_ref[...][None])


def _conv_decode(state, a3, w_dw, b_dw):
    bsz, n_state, c = state.shape
    n_new = a3.shape[1]
    bb = 8
    body = functools.partial(_conv_decode_body, n_state=n_state, n_new=n_new)
    return pl.pallas_call(
        body,
        grid=(bsz // bb,),
        in_specs=[
            pl.BlockSpec((bb, n_state, c), lambda i: (i, 0, 0)),
            pl.BlockSpec((bb, n_new, c), lambda i: (i, 0, 0)),
            pl.BlockSpec((CONV_WIDTH, c), lambda i: (0, 0)),
            pl.BlockSpec((1, c), lambda i: (0, 0)),
        ],
        out_specs=pl.BlockSpec((bb, n_new, c), lambda i: (i, 0, 0)),
        out_shape=jax.ShapeDtypeStruct((bsz, n_new, c), F32),
        scratch_shapes=[pltpu.VMEM((bb, n_state + n_new + 6, c), F32)],
        compiler_params=_cparams(1),
        name="conv_decode",
    )(state, a3, w_dw, b_dw)


def _attn_prompt_body(lq1, lk1, lq2, lk2, subg_ref, q_ref, k_ref, v_ref, o_ref,
                      kt_ref, vb_ref, acc1, acc2, m1, l1, m2, l2, *, blk, sub, n_blk, lam_init):
    i = pl.program_id(2)

    @pl.when(i == 0)
    def _():
        for c in range(n_blk):
            kt_ref[c] = k_ref[0, c * blk:(c + 1) * blk, :].T.astype(BF16)
            vb_ref[c] = v_ref[0, c * blk:(c + 1) * blk, :].astype(BF16)

    for m_ref, l_ref, acc_ref in ((m1, l1, acc1), (m2, l2, acc2)):
        m_ref[...] = jnp.full(m_ref.shape, NEG_BIG, F32)
        l_ref[...] = jnp.zeros(l_ref.shape, F32)
        acc_ref[...] = jnp.zeros(acc_ref.shape, F32)

    stats = ((m1, l1, acc1), (m2, l2, acc2))
    n_sub = blk // sub

    def scores(kb, sb, masked):
        r0 = sb * sub
        ncol = r0 + sub if masked else blk
        out = []
        for c in range(2):
            qq = q_ref[0, r0:r0 + sub, c * HEAD_DIM:(c + 1) * HEAD_DIM]
            kk = kt_ref[kb, c * HEAD_DIM:(c + 1) * HEAD_DIM, 0:ncol]
            out.append(jnp.dot(qq, kk, preferred_element_type=F32))
        return out

    def softmax_values(kb, sb, masked, s_pair):
        r0 = sb * sub
        rows = slice(r0, r0 + sub)
        ncol = r0 + sub if masked else blk
        ps, alphas = [], []
        for s, (m_ref, l_ref, _) in zip(s_pair, stats):
            if masked:
                row = lax.broadcasted_iota(I32, (sub, ncol), 0) + r0
                col = lax.broadcasted_iota(I32, (sub, ncol), 1)
                s = jnp.where(col <= row, s, NEG_BIG)
            m_old = m_ref[rows, :]
            m_new = jnp.maximum(m_old, jnp.max(s, axis=-1, keepdims=True))
            alpha = jnp.exp(m_old - m_new)
            p = jnp.exp(s - jnp.concatenate([m_new] * (ncol // LANES), axis=1))
            l_ref[rows, :] = alpha * l_ref[rows, :] + jnp.sum(p, axis=-1, keepdims=True)
            m_ref[rows, :] = m_new
            ps.append(p.astype(BF16))
            alphas.append(alpha)
        pv = jnp.dot(jnp.concatenate(ps, axis=0), vb_ref[kb, 0:ncol, :],
                     preferred_element_type=F32)
        for c, (alpha, (_, _, acc_ref)) in enumerate(zip(alphas, stats)):
            acc_ref[rows, :] = alpha * acc_ref[rows, :] + pv[c * sub:(c + 1) * sub]

    def run(chains):
        ahead = 2
        pending = {n: scores(*chains[n]) for n in range(min(ahead, len(chains)))}
        for n, chain in enumerate(chains):
            if n + ahead < len(chains):
                pending[n + ahead] = scores(*chains[n + ahead])
            softmax_values(*chain, pending.pop(n))

    def block(kb, masked):
        return [(kb, sb, masked) for sb in range(n_sub)]

    def pair_body(t, carry):
        run(block(2 * t, False) + block(2 * t + 1, False))
        return carry

    lax.fori_loop(0, lax.shift_right_logical(i, 1), pair_body, 0)
    odd = (i & 1) == 1

    @pl.when(odd)
    def _():
        run(block(i - 1, False) + block(i, True))

    @pl.when(jnp.logical_not(odd))
    def _():
        run(block(i, True))

    lam = _lam(lq1, lk1, lq2, lk2, lam_init)
    o = acc1[...] / l1[...] - lam * (acc2[...] / l2[...])
    o_ref[0] = (_rms(o, subg_ref[...]) * (1.0 - lam_init)).astype(BF16)


def _attn_prompt(q3, k3, v3, lams, subln_g, lam_init):
    bsz, seq, _ = q3.shape
    blk = ATTN_BLOCK
    n_blk = seq // blk
    vec = pl.BlockSpec((1, HEAD_DIM), lambda b, h, i: (0, 0))
    body = functools.partial(_attn_prompt_body, blk=blk, sub=ATTN_SUB, n_blk=n_blk,
                             lam_init=lam_init)
    return pl.pallas_call(
        body,
        grid=(bsz, N_HEADS, n_blk),
        in_specs=[
            vec, vec, vec, vec,
            pl.BlockSpec((1, HEAD_W), lambda b, h, i: (0, 0)),
            pl.BlockSpec((1, blk, HEAD_W), lambda b, h, i: (b, i, h)),
            pl.BlockSpec((1, seq, HEAD_W), lambda b, h, i: (b, 0, h)),
            pl.BlockSpec((1, seq, HEAD_W), lambda b, h, i: (b, 0, h)),
        ],
        out_specs=pl.BlockSpec((1, blk, HEAD_W), lambda b, h, i: (b, i, h)),
        out_shape=jax.ShapeDtypeStruct(q3.shape, BF16),
        scratch_shapes=[
            pltpu.VMEM((n_blk, HEAD_W, blk), BF16),
            pltpu.VMEM((n_blk, blk, HEAD_W), BF16),
            pltpu.VMEM((blk, HEAD_W), F32),
            pltpu.VMEM((blk, HEAD_W), F32),
            pltpu.VMEM((blk, LANES), F32),
            pltpu.VMEM((blk, LANES), F32),
            pltpu.VMEM((blk, LANES), F32),
            pltpu.VMEM((blk, LANES), F32),
        ],
        compiler_params=_cparams(3),
        name="attn_prompt",
    )(*lams, subln_g, q3, k3, v3)


NEW_PAD = 16
Q_PAD = 8
DECODE_PAGE_GROUP = 4


def _attn_decode_body(pt_ref, lq1, lk1, lq2, lk2, subg_ref, q_ref, kn_ref, vn_ref, *rest,
                      n_pages, n_new, lam_init):
    del pt_ref
    kpages = rest[:n_pages]
    vpages = rest[n_pages:2 * n_pages]
    o_ref, s_ref, stage = rest[2 * n_pages:]
    n_past = n_pages * PAGE_SIZE
    width = stage.shape[1]

    def padded_rows(rows_f32):
        stage[...] = jnp.zeros(stage.shape, F32)
        stage[0:n_new, :] = rows_f32
        return stage[...].astype(BF16)

    q16 = padded_rows(q_ref[0])
    sel_r = lax.broadcasted_iota(I32, (NEW_PAD, LANES), 0)
    sel_c = lax.broadcasted_iota(I32, (NEW_PAD, LANES), 1)
    sel = (sel_c % Q_PAD == sel_r).astype(BF16)
    qrep = lax.dot_general(q16, sel, (((0,), (0,)), ((), ())),
                           preferred_element_type=F32)
    rr = lax.broadcasted_iota(I32, (width, LANES), 0)
    cc = lax.broadcasted_iota(I32, (width, LANES), 1)
    qblk = jnp.where(rr // HEAD_DIM == cc // Q_PAD, qrep, 0.0).astype(BF16)

    def head_major(page_ref):
        return jnp.concatenate(
            [page_ref[pl.ds(h, PAGE_SIZE, stride=N_HEADS), :].astype(BF16)
             for h in range(N_HEADS)], axis=-1)

    groups = [range(j0, min(j0 + DECODE_PAGE_GROUP, n_pages))
              for j0 in range(0, n_pages, DECODE_PAGE_GROUP)]
    row_slices = [slice(g[0] * PAGE_SIZE, (g[-1] + 1) * PAGE_SIZE) for g in groups]

    for g, rows in zip(groups, row_slices):
        keys = jnp.concatenate([head_major(kpages[j]) for j in g], axis=0)
        s_ref[rows, :] = jnp.dot(keys, qblk, preferred_element_type=F32)
    s_new = jnp.dot(padded_rows(kn_ref[0]), qblk, preferred_element_type=F32)
    new_idx = lax.broadcasted_iota(I32, (NEW_PAD, LANES), 0)
    slot = lax.broadcasted_iota(I32, (NEW_PAD, LANES), 1) % Q_PAD
    s_new = jnp.where((new_idx <= slot) & (new_idx < n_new), s_new, NEG_BIG)
    m = jnp.maximum(jnp.max(s_ref[0:n_past, :], axis=0, keepdims=True),
                    jnp.max(s_new, axis=0, keepdims=True))

    contract0 = (((0,), (0,)), ((), ()))
    e_new = jnp.exp(s_new - m)
    denom = jnp.sum(e_new, axis=0, keepdims=True)
    acc = lax.dot_general(e_new.astype(BF16), padded_rows(vn_ref[0]), contract0,
                          preferred_element_type=F32)
    for g, rows in zip(groups, row_slices):
        e = jnp.exp(s_ref[rows, :] - m)
        denom = denom + jnp.sum(e, axis=0, keepdims=True)
        vals = jnp.concatenate([head_major(vpages[j]) for j in g], axis=0)
        acc = acc + lax.dot_general(e.astype(BF16), vals, contract0,
                                    preferred_element_type=F32)

    r_i = lax.broadcasted_iota(I32, (LANES, LANES), 0)
    c_i = lax.broadcasted_iota(I32, (LANES, LANES), 1)
    denom_rows = jnp.sum(jnp.where(r_i == c_i, jnp.broadcast_to(denom, (LANES, LANES)), 0.0),
                         axis=1, keepdims=True)
    o_norm = acc / denom_rows
    lam = _lam(lq1, lk1, lq2, lk2, lam_init)
    outs = []
    for h in range(N_HEADS):
        r0 = h * 2 * Q_PAD
        cols = slice(h * HEAD_W, (h + 1) * HEAD_W)
        oh = o_norm[r0:r0 + Q_PAD, cols] - lam * o_norm[r0 + Q_PAD:r0 + 2 * Q_PAD, cols]
        outs.append(_rms(oh, subg_ref[...]) * (1.0 - lam_init))
    o_ref[0] = jnp.concatenate(outs, axis=-1)[:n_new]


def _attn_decode(page_table, q3, kn3, vn3, cache_k, cache_v, lams, subln_g, lam_init):
    bsz, n_new, width = q3.shape
    n_pages = page_table.shape[1]
    vec = pl.BlockSpec((1, HEAD_DIM), lambda b, pt: (0, 0))
    per_b = pl.BlockSpec((1, n_new, width), lambda b, pt: (b, 0, 0))

    def page_spec(j):
        return pl.BlockSpec((PAGE_SIZE * N_HEADS, HEAD_W), lambda b, pt: (pt[b, j], 0))

    body = functools.partial(_attn_decode_body, n_pages=n_pages, n_new=n_new,
                             lam_init=lam_init)
    grid_spec = pltpu.PrefetchScalarGridSpec(
        num_scalar_prefetch=1,
        grid=(bsz,),
        in_specs=([vec, vec, vec, vec, pl.BlockSpec((1, HEAD_W), lambda b, pt: (0, 0)),
                   per_b, per_b, per_b]
                  + [page_spec(j) for j in range(n_pages)]
                  + [page_spec(j) for j in range(n_pages)]),
        out_specs=per_b,
        scratch_shapes=[pltpu.VMEM((n_pages * PAGE_SIZE, LANES), F32),
                        pltpu.VMEM((NEW_PAD, width), F32)],
    )
    return pl.pallas_call(
        body,
        grid_spec=grid_spec,
        out_shape=jax.ShapeDtypeStruct(q3.shape, F32),
        compiler_params=_cparams(1),
        name="attn_decode",
    )(page_table, *lams, subln_g, q3, kn3, vn3,
      *([cache_k] * n_pages), *([cache_v] * n_pages))


def _post_body(x_ref, conv_ref, o_ref, gate_ref, lng, lnb, wc, wa, wo, gffn, wr, br,
               x1_ref, hp_ref, ids_ref, tw_ref, *, d_model):
    n_groups = 2
    rows_per = x_ref.shape[0] // n_groups
    groups = [slice(g * rows_per, (g + 1) * rows_per) for g in range(n_groups)]
    attn_outs = [jnp.dot(o_ref[r, :], wa[...], preferred_element_type=F32) for r in groups]
    conv_outs = []
    for r in groups:
        c = conv_ref[r, :]
        mu = jnp.mean(c, axis=-1, keepdims=True)
        xc = c - mu
        cn = (xc * lax.rsqrt(jnp.mean(xc * xc, axis=-1, keepdims=True) + LN_EPS) * lng[...]
              + lnb[...])
        cact = (cn * jax.nn.sigmoid(cn)).astype(BF16)
        conv_outs.append(jnp.dot(cact, wc[...], preferred_element_type=F32))
    hbs = []
    for r, conv_out, attn_out in zip(groups, conv_outs, attn_outs):
        merged = (gate_ref[r, :d_model] * conv_out
                  + gate_ref[r, d_model:] * attn_out).astype(BF16)
        x1 = x_ref[r, :] + jnp.dot(merged, wo[...], preferred_element_type=F32)
        x1_ref[r, :] = x1
        hbs.append(_rms(x1, gffn[...]).astype(BF16))
    for r, hb in zip(groups, hbs):
        logits = jnp.dot(hb, wr[...], preferred_element_type=F32) + br[...]
        lane = lax.broadcasted_iota(I32, logits.shape, 1)
        vals, ids = [], []
        cur = logits
        for _ in range(TOP_K):
            mx = jnp.max(cur, axis=-1, keepdims=True)
            idx = jnp.min(jnp.where(cur == mx, lane, N_EXPERTS), axis=-1, keepdims=True)
            vals.append(mx)
            ids.append(idx)
            cur = jnp.where(lane == idx, -jnp.inf, cur)
        es = [jnp.exp(v - vals[0]) for v in vals]
        den = es[0] + es[1] + es[2] + es[3]
        tw_ref[r, :] = _columns([e / den for e in es])
        ids_ref[r, :] = _columns(ids)

        bits = pltpu.bitcast(hb.astype(F32), U32)
        half = d_model // 2
        words = (bits[:, :half] >> 16) | (bits[:, half:] & jnp.uint32(0xFFFF0000))
        pieces = half // LANES
        for c in range(pieces):
            hp_ref[pl.ds(r.start * pieces + c, rows_per, stride=pieces), :] = (
                words[:, c * LANES:(c + 1) * LANES])


def _post(x, conv, o, gate, ln_g, ln_b, wc, wa, wo, g_ffn, wr, br):
    t, d = x.shape
    tm = ROW_TILE
    row = lambda i: (i, 0)
    const = lambda i: (0, 0)
    mat = pl.BlockSpec((d, d), const)
    vec = pl.BlockSpec((1, d), const)
    body = functools.partial(_post_body, d_model=d)
    return pl.pallas_call(
        body,
        grid=(t // tm,),
        in_specs=[
            pl.BlockSpec((tm, d), row), pl.BlockSpec((tm, d), row), pl.BlockSpec((tm, d), row),
            pl.BlockSpec((tm, 2 * d), row),
            vec, vec, mat, mat, mat, vec,
            pl.BlockSpec((d, N_EXPERTS), const), pl.BlockSpec((1, N_EXPERTS), const),
        ],
        out_specs=[
            pl.BlockSpec((tm, d), row), pl.BlockSpec((tm * (d // 2 // LANES), LANES), row),
            pl.BlockSpec((tm, TOP_K), row), pl.BlockSpec((tm, TOP_K), row),
        ],
        out_shape=[
            jax.ShapeDtypeStruct((t, d), F32),
            jax.ShapeDtypeStruct((t * (d // 2 // LANES), LANES), U32),
            jax.ShapeDtypeStruct((t, TOP_K), I32), jax.ShapeDtypeStruct((t, TOP_K), F32),
        ],
        compiler_params=_cparams(1),
        name="post",
    )(x, conv, o, gate, ln_g, ln_b, wc, wa, wo, g_ffn, wr, br)


def _lane_cumsum(x):
    lane = lax.broadcasted_iota(I32, x.shape, 1)
    s = 1
    while s < LANES:
        x = x + jnp.where(lane >= s, pltpu.roll(x, s, 1), 0.0)
        s *= 2
    return x


def _wrap_i32(value):
    return (value + 2 ** 31) % 2 ** 32 - 2 ** 31


def _route_word(token, slot):
    dst = ((token // COMBINE_TILE) * (TOP_K * COMBINE_TILE) + slot * COMBINE_TILE
           + token % COMBINE_TILE)
    return jnp.left_shift(dst, ROUTE_TOKEN_BITS) | token


def _positions_body(ids_ref, pos_ref, word_ref, te_ref, count_ref, start_ref, *, n_tiles_pad):
    p = pl.program_id(0)
    i = pl.program_id(1)
    ids = ids_ref[...]
    tt = ids.shape[0]
    lane = lax.broadcasted_iota(I32, (tt, LANES), 1)
    onehots = [ids[:, k:k + 1] == lane for k in range(TOP_K)]
    tile_counts = [jnp.sum(oh.astype(F32), axis=0, keepdims=True) for oh in onehots]
    tile_total = tile_counts[0] + tile_counts[1] + tile_counts[2] + tile_counts[3]

    @pl.when((p == 0) & (i == 0))
    def _():
        count_ref[...] = jnp.zeros(count_ref.shape, F32)

    @pl.when((p == 1) & (i == 0))
    def _():
        counts = count_ref[...]
        padded = jnp.ceil(counts * (1.0 / MOE_TILE)) * MOE_TILE
        ends = _lane_cumsum(padded)
        start_ref[...] = ends - padded
        count_ref[...] = jnp.zeros(count_ref.shape, F32)
        tile_start = (lax.broadcasted_iota(I32, (n_tiles_pad, LANES), 0) * MOE_TILE).astype(F32)
        elane = lax.broadcasted_iota(I32, (n_tiles_pad, LANES), 1)
        done = (ends[0:1, :] <= tile_start) & (elane < N_EXPERTS)
        n_done = jnp.sum(done.astype(F32), axis=-1, keepdims=True)
        te_ref[...] = jnp.broadcast_to(n_done, (n_tiles_pad, LANES)).astype(I32)

    @pl.when(p == 1)
    def _():
        r = lax.broadcasted_iota(I32, (tt, tt), 0)
        c = lax.broadcasted_iota(I32, (tt, tt), 1)
        earlier = (c < r).astype(BF16)
        run = start_ref[0:1, :] + count_ref[0:1, :]
        cols = []
        for k in range(TOP_K):
            within = jnp.dot(earlier, onehots[k].astype(BF16), preferred_element_type=F32)
            cols.append(jnp.sum(jnp.where(onehots[k], within + run, 0.0),
                                axis=-1, keepdims=True))
            run = run + tile_counts[k]
        pos_ref[...] = _columns(cols).astype(I32)
        token = lax.broadcasted_iota(I32, (tt, TOP_K), 0) + i * tt
        slot = lax.broadcasted_iota(I32, (tt, TOP_K), 1)
        word_ref[...] = _route_word(token, slot)

    count_ref[...] = count_ref[...] + tile_total


def _positions(ids, n_tiles_pad):
    t = ids.shape[0]
    tt = POS_TILE
    body = functools.partial(_positions_body, n_tiles_pad=n_tiles_pad)
    return pl.pallas_call(
        body,
        grid=(2, t // tt),
        in_specs=[pl.BlockSpec((tt, TOP_K), lambda p, i: (i, 0))],
        out_specs=[
            pl.BlockSpec((tt, TOP_K), lambda p, i: (i * p, 0)),
            pl.BlockSpec((tt, TOP_K), lambda p, i: (i * p, 0)),
            pl.BlockSpec((n_tiles_pad, LANES), lambda p, i: (0, 0)),
        ],
        out_shape=[
            jax.ShapeDtypeStruct((t, TOP_K), I32),
            jax.ShapeDtypeStruct((t, TOP_K), I32),
            jax.ShapeDtypeStruct((n_tiles_pad, LANES), I32),
        ],
        scratch_shapes=[pltpu.VMEM((8, LANES), F32), pltpu.VMEM((8, LANES), F32)],
        compiler_params=_cparams(2),
        name="moe_positions",
    )(ids)


def _inverse_body(pos_ref, word_ref, init_hbm, inv_hbm, inv_smem, sem):
    c = pl.program_id(0)

    @pl.when(c == 0)
    def _():
        load = pltpu.make_async_copy(init_hbm, inv_smem, sem)
        load.start()
        load.wait()

    def place(a, carry):
        inv_smem[pos_ref[a]] = word_ref[a]
        return carry

    lax.fori_loop(0, pos_ref.shape[0], place, 0, unroll=8)

    @pl.when(c == pl.num_programs(0) - 1)
    def _():
        store = pltpu.make_async_copy(inv_smem, inv_hbm, sem)
        store.start()
        store.wait()


def _inverse(pos_flat, word_flat, init):
    n = pos_flat.shape[0]
    chunk = INVERSE_CHUNK
    assert n % chunk == 0
    smem_chunk = pl.BlockSpec((chunk,), lambda c: (c,), memory_space=pltpu.SMEM)
    return pl.pallas_call(
        _inverse_body,
        grid=(n // chunk,),
        in_specs=[smem_chunk, smem_chunk, pl.BlockSpec(memory_space=pl.ANY)],
        out_specs=pl.BlockSpec(memory_space=pl.ANY),
        out_shape=jax.ShapeDtypeStruct(init.shape, I32),
        scratch_shapes=[pltpu.SMEM(init.shape, I32), pltpu.SemaphoreType.DMA],
        compiler_params=_cparams(1),
        name="moe_inverse",
    )(pos_flat, word_flat, init)


def _experts_body(te_ref, nv_ref, inv_next, inv_prev, w1_ref, b1_ref, w2_ref, b2_ref, tok_hbm,
                  out_hbm, xs_buf, y_buf, act_buf, w1b, w2b, gsem, ssem, *, n_tok, n_out):
    j = pl.program_id(0)
    nv = nv_ref[0]
    d_ff, d = w2b.shape
    half = d // 2
    xp, yp = half // LANES, d // LANES
    tm = xs_buf.shape[1] // xp
    slot = lax.rem(j, 2)
    other = 1 - slot
    token_mask = (1 << ROUTE_TOKEN_BITS) - 1

    def gather_row(word, r, buf_slot):
        tok = pl.multiple_of((word & token_mask) * xp, xp)
        return pltpu.make_async_copy(tok_hbm.at[pl.ds(tok, xp)],
                                     xs_buf.at[buf_slot, pl.ds(r * xp, xp)], gsem)

    def scatter_row(word, r, buf_slot):
        dst = pl.multiple_of(lax.shift_right_logical(word, ROUTE_TOKEN_BITS) * yp, yp)
        return pltpu.make_async_copy(y_buf.at[buf_slot, pl.ds(r * yp, yp)],
                                     out_hbm.at[pl.ds(dst, yp)], ssem.at[buf_slot])

    def spare_word(r):
        return _wrap_i32((n_out + r) << ROUTE_TOKEN_BITS)

    def wait_gather(buf_slot):
        pltpu.make_async_copy(tok_hbm.at[pl.ds(0, tm * xp)], xs_buf.at[buf_slot], gsem).wait()

    def wait_scatter(buf_slot):
        pltpu.make_async_copy(y_buf.at[buf_slot], out_hbm.at[pl.ds(0, tm * yp)],
                              ssem.at[buf_slot]).wait()

    @pl.when(j == 0)
    def _():
        y_buf[...] = jnp.zeros(y_buf.shape, F32)
        for r in range(tm):
            gather_row(inv_prev[0, 0, r], r, 0).start()
            scatter_row(spare_word(r), r, 0).start()

    @pl.when(j < nv)
    def _():
        wait_gather(slot)

        @pl.when((j == 0) | (te_ref[j] != te_ref[jnp.maximum(j - 1, 0)]))
        def _():
            for c in range(0, d, LANES):
                w1b[c:c + LANES, :] = w1_ref[0, c:c + LANES, :].astype(BF16)
            for c in range(0, d_ff, LANES):
                w2b[c:c + LANES, :] = w2_ref[0, c:c + LANES, :].astype(BF16)

        has_prev = j > 0

        def gather_rows(r0, r1):
            for r in range(r0, r1):
                gather_row(inv_next[0, 0, r], r, other).start()

        def scatter_rows(r0, r1):
            for r in range(r0, r1):
                word = jnp.where(has_prev, inv_prev[0, 0, r], spare_word(r))
                scatter_row(word, r, other).start()

        xs_now = xs_buf.at[slot]
        bits = jnp.concatenate(
            [xs_now[pl.ds(c, tm, stride=xp), :] for c in range(xp)], axis=-1)
        x_lo = pltpu.bitcast(bits << 16, F32).astype(BF16)
        x_hi = pltpu.bitcast(bits & jnp.uint32(0xFFFF0000), F32).astype(BF16)
        y_now = y_buf.at[slot]
        ch = EXPERT_CHUNK

        def proj1(c0):
            return (jnp.dot(x_lo, w1b[:half, c0:c0 + ch], preferred_element_type=F32)
                    + jnp.dot(x_hi, w1b[half:, c0:c0 + ch], preferred_element_type=F32)
                    + b1_ref[0, :, c0:c0 + ch])

        rows_1 = tm // (d_ff // ch)
        for n, c in enumerate(range(0, d_ff, ch)):
            gate = jnp.minimum(proj1(c), SWIGLU_LIMIT)
            lin = jnp.clip(proj1(d_ff + c), -SWIGLU_LIMIT, SWIGLU_LIMIT)
            act_buf[:, c:c + ch] = (gate * jax.nn.sigmoid(SWIGLU_ALPHA * gate)
                                    * (lin + 1.0)).astype(BF16)
            gather_rows(n * rows_1, (n + 1) * rows_1)
        wait_scatter(slot)
        rows_2 = tm // (d // ch)
        for n, c in enumerate(range(0, d, ch)):
            y_cols = (jnp.dot(act_buf[...], w2b[:, c:c + ch], preferred_element_type=F32)
                      + b2_ref[0, :, c:c + ch])
            for g in range(ch // LANES):
                y_now[pl.ds(c // LANES + g, tm, stride=yp), :] = (
                    y_cols[:, g * LANES:(g + 1) * LANES])
            scatter_rows(n * rows_2, (n + 1) * rows_2)

    @pl.when(j == nv)
    def _():
        wait_gather(slot)
        wait_scatter(slot)
        for r in range(tm):
            scatter_row(inv_prev[0, 0, r], r, other).start()
        wait_scatter(other)


def _experts(tile_expert, n_valid, inv3, tokens, w1, b1, w2, b2, n_tok, n_out):
    nt, _, tm = inv3.shape
    d_ff, d = w2.shape[1:]
    half = d // 2
    xp, yp = half // LANES, d // LANES
    body = functools.partial(_experts_body, n_tok=n_tok, n_out=n_out)
    smem_tile = lambda index_map: pl.BlockSpec((1, 1, tm), index_map, memory_space=pltpu.SMEM)
    grid_spec = pltpu.PrefetchScalarGridSpec(
        num_scalar_prefetch=2,
        grid=(nt,),
        in_specs=[
            smem_tile(lambda j, te, nv: (jnp.minimum(j + 1, nt - 1), 0, 0)),
            smem_tile(lambda j, te, nv: (jnp.maximum(j - 1, 0), 0, 0)),
            pl.BlockSpec((1, d, 2 * d_ff), lambda j, te, nv: (te[j], 0, 0)),
            pl.BlockSpec((1, 1, 2 * d_ff), lambda j, te, nv: (te[j], 0, 0)),
            pl.BlockSpec((1, d_ff, d), lambda j, te, nv: (te[j], 0, 0)),
            pl.BlockSpec((1, 1, d), lambda j, te, nv: (te[j], 0, 0)),
            pl.BlockSpec(memory_space=pl.ANY),
        ],
        out_specs=pl.BlockSpec(memory_space=pl.ANY),
        scratch_shapes=[
            pltpu.VMEM((2, tm * xp, LANES), U32),
            pltpu.VMEM((2, tm * yp, LANES), F32),
            pltpu.VMEM((tm, d_ff), BF16),
            pltpu.VMEM((d, 2 * d_ff), BF16),
            pltpu.VMEM((d_ff, d), BF16),
            pltpu.SemaphoreType.DMA,
            pltpu.SemaphoreType.DMA((2,)),
        ],
    )
    return pl.pallas_call(
        body,
        grid_spec=grid_spec,
        out_shape=jax.ShapeDtypeStruct(((n_out + tm) * yp, LANES), F32),
        compiler_params=_cparams(1),
        name="moe_experts",
    )(tile_expert, n_valid, inv3, inv3, w1, b1, w2, b2, tokens)


def _combine_body(tw_ref, x1_ref, gf_ref, rows_ref, y_ref):
    tt, d = x1_ref.shape
    yp = d // LANES
    tw = tw_ref[...]
    cols = []
    for c in range(yp):
        acc = None
        for k in range(TOP_K):
            piece = rows_ref[pl.ds(k * tt * yp + c, tt, stride=yp), :]
            term = tw[:, k:k + 1] * piece
            acc = term if acc is None else acc + term
        cols.append(acc)
    y_ref[...] = _rms(x1_ref[...] + jnp.concatenate(cols, axis=-1), gf_ref[...])


def _combine(tw, x1, g_final, expert_rows, first_tile):
    t, d = x1.shape
    tt = COMBINE_TILE
    return pl.pallas_call(
        _combine_body,
        grid=(t // tt,),
        in_specs=[
            pl.BlockSpec((tt, TOP_K), lambda i: (i, 0)),
            pl.BlockSpec((tt, d), lambda i: (i, 0)),
            pl.BlockSpec((1, d), lambda i: (0, 0)),
            pl.BlockSpec((TOP_K * tt * (d // LANES), LANES), lambda i: (first_tile + i, 0)),
        ],
        out_specs=pl.BlockSpec((tt, d), lambda i: (i, 0)),
        out_shape=jax.ShapeDtypeStruct((t, d), F32),
        compiler_params=_cparams(1),
        name="moe_combine",
    )(tw, x1, g_final, expert_rows)


def kernel(x_prompt, x_sample, cache_k, cache_v, state_conv, page_table, g_mix, w_in, b_in,
           w_dw, b_dw, ln_g, ln_b, w_conv_out, lam_q1, lam_k1, lam_q2, lam_k2, subln_g,
           w_attn_out, w_o, g_ffn, w_router, b_router, w_moe1, b_moe1, w_moe2, b_moe2, g_final):
    bsz, seq, d = x_prompt.shape
    dec_b, dec_s, _ = x_sample.shape
    depth = g_mix.shape[0]
    c_conv = w_dw.shape[2]
    attn_w = N_HEADS * HEAD_W
    n_pages = page_table.shape[1]
    past_len = n_pages * PAGE_SIZE
    t_p, t_s = bsz * seq, dec_b * dec_s
    t_all = t_p + t_s
    d_ff = w_moe2.shape[2]
    assert depth == 1, "the combine kernel fuses the final norm, so only one layer is supported"
    assert seq % ROW_TILE == 0 and t_s % ROW_TILE == 0 and seq % CONV_ROWS == 0
    assert seq % ATTN_BLOCK == 0 and t_all % POS_TILE == 0 and dec_b % 8 == 0
    assert t_p % COMBINE_TILE == 0 and t_s % COMBINE_TILE == 0

    tab_p = _rope_tables(jnp.arange(seq))
    tab_s = _rope_tables(jnp.tile(past_len + jnp.arange(dec_s), dec_b))
    n_rows = t_all * TOP_K + N_EXPERTS * MOE_TILE
    n_tiles = n_rows // MOE_TILE
    row2 = lambda v: v.reshape(1, -1)

    hp, hs = x_prompt.reshape(t_p, d), x_sample.reshape(t_s, d)
    outs = [[] for _ in range(6)]
    for l in range(depth):
        lam_init = 0.8 - 0.6 * math.exp(-0.3 * l)
        lams = (row2(lam_q1[l]), row2(lam_k1[l]), row2(lam_q2[l]), row2(lam_k2[l]))
        subg = row2(subln_g[l])
        w_in_bf = w_in[l].astype(BF16)
        wc, wa, wo = (w_conv_out[l].astype(BF16), w_attn_out[l].astype(BF16),
                      w_o[l].astype(BF16))
        wr = w_router[l].astype(BF16)
        b1 = b_moe1[l].reshape(N_EXPERTS, 1, 2 * d_ff)
        b2 = b_moe2[l].reshape(N_EXPERTS, 1, d)
        proj_args = (row2(g_mix[l]), w_in_bf, row2(b_in[l]))
        post_args = (row2(ln_g[l]), row2(ln_b[l]), wc, wa, wo, row2(g_ffn[l]), wr,
                     row2(b_router[l]))

        a_p, q_p, k_p, v_p, gate_p = _in_proj(hp, *proj_args, tab_p, seq // ROW_TILE,
                                              c_conv, attn_w)
        a_p3 = a_p.reshape(bsz, seq, c_conv)
        conv_p = _conv_prompt(a_p3, w_dw[l], row2(b_dw[l]))
        o_p = _attn_prompt(q_p.reshape(bsz, seq, attn_w), k_p.reshape(bsz, seq, attn_w),
                           v_p.reshape(bsz, seq, attn_w), lams, subg, lam_init)
        x1_p, hpk_p, ids_p, tw_p = _post(hp, conv_p.reshape(t_p, c_conv),
                                         o_p.reshape(t_p, attn_w), gate_p, *post_args)

        a_s, q_s, k_s, v_s, gate_s = _in_proj(hs, *proj_args, tab_s, 1, c_conv, attn_w)
        a_s3 = a_s.reshape(dec_b, dec_s, c_conv)
        conv_s = _conv_decode(state_conv[l], a_s3, w_dw[l], row2(b_dw[l]))
        pool = cache_k.shape[1]
        o_s = _attn_decode(page_table, q_s.astype(F32).reshape(dec_b, dec_s, attn_w),
                           k_s.reshape(dec_b, dec_s, attn_w), v_s.reshape(dec_b, dec_s, attn_w),
                           cache_k[l].reshape(pool * PAGE_SIZE * N_HEADS, HEAD_W),
                           cache_v[l].reshape(pool * PAGE_SIZE * N_HEADS, HEAD_W),
                           lams, subg, lam_init)
        x1_s, hpk_s, ids_s, tw_s = _post(hs, conv_s.reshape(t_s, c_conv),
                                         o_s.reshape(t_s, attn_w).astype(BF16), gate_s,
                                         *post_args)

        ids = jnp.concatenate([ids_p, ids_s], axis=0)
        pos, word, te = _positions(ids, n_tiles)
        tile_expert = jnp.minimum(te[:, 0], N_EXPERTS - 1)
        n_valid = jnp.sum((te[:, 0] < N_EXPERTS).astype(I32)).reshape(1)
        slot_idx = jnp.arange(n_rows, dtype=U32)
        n_out = t_all * TOP_K
        unused = (((n_out + slot_idx % MOE_TILE) << ROUTE_TOKEN_BITS)
                  | (t_all + slot_idx % PAD_TOKENS))
        inv = _inverse(pos.reshape(-1), word.reshape(-1), lax.bitcast_convert_type(unused, I32))
        tokens = jnp.concatenate(
            [hpk_p, hpk_s, jnp.zeros((PAD_TOKENS * (d // 2 // LANES), LANES), U32)], axis=0)
        expert_rows = _experts(tile_expert, n_valid, inv.reshape(n_tiles, 1, MOE_TILE), tokens,
                               w_moe1[l], b1, w_moe2[l], b2, t_all, n_out)
        gf = row2(g_final)
        hp = _combine(tw_p, x1_p, gf, expert_rows, 0)
        hs = _combine(tw_s, x1_s, gf, expert_rows, t_p // COMBINE_TILE)

        outs[0].append(k_p.reshape(bsz, seq, N_HEADS, HEAD_W))
        outs[1].append(v_p.reshape(bsz, seq, N_HEADS, HEAD_W))
        outs[2].append(a_p3[:, seq - (CONV_WIDTH - 1):])
        outs[3].append(k_s.reshape(dec_b, dec_s, N_HEADS, HEAD_W))
        outs[4].append(v_s.reshape(dec_b, dec_s, N_HEADS, HEAD_W))
        outs[5].append(jnp.concatenate([state_conv[l], a_s3], axis=1)[:, -(CONV_WIDTH - 1):])

    y_prompt = hp.reshape(bsz, seq, d)
    y_sample = hs.reshape(dec_b, dec_s, d)
    return (y_prompt, y_sample) + tuple(jnp.stack(o) for o in outs)
```

```python
import functools
import math

import jax
import jax.numpy as jnp
from jax import lax
from jax.experimental import pallas as pl
from jax.experimental.pallas import tpu as pltpu

F32 = jnp.float32
BF16 = jnp.bfloat16
I32 = jnp.int32
U32 = jnp.uint32

N_HEADS = 8
HEAD_DIM = 64
HEAD_W = 2 * HEAD_DIM
ROT_DIM = HEAD_DIM // 4
ROT_HALF = ROT_DIM // 2
ROPE_THETA = 500000.0
CONV_WIDTH = 31
CONV_HALO = 32
N_EXPERTS = 32
TOP_K = 4
SWIGLU_ALPHA = 1.702
SWIGLU_LIMIT = 7.0
RMS_EPS = 1e-5
LN_EPS = 1e-5
PAGE_SIZE = 128
LANES = 128
NEG_BIG = -1e30

ROW_TILE = 512
ATTN_BLOCK = 512
ATTN_SUB = 128
CONV_ROWS = 1024
CONV_CHUNK = 64
MOE_TILE = 256
EXPERT_CHUNK = 256
POS_TILE = 512
COMBINE_TILE = 256
ROUTE_TOKEN_BITS = 15
INVERSE_CHUNK = 11 * 1024
PAD_TOKENS = 64
VMEM_LIMIT = 56 * 1024 * 1024


def _cparams(n_axes, vmem=VMEM_LIMIT):
    return pltpu.CompilerParams(dimension_semantics=("arbitrary",) * n_axes,
                                vmem_limit_bytes=vmem)


def _rms(x, g):
    return x * lax.rsqrt(jnp.mean(x * x, axis=-1, keepdims=True) + RMS_EPS) * g


def _columns(cols):
    rows = cols[0].shape[0]
    lane = lax.broadcasted_iota(I32, (rows, len(cols)), 1)
    out = jnp.zeros((rows, len(cols)), cols[0].dtype)
    for k, col in enumerate(cols):
        out = jnp.where(lane == k, col, out)
    return out


def _lam(lq1, lk1, lq2, lk2, lam_init):
    s1 = jnp.sum(lq1[...] * lk1[...], axis=-1, keepdims=True)
    s2 = jnp.sum(lq2[...] * lk2[...], axis=-1, keepdims=True)
    return jnp.exp(s1) - jnp.exp(s2) + lam_init


def _rope_tables(pos):
    inv = jnp.power(jnp.float32(ROPE_THETA),
                    -jnp.arange(ROT_HALF, dtype=F32) * (2.0 / ROT_DIM))
    ang = pos.astype(F32)[:, None] * inv[None, :]
    cos, sin = jnp.cos(ang), jnp.sin(ang)
    n = pos.shape[0]
    rest = HEAD_DIM - ROT_DIM
    zh = jnp.zeros((n, ROT_HALF), F32)
    c64 = jnp.concatenate([cos, cos, jnp.ones((n, rest), F32)], axis=-1)
    sa64 = jnp.concatenate([-sin, zh, jnp.zeros((n, rest), F32)], axis=-1)
    sb64 = jnp.concatenate([zh, sin, jnp.zeros((n, rest), F32)], axis=-1)
    tile = lambda t: jnp.concatenate([t, t], axis=-1)
    return tile(c64), tile(sa64), tile(sb64)


def _in_proj_body(x_ref, g_ref, w_ref, b_ref, cos_ref, sa_ref, sb_ref,
                  a_ref, q_ref, k_ref, v_ref, gate_ref, *, c_conv, attn_w, d_model):
    h = _rms(x_ref[...], g_ref[...]).astype(BF16)
    cos, sa, sb = cos_ref[...], sa_ref[...], sb_ref[...]
    ch = 512

    def proj(c0):
        return (jnp.dot(h, w_ref[:, c0:c0 + ch], preferred_element_type=F32)
                + b_ref[:, c0:c0 + ch])

    def rope(z):
        outs = []
        for j in range(ch // LANES):
            zj = z[:, j * LANES:(j + 1) * LANES]
            outs.append(zj * cos + pltpu.roll(zj, LANES - ROT_HALF, 1) * sa
                        + pltpu.roll(zj, ROT_HALF, 1) * sb)
        return jnp.concatenate(outs, axis=-1)

    for c in range(0, c_conv, ch):
        a_ref[:, c:c + ch] = proj(c) * jax.nn.sigmoid(proj(c_conv + c))
    base = 2 * c_conv
    for c in range(0, attn_w, ch):
        q_ref[:, c:c + ch] = (rope(proj(base + c)) * (HEAD_DIM ** -0.5)).astype(BF16)
    base += attn_w
    for c in range(0, attn_w, ch):
        k_ref[:, c:c + ch] = rope(proj(base + c))
    base += attn_w
    for c in range(0, attn_w, ch):
        v_ref[:, c:c + ch] = proj(base + c)
    base += attn_w
    for c in range(0, 2 * d_model, ch):
        gate_ref[:, c:c + ch] = jax.nn.sigmoid(proj(base + c))


def _in_proj(x, g_mix, w_in_bf, b_in, tables, n_tab_blocks, c_conv, attn_w):
    t, d = x.shape
    n_in = w_in_bf.shape[1]
    tm = ROW_TILE
    row = lambda i: (i, 0)
    const = lambda i: (0, 0)
    tab = lambda i: (i % n_tab_blocks, 0)
    body = functools.partial(_in_proj_body, c_conv=c_conv, attn_w=attn_w, d_model=d)
    return pl.pallas_call(
        body,
        grid=(t // tm,),
        in_specs=[
            pl.BlockSpec((tm, d), row),
            pl.BlockSpec((1, d), const),
            pl.BlockSpec((d, n_in), const, pipeline_mode=pl.Buffered(1)),
            pl.BlockSpec((1, n_in), const),
            pl.BlockSpec((tm, LANES), tab),
            pl.BlockSpec((tm, LANES), tab),
            pl.BlockSpec((tm, LANES), tab),
        ],
        out_specs=[
            pl.BlockSpec((tm, c_conv), row),
            pl.BlockSpec((tm, attn_w), row),
            pl.BlockSpec((tm, attn_w), row),
            pl.BlockSpec((tm, attn_w), row),
            pl.BlockSpec((tm, 2 * d), row),
        ],
        out_shape=[
            jax.ShapeDtypeStruct((t, c_conv), F32),
            jax.ShapeDtypeStruct((t, attn_w), BF16),
            jax.ShapeDtypeStruct((t, attn_w), F32),
            jax.ShapeDtypeStruct((t, attn_w), F32),
            jax.ShapeDtypeStruct((t, 2 * d), F32),
        ],
        compiler_params=_cparams(1),
        name="in_proj",
    )(x, g_mix, w_in_bf, b_in, *tables)


def _conv_prompt_body(a_ref, halo_ref, w_ref, b_ref, o_ref, ext_ref, *, rows):
    i = pl.program_id(2)
    keep = jnp.where(i > 0, 1.0, 0.0).astype(F32)
    ext_ref[0:CONV_HALO, :] = halo_ref[0] * keep
    ext_ref[CONV_HALO:, :] = a_ref[0]
    w = w_ref[...]
    bias = jnp.broadcast_to(b_ref[...], (CONV_CHUNK, LANES))
    off = CONV_HALO - (CONV_WIDTH - 1)
    for r0 in range(0, rows, CONV_CHUNK):
        acc = bias
        for j in range(CONV_WIDTH):
            acc = acc + w[j:j + 1, :] * ext_ref[r0 + off + j:r0 + off + j + CONV_CHUNK, :]
        o_ref[0, r0:r0 + CONV_CHUNK, :] = acc


def _conv_prompt(a3, w_dw, b_dw):
    bsz, seq, c = a3.shape
    rows = CONV_ROWS
    hb = rows // CONV_HALO
    body = functools.partial(_conv_prompt_body, rows=rows)
    return pl.pallas_call(
        body,
        grid=(bsz, c // LANES, seq // rows),
        in_specs=[
            pl.BlockSpec((1, rows, LANES), lambda b, g, i: (b, i, g)),
            pl.BlockSpec((1, CONV_HALO, LANES),
                         lambda b, g, i: (b, jnp.maximum(i * hb - 1, 0), g)),
            pl.BlockSpec((CONV_WIDTH, LANES), lambda b, g, i: (0, g)),
            pl.BlockSpec((1, LANES), lambda b, g, i: (0, g)),
        ],
        out_specs=pl.BlockSpec((1, rows, LANES), lambda b, g, i: (b, i, g)),
        out_shape=jax.ShapeDtypeStruct((bsz, seq, c), F32),
        scratch_shapes=[pltpu.VMEM((rows + CONV_HALO, LANES), F32)],
        compiler_params=_cparams(3),
        name="conv_prompt",
    )(a3, a3, w_dw, b_dw)


def _conv_decode_body(state_ref, a_ref, w_ref, b_ref, o_ref, ext_ref, *, n_state, n_new):
    ext_ref[:, 0:n_state, :] = state_ref[...]
    ext_ref[:, n_state:n_state + n_new, :] = a_ref[...]
    w = w_ref[...]
    for t in range(n_new):
        win = ext_ref[:, t:t + CONV_WIDTH, :]
        o_ref[:, t:t + 1, :] = (jnp.sum(win * w[None], axis=1, keepdims=True)
                                + b_ref[...][None])


def _conv_decode(state, a3, w_dw, b_dw):
    bsz, n_state, c = state.shape
    n_new = a3.shape[1]
    bb = 8
    body = functools.partial(_conv_decode_body, n_state=n_state, n_new=n_new)
    return pl.pallas_call(
        body,
        grid=(bsz // bb,),
        in_specs=[
            pl.BlockSpec((bb, n_state, c), lambda i: (i, 0, 0)),
            pl.BlockSpec((bb, n_new, c), lambda i: (i, 0, 0)),
            pl.BlockSpec((CONV_WIDTH, c), lambda i: (0, 0)),
            pl.BlockSpec((1, c), lambda i: (0, 0)),
        ],
        out_specs=pl.BlockSpec((bb, n_new, c), lambda i: (i, 0, 0)),
        out_shape=jax.ShapeDtypeStruct((bsz, n_new, c), F32),
        scratch_shapes=[pltpu.VMEM((bb, n_state + n_new + 6, c), F32)],
        compiler_params=_cparams(1),
        name="conv_decode",
    )(state, a3, w_dw, b_dw)


def _attn_prompt_body(lq1, lk1, lq2, lk2, subg_ref, q_ref, k_ref, v_ref, o_ref,
                      kt_ref, vb_ref, acc1, acc2, m1, l1, m2, l2, *, blk, sub, n_blk, lam_init):
    i = pl.program_id(2)

    @pl.when(i == 0)
    def _():
        for c in range(n_blk):
            kt_ref[c] = k_ref[0, c * blk:(c + 1) * blk, :].T.astype(BF16)
            vb_ref[c] = v_ref[0, c * blk:(c + 1) * blk, :].astype(BF16)

    for m_ref, l_ref, acc_ref in ((m1, l1, acc1), (m2, l2, acc2)):
        m_ref[...] = jnp.full(m_ref.shape, NEG_BIG, F32)
        l_ref[...] = jnp.zeros(l_ref.shape, F32)
        acc_ref[...] = jnp.zeros(acc_ref.shape, F32)

    stats = ((m1, l1, acc1), (m2, l2, acc2))
    n_sub = blk // sub

    def scores(kb, sb, masked):
        r0 = sb * sub
        ncol = r0 + sub if masked else blk
        out = []
        for c in range(2):
            qq = q_ref[0, r0:r0 + sub, c * HEAD_DIM:(c + 1) * HEAD_DIM]
            kk = kt_ref[kb, c * HEAD_DIM:(c + 1) * HEAD_DIM, 0:ncol]
            out.append(jnp.dot(qq, kk, preferred_element_type=F32))
        return out

    def softmax_values(kb, sb, masked, s_pair):
        r0 = sb * sub
        rows = slice(r0, r0 + sub)
        ncol = r0 + sub if masked else blk
        ps, alphas = [], []
        for s, (m_ref, l_ref, _) in zip(s_pair, stats):
            if masked:
                row = lax.broadcasted_iota(I32, (sub, ncol), 0) + r0
                col = lax.broadcasted_iota(I32, (sub, ncol), 1)
                s = jnp.where(col <= row, s, NEG_BIG)
            m_old = m_ref[rows, :]
            m_new = jnp.maximum(m_old, jnp.max(s, axis=-1, keepdims=True))
            alpha = jnp.exp(m_old - m_new)
            p = jnp.exp(s - jnp.concatenate([m_new] * (ncol // LANES), axis=1))
            l_ref[rows, :] = alpha * l_ref[rows, :] + jnp.sum(p, axis=-1, keepdims=True)
            m_ref[rows, :] = m_new
            ps.append(p.astype(BF16))
            alphas.append(alpha)
        pv = jnp.dot(jnp.concatenate(ps, axis=0), vb_ref[kb, 0:ncol, :],
                     preferred_element_type=F32)
        for c, (alpha, (_, _, acc_ref)) in enumerate(zip(alphas, stats)):
            acc_ref[rows, :] = alpha * acc_ref[rows, :] + pv[c * sub:(c + 1) * sub]

    def run(chains):
        ahead = 2
        pending = {n: scores(*chains[n]) for n in range(min(ahead, len(chains)))}
        for n, chain in enumerate(chains):
            if n + ahead < len(chains):
                pending[n + ahead] = scores(*chains[n + ahead])
            softmax_values(*chain, pending.pop(n))

    def block(kb, masked):
        return [(kb, sb, masked) for sb in range(n_sub)]

    def pair_body(t, carry):
        run(block(2 * t, False) + block(2 * t + 1, False))
        return carry

    lax.fori_loop(0, lax.shift_right_logical(i, 1), pair_body, 0)
    odd = (i & 1) == 1

    @pl.when(odd)
    def _():
        run(block(i - 1, False) + block(i, True))

    @pl.when(jnp.logical_not(odd))
    def _():
        run(block(i, True))

    lam = _lam(lq1, lk1, lq2, lk2, lam_init)
    o = acc1[...] / l1[...] - lam * (acc2[...] / l2[...])
    o_ref[0] = (_rms(o, subg_ref[...]) * (1.0 - lam_init)).astype(BF16)


def _attn_prompt(q3, k3, v3, lams, subln_g, lam_init):
    bsz, seq, _ = q3.shape
    blk = ATTN_BLOCK
    n_blk = seq // blk
    vec = pl.BlockSpec((1, HEAD_DIM), lambda b, h, i: (0, 0))
    body = functools.partial(_attn_prompt_body, blk=blk, sub=ATTN_SUB, n_blk=n_blk,
                             lam_init=lam_init)
    return pl.pallas_call(
        body,
        grid=(bsz, N_HEADS, n_blk),
        in_specs=[
            vec, vec, vec, vec,
            pl.BlockSpec((1, HEAD_W), lambda b, h, i: (0, 0)),
            pl.BlockSpec((1, blk, HEAD_W), lambda b, h, i: (b, i, h)),
            pl.BlockSpec((1, seq, HEAD_W), lambda b, h, i: (b, 0, h)),
            pl.BlockSpec((1, seq, HEAD_W), lambda b, h, i: (b, 0, h)),
        ],
        out_specs=pl.BlockSpec((1, blk, HEAD_W), lambda b, h, i: (b, i, h)),
        out_shape=jax.ShapeDtypeStruct(q3.shape, BF16),
        scratch_shapes=[
            pltpu.VMEM((n_blk, HEAD_W, blk), BF16),
            pltpu.VMEM((n_blk, blk, HEAD_W), BF16),
            pltpu.VMEM((blk, HEAD_W), F32),
            pltpu.VMEM((blk, HEAD_W), F32),
            pltpu.VMEM((blk, LANES), F32),
            pltpu.VMEM((blk, LANES), F32),
            pltpu.VMEM((blk, LANES), F32),
            pltpu.VMEM((blk, LANES), F32),
        ],
        compiler_params=_cparams(3),
        name="attn_prompt",
    )(*lams, subln_g, q3, k3, v3)


NEW_PAD = 16
Q_PAD = 8
DECODE_PAGE_GROUP = 4


def _attn_decode_body(pt_ref, lq1, lk1, lq2, lk2, subg_ref, q_ref, kn_ref, vn_ref, ck_hbm,
                      cv_hbm, o_ref, kbuf, vbuf, s_ref, stage, sem, *, n_pages, n_new,
                      lam_init):
    b = pl.program_id(0)
    buf = lax.rem(b, 2)
    n_past = n_pages * PAGE_SIZE
    width = stage.shape[1]
    page_rows = PAGE_SIZE * N_HEADS

    def fetch(seq, half):
        def one_page(j, carry):
            src = pl.ds(pl.multiple_of(pt_ref[seq, j] * page_rows, page_rows), page_rows)
            dst = pl.ds(pl.multiple_of(j * page_rows, page_rows), page_rows)
            pltpu.make_async_copy(ck_hbm.at[src], kbuf.at[half, dst], sem.at[0, half]).start()
            pltpu.make_async_copy(cv_hbm.at[src], vbuf.at[half, dst], sem.at[1, half]).start()
            return carry
        lax.fori_loop(0, n_pages, one_page, 0)

    def wait_pages(hbm, dst_buf, which):
        pltpu.make_async_copy(hbm.at[pl.ds(0, n_pages * page_rows)], dst_buf.at[buf],
                              sem.at[which, buf]).wait()

    @pl.when(b == 0)
    def _():
        fetch(0, 0)

    @pl.when(b + 1 < pl.num_programs(0))
    def _():
        fetch(b + 1, 1 - buf)

    k_now = kbuf.at[buf]
    v_now = vbuf.at[buf]

    def padded_rows(rows_f32):
        stage[...] = jnp.zeros(stage.shape, F32)
        stage[0:n_new, :] = rows_f32
        return stage[...].astype(BF16)

    q16 = padded_rows(q_ref[0])
    sel_r = lax.broadcasted_iota(I32, (NEW_PAD, LANES), 0)
    sel_c = lax.broadcasted_iota(I32, (NEW_PAD, LANES), 1)
    sel = (sel_c % Q_PAD == sel_r).astype(BF16)
    qrep = lax.dot_general(q16, sel, (((0,), (0,)), ((), ())),
                           preferred_element_type=F32)
    rr = lax.broadcasted_iota(I32, (width, LANES), 0)
    cc = lax.broadcasted_iota(I32, (width, LANES), 1)
    qblk = jnp.where(rr // HEAD_DIM == cc // Q_PAD, qrep, 0.0).astype(BF16)

    def head_major(pages_ref, j):
        return jnp.concatenate(
            [pages_ref[pl.ds(j * page_rows + h, PAGE_SIZE, stride=N_HEADS), :].astype(BF16)
             for h in range(N_HEADS)], axis=-1)

    groups = [range(j0, min(j0 + DECODE_PAGE_GROUP, n_pages))
              for j0 in range(0, n_pages, DECODE_PAGE_GROUP)]
    row_slices = [slice(g[0] * PAGE_SIZE, (g[-1] + 1) * PAGE_SIZE) for g in groups]

    wait_pages(ck_hbm, kbuf, 0)
    for g, rows in zip(groups, row_slices):
        keys = jnp.concatenate([head_major(k_now, j) for j in g], axis=0)
        s_ref[rows, :] = jnp.dot(keys, qblk, preferred_element_type=F32)
    s_new = jnp.dot(padded_rows(kn_ref[0]), qblk, preferred_element_type=F32)
    new_idx = lax.broadcasted_iota(I32, (NEW_PAD, LANES), 0)
    slot = lax.broadcasted_iota(I32, (NEW_PAD, LANES), 1) % Q_PAD
    s_new = jnp.where((new_idx <= slot) & (new_idx < n_new), s_new, NEG_BIG)
    m = jnp.maximum(jnp.max(s_ref[0:n_past, :], axis=0, keepdims=True),
                    jnp.max(s_new, axis=0, keepdims=True))

    contract0 = (((0,), (0,)), ((), ()))
    e_new = jnp.exp(s_new - m)
    denom = jnp.sum(e_new, axis=0, keepdims=True)
    acc = lax.dot_general(e_new.astype(BF16), padded_rows(vn_ref[0]), contract0,
                          preferred_element_type=F32)
    wait_pages(cv_hbm, vbuf, 1)
    for g, rows in zip(groups, row_slices):
        e = jnp.exp(s_ref[rows, :] - m)
        denom = denom + jnp.sum(e, axis=0, keepdims=True)
        vals = jnp.concatenate([head_major(v_now, j) for j in g], axis=0)
        acc = acc + lax.dot_general(e.astype(BF16), vals, contract0,
                                    preferred_element_type=F32)

    r_i = lax.broadcasted_iota(I32, (LANES, LANES), 0)
    c_i = lax.broadcasted_iota(I32, (LANES, LANES), 1)
    denom_rows = jnp.sum(jnp.where(r_i == c_i, jnp.broadcast_to(denom, (LANES, LANES)), 0.0),
                         axis=1, keepdims=True)
    o_norm = acc / denom_rows
    lam = _lam(lq1, lk1, lq2, lk2, lam_init)
    outs = []
    for h in range(N_HEADS):
        r0 = h * 2 * Q_PAD
        cols = slice(h * HEAD_W, (h + 1) * HEAD_W)
        oh = o_norm[r0:r0 + Q_PAD, cols] - lam * o_norm[r0 + Q_PAD:r0 + 2 * Q_PAD, cols]
        outs.append(_rms(oh, subg_ref[...]) * (1.0 - lam_init))
    o_ref[0] = jnp.concatenate(outs, axis=-1)[:n_new]


def _attn_decode(page_table, q3, kn3, vn3, cache_k, cache_v, lams, subln_g, lam_init):
    bsz, n_new, width = q3.shape
    n_pages = page_table.shape[1]
    vec = pl.BlockSpec((1, HEAD_DIM), lambda b, pt: (0, 0))
    per_b = pl.BlockSpec((1, n_new, width), lambda b, pt: (b, 0, 0))

    body = functools.partial(_attn_decode_body, n_pages=n_pages, n_new=n_new,
                             lam_init=lam_init)
    hbm = pl.BlockSpec(memory_space=pl.ANY)
    past_rows = n_pages * PAGE_SIZE * N_HEADS
    grid_spec = pltpu.PrefetchScalarGridSpec(
        num_scalar_prefetch=1,
        grid=(bsz,),
        in_specs=[vec, vec, vec, vec, pl.BlockSpec((1, HEAD_W), lambda b, pt: (0, 0)),
                  per_b, per_b, per_b, hbm, hbm],
        out_specs=per_b,
        scratch_shapes=[pltpu.VMEM((2, past_rows, HEAD_W), F32),
                        pltpu.VMEM((2, past_rows, HEAD_W), F32),
                        pltpu.VMEM((n_pages * PAGE_SIZE, LANES), F32),
                        pltpu.VMEM((NEW_PAD, width), F32),
                        pltpu.SemaphoreType.DMA((2, 2))],
    )
    return pl.pallas_call(
        body,
        grid_spec=grid_spec,
        out_shape=jax.ShapeDtypeStruct(q3.shape, F32),
        compiler_params=_cparams(1),
        name="attn_decode",
    )(page_table, *lams, subln_g, q3, kn3, vn3, cache_k, cache_v)


def _post_body(x_ref, conv_ref, o_ref, gate_ref, lng, lnb, wc, wa, wo, gffn, wr, br, *rest,
               d_model):
    x1_ref, hp_ref, ids_ref, tw_ref = rest[-4:]
    n_groups = 2
    rows_per = x_ref.shape[0] // n_groups
    groups = [slice(g * rows_per, (g + 1) * rows_per) for g in range(n_groups)]
    attn_outs = [jnp.dot(o_ref[r, :], wa[...], preferred_element_type=F32) for r in groups]
    conv_outs = []
    for r in groups:
        c = conv_ref[r, :]
        mu = jnp.mean(c, axis=-1, keepdims=True)
        xc = c - mu
        cn = (xc * lax.rsqrt(jnp.mean(xc * xc, axis=-1, keepdims=True) + LN_EPS) * lng[...]
              + lnb[...])
        cact = (cn * jax.nn.sigmoid(cn)).astype(BF16)
        conv_outs.append(jnp.dot(cact, wc[...], preferred_element_type=F32))
    hbs = []
    for r, conv_out, attn_out in zip(groups, conv_outs, attn_outs):
        merged = (gate_ref[r, :d_model] * conv_out
                  + gate_ref[r, d_model:] * attn_out).astype(BF16)
        x1 = x_ref[r, :] + jnp.dot(merged, wo[...], preferred_element_type=F32)
        x1_ref[r, :] = x1
        hbs.append(_rms(x1, gffn[...]).astype(BF16))
    for r, hb in zip(groups, hbs):
        logits = jnp.dot(hb, wr[...], preferred_element_type=F32) + br[...]
        lane = lax.broadcasted_iota(I32, logits.shape, 1)
        vals, ids = [], []
        cur = logits
        for _ in range(TOP_K):
            mx = jnp.max(cur, axis=-1, keepdims=True)
            idx = jnp.min(jnp.where(cur == mx, lane, N_EXPERTS), axis=-1, keepdims=True)
            vals.append(mx)
            ids.append(idx)
            cur = jnp.where(lane == idx, -jnp.inf, cur)
        es = [jnp.exp(v - vals[0]) for v in vals]
        den = es[0] + es[1] + es[2] + es[3]
        tw_ref[r, :] = _columns([e / den for e in es])
        ids_ref[r, :] = _columns(ids)

        bits = pltpu.bitcast(hb.astype(F32), U32)
        half = d_model // 2
        words = (bits[:, :half] >> 16) | (bits[:, half:] & jnp.uint32(0xFFFF0000))
        pieces = half // LANES
        for c in range(pieces):
            hp_ref[pl.ds(r.start * pieces + c, rows_per, stride=pieces), :] = (
                words[:, c * LANES:(c + 1) * LANES])


def _post(x, conv, o, gate, ln_g, ln_b, wc, wa, wo, g_ffn, wr, br, tokens, first_token):
    t, d = x.shape
    tm = ROW_TILE
    first_tile = first_token // tm
    row = lambda i: (i, 0)
    const = lambda i: (0, 0)
    mat = pl.BlockSpec((d, d), const)
    vec = pl.BlockSpec((1, d), const)
    body = functools.partial(_post_body, d_model=d)
    in_specs = [
        pl.BlockSpec((tm, d), row), pl.BlockSpec((tm, d), row), pl.BlockSpec((tm, d), row),
        pl.BlockSpec((tm, 2 * d), row),
        vec, vec, mat, mat, mat, vec,
        pl.BlockSpec((d, N_EXPERTS), const), pl.BlockSpec((1, N_EXPERTS), const),
    ]
    in_specs.append(pl.BlockSpec(memory_space=pl.ANY))
    args = [x, conv, o, gate, ln_g, ln_b, wc, wa, wo, g_ffn, wr, br, tokens]
    return pl.pallas_call(
        body,
        grid=(t // tm,),
        in_specs=in_specs,
        out_specs=[
            pl.BlockSpec((tm, d), row),
            pl.BlockSpec((tm * (d // 2 // LANES), LANES), lambda i: (first_tile + i, 0)),
            pl.BlockSpec((tm, TOP_K), row), pl.BlockSpec((tm, TOP_K), row),
        ],
        out_shape=[
            jax.ShapeDtypeStruct((t, d), F32),
            jax.ShapeDtypeStruct(tokens.shape, U32),
            jax.ShapeDtypeStruct((t, TOP_K), I32), jax.ShapeDtypeStruct((t, TOP_K), F32),
        ],
        input_output_aliases={len(args) - 1: 1},
        compiler_params=_cparams(1),
        name="post",
    )(*args)


def _lane_cumsum(x):
    lane = lax.broadcasted_iota(I32, x.shape, 1)
    s = 1
    while s < LANES:
        x = x + jnp.where(lane >= s, pltpu.roll(x, s, 1), 0.0)
        s *= 2
    return x


def _wrap_i32(value):
    return (value + 2 ** 31) % 2 ** 32 - 2 ** 31


def _route_word(token, slot):
    dst = ((token // COMBINE_TILE) * (TOP_K * COMBINE_TILE) + slot * COMBINE_TILE
           + token % COMBINE_TILE)
    return jnp.left_shift(dst, ROUTE_TOKEN_BITS) | token


def _positions_body(ids_ref, pos_ref, word_ref, te_ref, count_ref, start_ref, *, n_tiles_pad):
    p = pl.program_id(0)
    i = pl.program_id(1)
    ids = ids_ref[...]
    tt = ids.shape[0]
    lane = lax.broadcasted_iota(I32, (tt, LANES), 1)
    onehots = [ids[:, k:k + 1] == lane for k in range(TOP_K)]
    tile_counts = [jnp.sum(oh.astype(F32), axis=0, keepdims=True) for oh in onehots]
    tile_total = tile_counts[0] + tile_counts[1] + tile_counts[2] + tile_counts[3]

    @pl.when((p == 0) & (i == 0))
    def _():
        count_ref[...] = jnp.zeros(count_ref.shape, F32)

    @pl.when((p == 1) & (i == 0))
    def _():
        counts = count_ref[...]
        padded = jnp.ceil(counts * (1.0 / MOE_TILE)) * MOE_TILE
        ends = _lane_cumsum(padded)
        start_ref[...] = ends - padded
        count_ref[...] = jnp.zeros(count_ref.shape, F32)
        tile_start = (lax.broadcasted_iota(I32, (n_tiles_pad, LANES), 0) * MOE_TILE).astype(F32)
        elane = lax.broadcasted_iota(I32, (n_tiles_pad, LANES), 1)
        done = (ends[0:1, :] <= tile_start) & (elane < N_EXPERTS)
        n_done = jnp.sum(done.astype(F32), axis=-1, keepdims=True)
        te_ref[...] = jnp.broadcast_to(n_done, (n_tiles_pad, LANES)).astype(I32)

    @pl.when(p == 1)
    def _():
        r = lax.broadcasted_iota(I32, (tt, tt), 0)
        c = lax.broadcasted_iota(I32, (tt, tt), 1)
        earlier = (c < r).astype(BF16)
        run = start_ref[0:1, :] + count_ref[0:1, :]
        cols = []
        for k in range(TOP_K):
            within = jnp.dot(earlier, onehots[k].astype(BF16), preferred_element_type=F32)
            cols.append(jnp.sum(jnp.where(onehots[k], within + run, 0.0),
                                axis=-1, keepdims=True))
            run = run + tile_counts[k]
        pos_ref[...] = _columns(cols).astype(I32)
        token = lax.broadcasted_iota(I32, (tt, TOP_K), 0) + i * tt
        slot = lax.broadcasted_iota(I32, (tt, TOP_K), 1)
        word_ref[...] = _route_word(token, slot)

    count_ref[...] = count_ref[...] + tile_total


def _positions(ids, n_tiles_pad):
    t = ids.shape[0]
    tt = POS_TILE
    body = functools.partial(_positions_body, n_tiles_pad=n_tiles_pad)
    return pl.pallas_call(
        body,
        grid=(2, t // tt),
        in_specs=[pl.BlockSpec((tt, TOP_K), lambda p, i: (i, 0))],
        out_specs=[
            pl.BlockSpec((tt, TOP_K), lambda p, i: (i * p, 0)),
            pl.BlockSpec((tt, TOP_K), lambda p, i: (i * p, 0)),
            pl.BlockSpec((n_tiles_pad, LANES), lambda p, i: (0, 0)),
        ],
        out_shape=[
            jax.ShapeDtypeStruct((t, TOP_K), I32),
            jax.ShapeDtypeStruct((t, TOP_K), I32),
            jax.ShapeDtypeStruct((n_tiles_pad, LANES), I32),
        ],
        scratch_shapes=[pltpu.VMEM((8, LANES), F32), pltpu.VMEM((8, LANES), F32)],
        compiler_params=_cparams(2),
        name="moe_positions",
    )(ids)


def _inverse_body(pos_ref, word_ref, init_hbm, inv_hbm, inv_smem, sem):
    c = pl.program_id(0)

    @pl.when(c == 0)
    def _():
        load = pltpu.make_async_copy(init_hbm, inv_smem, sem)
        load.start()
        load.wait()

    def place(a, carry):
        inv_smem[pos_ref[a]] = word_ref[a]
        return carry

    lax.fori_loop(0, pos_ref.shape[0], place, 0, unroll=8)

    @pl.when(c == pl.num_programs(0) - 1)
    def _():
        store = pltpu.make_async_copy(inv_smem, inv_hbm, sem)
        store.start()
        store.wait()


def _inverse(pos_flat, word_flat, init):
    n = pos_flat.shape[0]
    chunk = INVERSE_CHUNK
    assert n % chunk == 0
    smem_chunk = pl.BlockSpec((chunk,), lambda c: (c,), memory_space=pltpu.SMEM)
    return pl.pallas_call(
        _inverse_body,
        grid=(n // chunk,),
        in_specs=[smem_chunk, smem_chunk, pl.BlockSpec(memory_space=pl.ANY)],
        out_specs=pl.BlockSpec(memory_space=pl.ANY),
        out_shape=jax.ShapeDtypeStruct(init.shape, I32),
        scratch_shapes=[pltpu.SMEM(init.shape, I32), pltpu.SemaphoreType.DMA],
        compiler_params=_cparams(1),
        name="moe_inverse",
    )(pos_flat, word_flat, init)


def _experts_body(te_ref, nv_ref, inv_next, inv_prev, w1_ref, b1_ref, w2_ref, b2_ref, tok_hbm,
                  out_hbm, xs_buf, y_buf, act_buf, w1b, w2b, gsem, ssem, *, n_tok, n_out):
    j = pl.program_id(0)
    nv = nv_ref[0]
    d_ff, d = w2b.shape
    half = d // 2
    xp, yp = half // LANES, d // LANES
    tm = xs_buf.shape[1] // xp
    slot = lax.rem(j, 2)
    other = 1 - slot
    token_mask = (1 << ROUTE_TOKEN_BITS) - 1

    def gather_row(word, r, buf_slot):
        tok = pl.multiple_of((word & token_mask) * xp, xp)
        return pltpu.make_async_copy(tok_hbm.at[pl.ds(tok, xp)],
                                     xs_buf.at[buf_slot, pl.ds(r * xp, xp)], gsem)

    def scatter_row(word, r, buf_slot):
        dst = pl.multiple_of(lax.shift_right_logical(word, ROUTE_TOKEN_BITS) * yp, yp)
        return pltpu.make_async_copy(y_buf.at[buf_slot, pl.ds(r * yp, yp)],
                                     out_hbm.at[pl.ds(dst, yp)], ssem.at[buf_slot])

    def spare_word(r):
        return _wrap_i32((n_out + r) << ROUTE_TOKEN_BITS)

    def wait_gather(buf_slot):
        pltpu.make_async_copy(tok_hbm.at[pl.ds(0, tm * xp)], xs_buf.at[buf_slot], gsem).wait()

    def wait_scatter(buf_slot):
        pltpu.make_async_copy(y_buf.at[buf_slot], out_hbm.at[pl.ds(0, tm * yp)],
                              ssem.at[buf_slot]).wait()

    @pl.when(j == 0)
    def _():
        y_buf[...] = jnp.zeros(y_buf.shape, F32)
        for r in range(tm):
            gather_row(inv_prev[0, 0, r], r, 0).start()
            scatter_row(spare_word(r), r, 0).start()

    @pl.when(j < nv)
    def _():
        wait_gather(slot)

        @pl.when((j == 0) | (te_ref[j] != te_ref[jnp.maximum(j - 1, 0)]))
        def _():
            for c in range(0, d, LANES):
                w1b[c:c + LANES, :] = w1_ref[0, c:c + LANES, :].astype(BF16)
            for c in range(0, d_ff, LANES):
                w2b[c:c + LANES, :] = w2_ref[0, c:c + LANES, :].astype(BF16)

        has_prev = j > 0

        def gather_rows(r0, r1):
            for r in range(r0, r1):
                gather_row(inv_next[0, 0, r], r, other).start()

        def scatter_rows(r0, r1):
            for r in range(r0, r1):
                word = jnp.where(has_prev, inv_prev[0, 0, r], spare_word(r))
                scatter_row(word, r, other).start()

        xs_now = xs_buf.at[slot]
        bits = jnp.concatenate(
            [xs_now[pl.ds(c, tm, stride=xp), :] for c in range(xp)], axis=-1)
        x_lo = pltpu.bitcast(bits << 16, F32).astype(BF16)
        x_hi = pltpu.bitcast(bits & jnp.uint32(0xFFFF0000), F32).astype(BF16)
        y_now = y_buf.at[slot]
        ch = EXPERT_CHUNK

        def proj1(c0):
            return (jnp.dot(x_lo, w1b[:half, c0:c0 + ch], preferred_element_type=F32)
                    + jnp.dot(x_hi, w1b[half:, c0:c0 + ch], preferred_element_type=F32)
                    + b1_ref[0, :, c0:c0 + ch])

        rows_1 = tm // (d_ff // ch)
        for n, c in enumerate(range(0, d_ff, ch)):
            gate = jnp.minimum(proj1(c), SWIGLU_LIMIT)
            lin = jnp.clip(proj1(d_ff + c), -SWIGLU_LIMIT, SWIGLU_LIMIT)
            act_buf[:, c:c + ch] = (gate * jax.nn.sigmoid(SWIGLU_ALPHA * gate)
                                    * (lin + 1.0)).astype(BF16)
            gather_rows(n * rows_1, (n + 1) * rows_1)
        wait_scatter(slot)
        rows_2 = tm // (d // ch)
        for n, c in enumerate(range(0, d, ch)):
            y_cols = (jnp.dot(act_buf[...], w2b[:, c:c + ch], preferred_element_type=F32)
                      + b2_ref[0, :, c:c + ch])
            for g in range(ch // LANES):
                y_now[pl.ds(c // LANES + g, tm, stride=yp), :] = (
                    y_cols[:, g * LANES:(g + 1) * LANES])
            scatter_rows(n * rows_2, (n + 1) * rows_2)

    @pl.when(j == nv)
    def _():
        wait_gather(slot)
        wait_scatter(slot)
        for r in range(tm):
            scatter_row(inv_prev[0, 0, r], r, other).start()
        wait_scatter(other)


def _experts(tile_expert, n_valid, inv3, tokens, w1, b1, w2, b2, n_tok, n_out):
    nt, _, tm = inv3.shape
    d_ff, d = w2.shape[1:]
    half = d // 2
    xp, yp = half // LANES, d // LANES
    body = functools.partial(_experts_body, n_tok=n_tok, n_out=n_out)
    smem_tile = lambda index_map: pl.BlockSpec((1, 1, tm), index_map, memory_space=pltpu.SMEM)
    grid_spec = pltpu.PrefetchScalarGridSpec(
        num_scalar_prefetch=2,
        grid=(nt,),
        in_specs=[
            smem_tile(lambda j, te, nv: (jnp.minimum(j + 1, nt - 1), 0, 0)),
            smem_tile(lambda j, te, nv: (jnp.maximum(j - 1, 0), 0, 0)),
            pl.BlockSpec((1, d, 2 * d_ff), lambda j, te, nv: (te[j], 0, 0)),
            pl.BlockSpec((1, 1, 2 * d_ff), lambda j, te, nv: (te[j], 0, 0)),
            pl.BlockSpec((1, d_ff, d), lambda j, te, nv: (te[j], 0, 0)),
            pl.BlockSpec((1, 1, d), lambda j, te, nv: (te[j], 0, 0)),
            pl.BlockSpec(memory_space=pl.ANY),
        ],
        out_specs=pl.BlockSpec(memory_space=pl.ANY),
        scratch_shapes=[
            pltpu.VMEM((2, tm * xp, LANES), U32),
            pltpu.VMEM((2, tm * yp, LANES), F32),
            pltpu.VMEM((tm, d_ff), BF16),
            pltpu.VMEM((d, 2 * d_ff), BF16),
            pltpu.VMEM((d_ff, d), BF16),
            pltpu.SemaphoreType.DMA,
            pltpu.SemaphoreType.DMA((2,)),
        ],
    )
    return pl.pallas_call(
        body,
        grid_spec=grid_spec,
        out_shape=jax.ShapeDtypeStruct(((n_out + tm) * yp, LANES), F32),
        compiler_params=_cparams(1),
        name="moe_experts",
    )(tile_expert, n_valid, inv3, inv3, w1, b1, w2, b2, tokens)


def _combine_body(tw_ref, x1_ref, gf_ref, rows_ref, y_ref):
    tt, d = x1_ref.shape
    yp = d // LANES
    tw = tw_ref[...]
    cols = []
    for c in range(yp):
        acc = None
        for k in range(TOP_K):
            piece = rows_ref[pl.ds(k * tt * yp + c, tt, stride=yp), :]
            term = tw[:, k:k + 1] * piece
            acc = term if acc is None else acc + term
        cols.append(acc)
    y_ref[...] = _rms(x1_ref[...] + jnp.concatenate(cols, axis=-1), gf_ref[...])


def _combine(tw, x1, g_final, expert_rows, first_tile):
    t, d = x1.shape
    tt = COMBINE_TILE
    return pl.pallas_call(
        _combine_body,
        grid=(t // tt,),
        in_specs=[
            pl.BlockSpec((tt, TOP_K), lambda i: (i, 0)),
            pl.BlockSpec((tt, d), lambda i: (i, 0)),
            pl.BlockSpec((1, d), lambda i: (0, 0)),
            pl.BlockSpec((TOP_K * tt * (d // LANES), LANES), lambda i: (first_tile + i, 0)),
        ],
        out_specs=pl.BlockSpec((tt, d), lambda i: (i, 0)),
        out_shape=jax.ShapeDtypeStruct((t, d), F32),
        compiler_params=_cparams(1),
        name="moe_combine",
    )(tw, x1, g_final, expert_rows)


def kernel(x_prompt, x_sample, cache_k, cache_v, state_conv, page_table, g_mix, w_in, b_in,
           w_dw, b_dw, ln_g, ln_b, w_conv_out, lam_q1, lam_k1, lam_q2, lam_k2, subln_g,
           w_attn_out, w_o, g_ffn, w_router, b_router, w_moe1, b_moe1, w_moe2, b_moe2, g_final):
    bsz, seq, d = x_prompt.shape
    dec_b, dec_s, _ = x_sample.shape
    depth = g_mix.shape[0]
    c_conv = w_dw.shape[2]
    attn_w = N_HEADS * HEAD_W
    n_pages = page_table.shape[1]
    past_len = n_pages * PAGE_SIZE
    t_p, t_s = bsz * seq, dec_b * dec_s
    t_all = t_p + t_s
    d_ff = w_moe2.shape[2]
    assert depth == 1, "the combine kernel fuses the final norm, so only one layer is supported"
    assert seq % ROW_TILE == 0 and t_s % ROW_TILE == 0 and seq % CONV_ROWS == 0
    assert seq % ATTN_BLOCK == 0 and t_all % POS_TILE == 0 and dec_b % 8 == 0
    assert t_p % COMBINE_TILE == 0 and t_s % COMBINE_TILE == 0

    tab_p = _rope_tables(jnp.arange(seq))
    tab_s = _rope_tables(jnp.tile(past_len + jnp.arange(dec_s), dec_b))
    n_rows = t_all * TOP_K + N_EXPERTS * MOE_TILE
    n_tiles = n_rows // MOE_TILE
    row2 = lambda v: v.reshape(1, -1)

    hp, hs = x_prompt.reshape(t_p, d), x_sample.reshape(t_s, d)
    outs = [[] for _ in range(6)]
    for l in range(depth):
        lam_init = 0.8 - 0.6 * math.exp(-0.3 * l)
        lams = (row2(lam_q1[l]), row2(lam_k1[l]), row2(lam_q2[l]), row2(lam_k2[l]))
        subg = row2(subln_g[l])
        w_in_bf = w_in[l].astype(BF16)
        wc, wa, wo = (w_conv_out[l].astype(BF16), w_attn_out[l].astype(BF16),
                      w_o[l].astype(BF16))
        wr = w_router[l].astype(BF16)
        b1 = b_moe1[l].reshape(N_EXPERTS, 1, 2 * d_ff)
        b2 = b_moe2[l].reshape(N_EXPERTS, 1, d)
        proj_args = (row2(g_mix[l]), w_in_bf, row2(b_in[l]))
        post_args = (row2(ln_g[l]), row2(ln_b[l]), wc, wa, wo, row2(g_ffn[l]), wr,
                     row2(b_router[l]))

        a_p, q_p, k_p, v_p, gate_p = _in_proj(hp, *proj_args, tab_p, seq // ROW_TILE,
                                              c_conv, attn_w)
        a_p3 = a_p.reshape(bsz, seq, c_conv)
        conv_p = _conv_prompt(a_p3, w_dw[l], row2(b_dw[l]))
        o_p = _attn_prompt(q_p.reshape(bsz, seq, attn_w), k_p.reshape(bsz, seq, attn_w),
                           v_p.reshape(bsz, seq, attn_w), lams, subg, lam_init)
        tokens = jnp.zeros(((t_all + PAD_TOKENS) * (d // 2 // LANES), LANES), U32)
        x1_p, tokens, ids_p, tw_p = _post(hp, conv_p.reshape(t_p, c_conv),
                                          o_p.reshape(t_p, attn_w), gate_p, *post_args,
                                          tokens, 0)

        a_s, q_s, k_s, v_s, gate_s = _in_proj(hs, *proj_args, tab_s, 1, c_conv, attn_w)
        a_s3 = a_s.reshape(dec_b, dec_s, c_conv)
        conv_s = _conv_decode(state_conv[l], a_s3, w_dw[l], row2(b_dw[l]))
        pool = cache_k.shape[1]
        o_s = _attn_decode(page_table, q_s.astype(F32).reshape(dec_b, dec_s, attn_w),
                           k_s.reshape(dec_b, dec_s, attn_w), v_s.reshape(dec_b, dec_s, attn_w),
                           cache_k[l].reshape(pool * PAGE_SIZE * N_HEADS, HEAD_W),
                           cache_v[l].reshape(pool * PAGE_SIZE * N_HEADS, HEAD_W),
                           lams, subg, lam_init)
        x1_s, tokens, ids_s, tw_s = _post(hs, conv_s.reshape(t_s, c_conv),
                                          o_s.reshape(t_s, attn_w).astype(BF16), gate_s,
                                          *post_args, tokens, t_p)

        ids = jnp.concatenate([ids_p, ids_s], axis=0)
        pos, word, te = _positions(ids, n_tiles)
        tile_expert = jnp.minimum(te[:, 0], N_EXPERTS - 1)
        n_valid = jnp.sum((te[:, 0] < N_EXPERTS).astype(I32)).reshape(1)
        slot_idx = jnp.arange(n_rows, dtype=U32)
        n_out = t_all * TOP_K
        unused = (((n_out + slot_idx % MOE_TILE) << ROUTE_TOKEN_BITS)
                  | (t_all + slot_idx % PAD_TOKENS))
        inv = _inverse(pos.reshape(-1), word.reshape(-1), lax.bitcast_convert_type(unused, I32))
        expert_rows = _experts(tile_expert, n_valid, inv.reshape(n_tiles, 1, MOE_TILE), tokens,
                               w_moe1[l], b1, w_moe2[l], b2, t_all, n_out)
        gf = row2(g_final)
        hp = _combine(tw_p, x1_p, gf, expert_rows, 0)
        hs = _combine(tw_s, x1_s, gf, expert_rows, t_p // COMBINE_TILE)

        outs[0].append(k_p.reshape(bsz, seq, N_HEADS, HEAD_W))
        outs[1].append(v_p.reshape(bsz, seq, N_HEADS, HEAD_W))
        outs[2].append(a_p3[:, seq - (CONV_WIDTH - 1):])
        outs[3].append(k_s.reshape(dec_b, dec_s, N_HEADS, HEAD_W))
        outs[4].append(v_s.reshape(dec_b, dec_s, N_HEADS, HEAD_W))
        outs[5].append(jnp.concatenate([state_conv[l], a_s3], axis=1)[:, -(CONV_WIDTH - 1):])

    y_prompt = hp.reshape(bsz, seq, d)
    y_sample = hs.reshape(dec_b, dec_s, d)
    return (y_prompt, y_sample) + tuple(jnp.stack(o) for o in outs)
```

```python
import functools
import math

import jax
import jax.numpy as jnp
from jax import lax
from jax.experimental import pallas as pl
from jax.experimental.pallas import tpu as pltpu

F32 = jnp.float32
BF16 = jnp.bfloat16
I32 = jnp.int32
U32 = jnp.uint32

N_HEADS = 8
HEAD_DIM = 64
HEAD_W = 2 * HEAD_DIM
ROT_DIM = HEAD_DIM // 4
ROT_HALF = ROT_DIM // 2
ROPE_THETA = 500000.0
CONV_WIDTH = 31
CONV_HALO = 32
N_EXPERTS = 32
TOP_K = 4
SWIGLU_ALPHA = 1.702
SWIGLU_LIMIT = 7.0
RMS_EPS = 1e-5
LN_EPS = 1e-5
PAGE_SIZE = 128
LANES = 128
NEG_BIG = -1e30

ROW_TILE = 512
ATTN_BLOCK = 512
ATTN_SUB = 128
CONV_ROWS = 1024
CONV_CHUNK = 64
MOE_TILE = 256
EXPERT_CHUNK = 256
POS_TILE = 512
COMBINE_TILE = 256
ROUTE_TOKEN_BITS = 15
INVERSE_CHUNK = 11 * 1024
PAD_TOKENS = 64
VMEM_LIMIT = 56 * 1024 * 1024


def _cparams(n_axes, vmem=VMEM_LIMIT):
    return pltpu.CompilerParams(dimension_semantics=("arbitrary",) * n_axes,
                                vmem_limit_bytes=vmem)


def _rms(x, g):
    return x * lax.rsqrt(jnp.mean(x * x, axis=-1, keepdims=True) + RMS_EPS) * g


def _columns(cols):
    rows = cols[0].shape[0]
    lane = lax.broadcasted_iota(I32, (rows, len(cols)), 1)
    out = jnp.zeros((rows, len(cols)), cols[0].dtype)
    for k, col in enumerate(cols):
        out = jnp.where(lane == k, col, out)
    return out


def _lam(lq1, lk1, lq2, lk2, lam_init):
    s1 = jnp.sum(lq1[...] * lk1[...], axis=-1, keepdims=True)
    s2 = jnp.sum(lq2[...] * lk2[...], axis=-1, keepdims=True)
    return jnp.exp(s1) - jnp.exp(s2) + lam_init


def _rope_tables(pos):
    inv = jnp.power(jnp.float32(ROPE_THETA),
                    -jnp.arange(ROT_HALF, dtype=F32) * (2.0 / ROT_DIM))
    ang = pos.astype(F32)[:, None] * inv[None, :]
    cos, sin = jnp.cos(ang), jnp.sin(ang)
    n = pos.shape[0]
    rest = HEAD_DIM - ROT_DIM
    zh = jnp.zeros((n, ROT_HALF), F32)
    c64 = jnp.concatenate([cos, cos, jnp.ones((n, rest), F32)], axis=-1)
    sa64 = jnp.concatenate([-sin, zh, jnp.zeros((n, rest), F32)], axis=-1)
    sb64 = jnp.concatenate([zh, sin, jnp.zeros((n, rest), F32)], axis=-1)
    tile = lambda t: jnp.concatenate([t, t], axis=-1)
    return tile(c64), tile(sa64), tile(sb64)


def _in_proj_body(x_ref, g_ref, w_ref, b_ref, cos_ref, sa_ref, sb_ref,
                  a_ref, q_ref, k_ref, v_ref, gate_ref, *, c_conv, attn_w, d_model):
    h = _rms(x_ref[...], g_ref[...]).astype(BF16)
    cos, sa, sb = cos_ref[...], sa_ref[...], sb_ref[...]
    ch = 512

    def proj(c0):
        return (jnp.dot(h, w_ref[:, c0:c0 + ch], preferred_element_type=F32)
                + b_ref[:, c0:c0 + ch])

    def rope(z):
        outs = []
        for j in range(ch // LANES):
            zj = z[:, j * LANES:(j + 1) * LANES]
            outs.append(zj * cos + pltpu.roll(zj, LANES - ROT_HALF, 1) * sa
                        + pltpu.roll(zj, ROT_HALF, 1) * sb)
        return jnp.concatenate(outs, axis=-1)

    for c in range(0, c_conv, ch):
        a_ref[:, c:c + ch] = proj(c) * jax.nn.sigmoid(proj(c_conv + c))
    base = 2 * c_conv
    for c in range(0, attn_w, ch):
        q_ref[:, c:c + ch] = (rope(proj(base + c)) * (HEAD_DIM ** -0.5)).astype(BF16)
    base += attn_w
    for c in range(0, attn_w, ch):
        k_ref[:, c:c + ch] = rope(proj(base + c))
    base += attn_w
    for c in range(0, attn_w, ch):
        v_ref[:, c:c + ch] = proj(base + c)
    base += attn_w
    for c in range(0, 2 * d_model, ch):
        gate_ref[:, c:c + ch] = jax.nn.sigmoid(proj(base + c))


def _in_proj(x, g_mix, w_in_bf, b_in, tables, n_tab_blocks, c_conv, attn_w):
    t, d = x.shape
    n_in = w_in_bf.shape[1]
    tm = ROW_TILE
    row = lambda i: (i, 0)
    const = lambda i: (0, 0)
    tab = lambda i: (i % n_tab_blocks, 0)
    body = functools.partial(_in_proj_body, c_conv=c_conv, attn_w=attn_w, d_model=d)
    return pl.pallas_call(
        body,
        grid=(t // tm,),
        in_specs=[
            pl.BlockSpec((tm, d), row),
            pl.BlockSpec((1, d), const),
            pl.BlockSpec((d, n_in), const, pipeline_mode=pl.Buffered(1)),
            pl.BlockSpec((1, n_in), const),
            pl.BlockSpec((tm, LANES), tab),
            pl.BlockSpec((tm, LANES), tab),
            pl.BlockSpec((tm, LANES), tab),
        ],
        out_specs=[
            pl.BlockSpec((tm, c_conv), row),
            pl.BlockSpec((tm, attn_w), row),
            pl.BlockSpec((tm, attn_w), row),
            pl.BlockSpec((tm, attn_w), row),
            pl.BlockSpec((tm, 2 * d), row),
        ],
        out_shape=[
            jax.ShapeDtypeStruct((t, c_conv), F32),
            jax.ShapeDtypeStruct((t, attn_w), BF16),
            jax.ShapeDtypeStruct((t, attn_w), F32),
            jax.ShapeDtypeStruct((t, attn_w), F32),
            jax.ShapeDtypeStruct((t, 2 * d), F32),
        ],
        compiler_params=_cparams(1),
        name="in_proj",
    )(x, g_mix, w_in_bf, b_in, *tables)


def _conv_prompt_body(a_ref, halo_ref, w_ref, b_ref, o_ref, ext_ref, *, rows):
    i = pl.program_id(2)
    keep = jnp.where(i > 0, 1.0, 0.0).astype(F32)
    ext_ref[0:CONV_HALO, :] = halo_ref[0] * keep
    ext_ref[CONV_HALO:, :] = a_ref[0]
    w = w_ref[...]
    bias = jnp.broadcast_to(b_ref[...], (CONV_CHUNK, LANES))
    off = CONV_HALO - (CONV_WIDTH - 1)
    for r0 in range(0, rows, CONV_CHUNK):
        acc = bias
        for j in range(CONV_WIDTH):
            acc = acc + w[j:j + 1, :] * ext_ref[r0 + off + j:r0 + off + j + CONV_CHUNK, :]
        o_ref[0, r0:r0 + CONV_CHUNK, :] = acc


def _conv_prompt(a3, w_dw, b_dw):
    bsz, seq, c = a3.shape
    rows = CONV_ROWS
    hb = rows // CONV_HALO
    body = functools.partial(_conv_prompt_body, rows=rows)
    return pl.pallas_call(
        body,
        grid=(bsz, c // LANES, seq // rows),
        in_specs=[
            pl.BlockSpec((1, rows, LANES), lambda b, g, i: (b, i, g)),
            pl.BlockSpec((1, CONV_HALO, LANES),
                         lambda b, g, i: (b, jnp.maximum(i * hb - 1, 0), g)),
            pl.BlockSpec((CONV_WIDTH, LANES), lambda b, g, i: (0, g)),
            pl.BlockSpec((1, LANES), lambda b, g, i: (0, g)),
        ],
        out_specs=pl.BlockSpec((1, rows, LANES), lambda b, g, i: (b, i, g)),
        out_shape=jax.ShapeDtypeStruct((bsz, seq, c), F32),
        scratch_shapes=[pltpu.VMEM((rows + CONV_HALO, LANES), F32)],
        compiler_params=_cparams(3),
        name="conv_prompt",
    )(a3, a3, w_dw, b_dw)


def _conv_decode_body(state_ref, a_ref, w_ref, b_ref, o_ref, ext_ref, *, n_state, n_new):
    ext_ref[:, 0:n_state, :] = state_ref[...]
    ext_ref[:, n_state:n_state + n_new, :] = a_ref[...]
    w = w_ref[...]
    for t in range(n_new):
        win = ext_ref[:, t:t + CONV_WIDTH, :]
        o_ref[:, t:t + 1, :] = (jnp.sum(win * w[None], axis=1, keepdims=True)
                                + b_ref[...][None])


def _conv_decode(state, a3, w_dw, b_dw):
    bsz, n_state, c = state.shape
    n_new = a3.shape[1]
    bb = 8
    body = functools.partial(_conv_decode_body, n_state=n_state, n_new=n_new)
    return pl.pallas_call(
        body,
        grid=(bsz // bb,),
        in_specs=[
            pl.BlockSpec((bb, n_state, c), lambda i: (i, 0, 0)),
            pl.BlockSpec((bb, n_new, c), lambda i: (i, 0, 0)),
            pl.BlockSpec((CONV_WIDTH, c), lambda i: (0, 0)),
            pl.BlockSpec((1, c), lambda i: (0, 0)),
        ],
        out_specs=pl.BlockSpec((bb, n_new, c), lambda i: (i, 0, 0)),
        out_shape=jax.ShapeDtypeStruct((bsz, n_new, c), F32),
        scratch_shapes=[pltpu.VMEM((bb, n_state + n_new + 6, c), F32)],
        compiler_params=_cparams(1),
        name="conv_decode",
    )(state, a3, w_dw, b_dw)


def _attn_prompt_body(lq1, lk1, lq2, lk2, subg_ref, q_ref, k_ref, v_ref, o_ref,
                      kt_ref, vb_ref, acc1, acc2, m1, l1, m2, l2, *, blk, sub, n_blk, lam_init):
    i = pl.program_id(2)

    def reset_stats():
        for m_ref, l_ref, acc_ref in ((m1, l1, acc1), (m2, l2, acc2)):
            m_ref[...] = jnp.full(m_ref.shape, NEG_BIG, F32)
            l_ref[...] = jnp.zeros(l_ref.shape, F32)
            acc_ref[...] = jnp.zeros(acc_ref.shape, F32)

    @pl.when(i == 0)
    def _():
        for c in range(n_blk):
            kt_ref[c] = k_ref[0, c * blk:(c + 1) * blk, :].T.astype(BF16)
            vb_ref[c] = v_ref[0, c * blk:(c + 1) * blk, :].astype(BF16)
        reset_stats()

    stats = ((m1, l1, acc1), (m2, l2, acc2))
    n_sub = blk // sub

    def scores(kb, sb, masked):
        r0 = sb * sub
        ncol = r0 + sub if masked else blk
        out = []
        for c in range(2):
            qq = q_ref[0, r0:r0 + sub, c * HEAD_DIM:(c + 1) * HEAD_DIM]
            kk = kt_ref[kb, c * HEAD_DIM:(c + 1) * HEAD_DIM, 0:ncol]
            out.append(jnp.dot(qq, kk, preferred_element_type=F32))
        return out

    def softmax_values(kb, sb, masked, s_pair):
        r0 = sb * sub
        rows = slice(r0, r0 + sub)
        ncol = r0 + sub if masked else blk
        ps, alphas = [], []
        for s, (m_ref, l_ref, _) in zip(s_pair, stats):
            if masked:
                row = lax.broadcasted_iota(I32, (sub, ncol), 0) + r0
                col = lax.broadcasted_iota(I32, (sub, ncol), 1)
                s = jnp.where(col <= row, s, NEG_BIG)
            m_old = m_ref[rows, :]
            m_new = jnp.maximum(m_old, jnp.max(s, axis=-1, keepdims=True))
            alpha = jnp.exp(m_old - m_new)
            p = jnp.exp(s - jnp.concatenate([m_new] * (ncol // LANES), axis=1))
            l_ref[rows, :] = alpha * l_ref[rows, :] + jnp.sum(p, axis=-1, keepdims=True)
            m_ref[rows, :] = m_new
            ps.append(p.astype(BF16))
            alphas.append(alpha)
        pv = jnp.dot(jnp.concatenate(ps, axis=0), vb_ref[kb, 0:ncol, :],
                     preferred_element_type=F32)
        for c, (alpha, (_, _, acc_ref)) in enumerate(zip(alphas, stats)):
            acc_ref[rows, :] = alpha * acc_ref[rows, :] + pv[c * sub:(c + 1) * sub]

    def run(chains):
        ahead = 2
        pending = {n: scores(*chains[n]) for n in range(min(ahead, len(chains)))}
        for n, chain in enumerate(chains):
            if n + ahead < len(chains):
                pending[n + ahead] = scores(*chains[n + ahead])
            softmax_values(*chain, pending.pop(n))

    def block(kb, masked):
        return [(kb, sb, masked) for sb in range(n_sub)]

    def pair_body(t, carry):
        run(block(2 * t, False) + block(2 * t + 1, False))
        return carry

    lax.fori_loop(0, lax.shift_right_logical(i, 1), pair_body, 0)
    odd = (i & 1) == 1

    def finish():
        lam = _lam(lq1, lk1, lq2, lk2, lam_init)
        o = acc1[...] / l1[...] - lam * (acc2[...] / l2[...])
        o_ref[0] = (_rms(o, subg_ref[...]) * (1.0 - lam_init)).astype(BF16)
        reset_stats()

    @pl.when(odd)
    def _():
        run(block(i - 1, False) + block(i, True))
        finish()

    @pl.when(jnp.logical_not(odd))
    def _():
        run(block(i, True))
        finish()


def _attn_prompt(q3, k3, v3, lams, subln_g, lam_init):
    bsz, seq, _ = q3.shape
    blk = ATTN_BLOCK
    n_blk = seq // blk
    vec = pl.BlockSpec((1, HEAD_DIM), lambda b, h, i: (0, 0))
    body = functools.partial(_attn_prompt_body, blk=blk, sub=ATTN_SUB, n_blk=n_blk,
                             lam_init=lam_init)
    return pl.pallas_call(
        body,
        grid=(bsz, N_HEADS, n_blk),
        in_specs=[
            vec, vec, vec, vec,
            pl.BlockSpec((1, HEAD_W), lambda b, h, i: (0, 0)),
            pl.BlockSpec((1, blk, HEAD_W), lambda b, h, i: (b, i, h)),
            pl.BlockSpec((1, seq, HEAD_W), lambda b, h, i: (b, 0, h)),
            pl.BlockSpec((1, seq, HEAD_W), lambda b, h, i: (b, 0, h)),
        ],
        out_specs=pl.BlockSpec((1, blk, HEAD_W), lambda b, h, i: (b, i, h)),
        out_shape=jax.ShapeDtypeStruct(q3.shape, BF16),
        scratch_shapes=[
            pltpu.VMEM((n_blk, HEAD_W, blk), BF16),
            pltpu.VMEM((n_blk, blk, HEAD_W), BF16),
            pltpu.VMEM((blk, HEAD_W), F32),
            pltpu.VMEM((blk, HEAD_W), F32),
            pltpu.VMEM((blk, LANES), F32),
            pltpu.VMEM((blk, LANES), F32),
            pltpu.VMEM((blk, LANES), F32),
            pltpu.VMEM((blk, LANES), F32),
        ],
        compiler_params=_cparams(3),
        name="attn_prompt",
    )(*lams, subln_g, q3, k3, v3)


NEW_PAD = 16
Q_PAD = 8
DECODE_PAGE_GROUP = 4


def _attn_decode_body(pt_ref, lq1, lk1, lq2, lk2, subg_ref, q_ref, kn_ref, vn_ref, ck_hbm,
                      cv_hbm, o_ref, kbuf, vbuf, s_ref, stage, sem, *, n_pages, n_new,
                      lam_init):
    b = pl.program_id(0)
    buf = lax.rem(b, 2)
    n_past = n_pages * PAGE_SIZE
    width = stage.shape[1]
    page_rows = PAGE_SIZE * N_HEADS

    def fetch(seq, half):
        def one_page(j, carry):
            src = pl.ds(pl.multiple_of(pt_ref[seq, j] * page_rows, page_rows), page_rows)
            dst = pl.ds(pl.multiple_of(j * page_rows, page_rows), page_rows)
            pltpu.make_async_copy(ck_hbm.at[src], kbuf.at[half, dst], sem.at[0, half]).start()
            pltpu.make_async_copy(cv_hbm.at[src], vbuf.at[half, dst], sem.at[1, half]).start()
            return carry
        lax.fori_loop(0, n_pages, one_page, 0)

    def wait_pages(hbm, dst_buf, which):
        pltpu.make_async_copy(hbm.at[pl.ds(0, n_pages * page_rows)], dst_buf.at[buf],
                              sem.at[which, buf]).wait()

    @pl.when(b == 0)
    def _():
        fetch(0, 0)

    @pl.when(b + 1 < pl.num_programs(0))
    def _():
        fetch(b + 1, 1 - buf)

    k_now = kbuf.at[buf]
    v_now = vbuf.at[buf]

    def padded_rows(rows_f32):
        stage[...] = jnp.zeros(stage.shape, F32)
        stage[0:n_new, :] = rows_f32
        return stage[...].astype(BF16)

    q16 = padded_rows(q_ref[0])
    sel_r = lax.broadcasted_iota(I32, (NEW_PAD, LANES), 0)
    sel_c = lax.broadcasted_iota(I32, (NEW_PAD, LANES), 1)
    sel = (sel_c % Q_PAD == sel_r).astype(BF16)
    qrep = lax.dot_general(q16, sel, (((0,), (0,)), ((), ())),
                           preferred_element_type=F32)
    rr = lax.broadcasted_iota(I32, (width, LANES), 0)
    cc = lax.broadcasted_iota(I32, (width, LANES), 1)
    qblk = jnp.where(rr // HEAD_DIM == cc // Q_PAD, qrep, 0.0).astype(BF16)

    def head_major(pages_ref, j):
        return jnp.concatenate(
            [pages_ref[pl.ds(j * page_rows + h, PAGE_SIZE, stride=N_HEADS), :].astype(BF16)
             for h in range(N_HEADS)], axis=-1)

    groups = [range(j0, min(j0 + DECODE_PAGE_GROUP, n_pages))
              for j0 in range(0, n_pages, DECODE_PAGE_GROUP)]
    row_slices = [slice(g[0] * PAGE_SIZE, (g[-1] + 1) * PAGE_SIZE) for g in groups]

    wait_pages(ck_hbm, kbuf, 0)
    for g, rows in zip(groups, row_slices):
        keys = jnp.concatenate([head_major(k_now, j) for j in g], axis=0)
        s_ref[rows, :] = jnp.dot(keys, qblk, preferred_element_type=F32)
    s_new = jnp.dot(padded_rows(kn_ref[0]), qblk, preferred_element_type=F32)
    new_idx = lax.broadcasted_iota(I32, (NEW_PAD, LANES), 0)
    slot = lax.broadcasted_iota(I32, (NEW_PAD, LANES), 1) % Q_PAD
    s_new = jnp.where((new_idx <= slot) & (new_idx < n_new), s_new, NEG_BIG)
    m = jnp.maximum(jnp.max(s_ref[0:n_past, :], axis=0, keepdims=True),
                    jnp.max(s_new, axis=0, keepdims=True))

    contract0 = (((0,), (0,)), ((), ()))
    e_new = jnp.exp(s_new - m)
    denom = jnp.sum(e_new, axis=0, keepdims=True)
    acc = lax.dot_general(e_new.astype(BF16), padded_rows(vn_ref[0]), contract0,
                          preferred_element_type=F32)
    wait_pages(cv_hbm, vbuf, 1)
    for g, rows in zip(groups, row_slices):
        e = jnp.exp(s_ref[rows, :] - m)
        denom = denom + jnp.sum(e, axis=0, keepdims=True)
        vals = jnp.concatenate([head_major(v_now, j) for j in g], axis=0)
        acc = acc + lax.dot_general(e.astype(BF16), vals, contract0,
                                    preferred_element_type=F32)

    r_i = lax.broadcasted_iota(I32, (LANES, LANES), 0)
    c_i = lax.broadcasted_iota(I32, (LANES, LANES), 1)
    denom_rows = jnp.sum(jnp.where(r_i == c_i, jnp.broadcast_to(denom, (LANES, LANES)), 0.0),
                         axis=1, keepdims=True)
    o_norm = acc / denom_rows
    lam = _lam(lq1, lk1, lq2, lk2, lam_init)
    outs = []
    for h in range(N_HEADS):
        r0 = h * 2 * Q_PAD
        cols = slice(h * HEAD_W, (h + 1) * HEAD_W)
        oh = o_norm[r0:r0 + Q_PAD, cols] - lam * o_norm[r0 + Q_PAD:r0 + 2 * Q_PAD, cols]
        outs.append(_rms(oh, subg_ref[...]) * (1.0 - lam_init))
    o_ref[0] = jnp.concatenate(outs, axis=-1)[:n_new]


def _attn_decode(page_table, q3, kn3, vn3, cache_k, cache_v, lams, subln_g, lam_init):
    bsz, n_new, width = q3.shape
    n_pages = page_table.shape[1]
    vec = pl.BlockSpec((1, HEAD_DIM), lambda b, pt: (0, 0))
    per_b = pl.BlockSpec((1, n_new, width), lambda b, pt: (b, 0, 0))

    body = functools.partial(_attn_decode_body, n_pages=n_pages, n_new=n_new,
                             lam_init=lam_init)
    hbm = pl.BlockSpec(memory_space=pl.ANY)
    past_rows = n_pages * PAGE_SIZE * N_HEADS
    grid_spec = pltpu.PrefetchScalarGridSpec(
        num_scalar_prefetch=1,
        grid=(bsz,),
        in_specs=[vec, vec, vec, vec, pl.BlockSpec((1, HEAD_W), lambda b, pt: (0, 0)),
                  per_b, per_b, per_b, hbm, hbm],
        out_specs=per_b,
        scratch_shapes=[pltpu.VMEM((2, past_rows, HEAD_W), F32),
                        pltpu.VMEM((2, past_rows, HEAD_W), F32),
                        pltpu.VMEM((n_pages * PAGE_SIZE, LANES), F32),
                        pltpu.VMEM((NEW_PAD, width), F32),
                        pltpu.SemaphoreType.DMA((2, 2))],
    )
    return pl.pallas_call(
        body,
        grid_spec=grid_spec,
        out_shape=jax.ShapeDtypeStruct(q3.shape, F32),
        compiler_params=_cparams(1),
        name="attn_decode",
    )(page_table, *lams, subln_g, q3, kn3, vn3, cache_k, cache_v)


def _post_body(x_ref, conv_ref, o_ref, gate_ref, lng, lnb, wc, wa, wo, gffn, wr, br, *rest,
               d_model):
    x1_ref, hp_ref, ids_ref, tw_ref = rest[-4:]
    n_groups = 2
    rows_per = x_ref.shape[0] // n_groups
    groups = [slice(g * rows_per, (g + 1) * rows_per) for g in range(n_groups)]
    attn_outs = [jnp.dot(o_ref[r, :], wa[...], preferred_element_type=F32) for r in groups]
    conv_outs = []
    for r in groups:
        c = conv_ref[r, :]
        mu = jnp.mean(c, axis=-1, keepdims=True)
        xc = c - mu
        cn = (xc * lax.rsqrt(jnp.mean(xc * xc, axis=-1, keepdims=True) + LN_EPS) * lng[...]
              + lnb[...])
        cact = (cn * jax.nn.sigmoid(cn)).astype(BF16)
        conv_outs.append(jnp.dot(cact, wc[...], preferred_element_type=F32))
    hbs = []
    for r, conv_out, attn_out in zip(groups, conv_outs, attn_outs):
        merged = (gate_ref[r, :d_model] * conv_out
                  + gate_ref[r, d_model:] * attn_out).astype(BF16)
        x1 = x_ref[r, :] + jnp.dot(merged, wo[...], preferred_element_type=F32)
        x1_ref[r, :] = x1
        hbs.append(_rms(x1, gffn[...]).astype(BF16))
    for r, hb in zip(groups, hbs):
        logits = jnp.dot(hb, wr[...], preferred_element_type=F32) + br[...]
        lane = lax.broadcasted_iota(I32, logits.shape, 1)
        vals, ids = [], []
        cur = logits
        for _ in range(TOP_K):
            mx = jnp.max(cur, axis=-1, keepdims=True)
            idx = jnp.min(jnp.where(cur == mx, lane, N_EXPERTS), axis=-1, keepdims=True)
            vals.append(mx)
            ids.append(idx)
            cur = jnp.where(lane == idx, -jnp.inf, cur)
        es = [jnp.exp(v - vals[0]) for v in vals]
        den = es[0] + es[1] + es[2] + es[3]
        tw_ref[r, :] = _columns([e / den for e in es])
        ids_ref[r, :] = _columns(ids)

        bits = pltpu.bitcast(hb.astype(F32), U32)
        half = d_model // 2
        words = (bits[:, :half] >> 16) | (bits[:, half:] & jnp.uint32(0xFFFF0000))
        pieces = half // LANES
        for c in range(pieces):
            hp_ref[pl.ds(r.start * pieces + c, rows_per, stride=pieces), :] = (
                words[:, c * LANES:(c + 1) * LANES])


def _post(x, conv, o, gate, ln_g, ln_b, wc, wa, wo, g_ffn, wr, br, tokens, first_token):
    t, d = x.shape
    tm = ROW_TILE
    first_tile = first_token // tm
    row = lambda i: (i, 0)
    const = lambda i: (0, 0)
    mat = pl.BlockSpec((d, d), const)
    vec = pl.BlockSpec((1, d), const)
    body = functools.partial(_post_body, d_model=d)
    in_specs = [
        pl.BlockSpec((tm, d), row), pl.BlockSpec((tm, d), row), pl.BlockSpec((tm, d), row),
        pl.BlockSpec((tm, 2 * d), row),
        vec, vec, mat, mat, mat, vec,
        pl.BlockSpec((d, N_EXPERTS), const), pl.BlockSpec((1, N_EXPERTS), const),
    ]
    in_specs.append(pl.BlockSpec(memory_space=pl.ANY))
    args = [x, conv, o, gate, ln_g, ln_b, wc, wa, wo, g_ffn, wr, br, tokens]
    return pl.pallas_call(
        body,
        grid=(t // tm,),
        in_specs=in_specs,
        out_specs=[
            pl.BlockSpec((tm, d), row),
            pl.BlockSpec((tm * (d // 2 // LANES), LANES), lambda i: (first_tile + i, 0)),
            pl.BlockSpec((tm, TOP_K), row), pl.BlockSpec((tm, TOP_K), row),
        ],
        out_shape=[
            jax.ShapeDtypeStruct((t, d), F32),
            jax.ShapeDtypeStruct(tokens.shape, U32),
            jax.ShapeDtypeStruct((t, TOP_K), I32), jax.ShapeDtypeStruct((t, TOP_K), F32),
        ],
        input_output_aliases={len(args) - 1: 1},
        compiler_params=_cparams(1),
        name="post",
    )(*args)


def _lane_cumsum(x):
    lane = lax.broadcasted_iota(I32, x.shape, 1)
    s = 1
    while s < LANES:
        x = x + jnp.where(lane >= s, pltpu.roll(x, s, 1), 0.0)
        s *= 2
    return x


def _wrap_i32(value):
    return (value + 2 ** 31) % 2 ** 32 - 2 ** 31


def _route_word(token, slot):
    dst = ((token // COMBINE_TILE) * (TOP_K * COMBINE_TILE) + slot * COMBINE_TILE
           + token % COMBINE_TILE)
    return jnp.left_shift(dst, ROUTE_TOKEN_BITS) | token


def _positions_body(ids_ref, pos_ref, word_ref, te_ref, count_ref, start_ref, *, n_tiles_pad):
    p = pl.program_id(0)
    i = pl.program_id(1)
    ids = ids_ref[...]
    tt = ids.shape[0]
    lane = lax.broadcasted_iota(I32, (tt, LANES), 1)
    onehots = [ids[:, k:k + 1] == lane for k in range(TOP_K)]
    tile_counts = [jnp.sum(oh.astype(F32), axis=0, keepdims=True) for oh in onehots]
    tile_total = tile_counts[0] + tile_counts[1] + tile_counts[2] + tile_counts[3]

    @pl.when((p == 0) & (i == 0))
    def _():
        count_ref[...] = jnp.zeros(count_ref.shape, F32)

    @pl.when((p == 1) & (i == 0))
    def _():
        counts = count_ref[...]
        padded = jnp.ceil(counts * (1.0 / MOE_TILE)) * MOE_TILE
        ends = _lane_cumsum(padded)
        start_ref[...] = ends - padded
        count_ref[...] = jnp.zeros(count_ref.shape, F32)
        tile_start = (lax.broadcasted_iota(I32, (n_tiles_pad, LANES), 0) * MOE_TILE).astype(F32)
        elane = lax.broadcasted_iota(I32, (n_tiles_pad, LANES), 1)
        done = (ends[0:1, :] <= tile_start) & (elane < N_EXPERTS)
        n_done = jnp.sum(done.astype(F32), axis=-1, keepdims=True)
        te_ref[...] = jnp.broadcast_to(n_done, (n_tiles_pad, LANES)).astype(I32)

    @pl.when(p == 1)
    def _():
        r = lax.broadcasted_iota(I32, (tt, tt), 0)
        c = lax.broadcasted_iota(I32, (tt, tt), 1)
        earlier = (c < r).astype(BF16)
        run = start_ref[0:1, :] + count_ref[0:1, :]
        cols = []
        for k in range(TOP_K):
            within = jnp.dot(earlier, onehots[k].astype(BF16), preferred_element_type=F32)
            cols.append(jnp.sum(jnp.where(onehots[k], within + run, 0.0),
                                axis=-1, keepdims=True))
            run = run + tile_counts[k]
        pos_ref[...] = _columns(cols).astype(I32)
        token = lax.broadcasted_iota(I32, (tt, TOP_K), 0) + i * tt
        slot = lax.broadcasted_iota(I32, (tt, TOP_K), 1)
        word_ref[...] = _route_word(token, slot)

    count_ref[...] = count_ref[...] + tile_total


def _positions(ids, n_tiles_pad):
    t = ids.shape[0]
    tt = POS_TILE
    body = functools.partial(_positions_body, n_tiles_pad=n_tiles_pad)
    return pl.pallas_call(
        body,
        grid=(2, t // tt),
        in_specs=[pl.BlockSpec((tt, TOP_K), lambda p, i: (i, 0))],
        out_specs=[
            pl.BlockSpec((tt, TOP_K), lambda p, i: (i * p, 0)),
            pl.BlockSpec((tt, TOP_K), lambda p, i: (i * p, 0)),
            pl.BlockSpec((n_tiles_pad, LANES), lambda p, i: (0, 0)),
        ],
        out_shape=[
            jax.ShapeDtypeStruct((t, TOP_K), I32),
            jax.ShapeDtypeStruct((t, TOP_K), I32),
            jax.ShapeDtypeStruct((n_tiles_pad, LANES), I32),
        ],
        scratch_shapes=[pltpu.VMEM((8, LANES), F32), pltpu.VMEM((8, LANES), F32)],
        compiler_params=_cparams(2),
        name="moe_positions",
    )(ids)


def _inverse_body(pos_ref, word_ref, init_hbm, inv_hbm, inv_smem, sem):
    c = pl.program_id(0)

    @pl.when(c == 0)
    def _():
        load = pltpu.make_async_copy(init_hbm, inv_smem, sem)
        load.start()
        load.wait()

    def place(a, carry):
        inv_smem[pos_ref[a]] = word_ref[a]
        return carry

    lax.fori_loop(0, pos_ref.shape[0], place, 0, unroll=32)

    @pl.when(c == pl.num_programs(0) - 1)
    def _():
        store = pltpu.make_async_copy(inv_smem, inv_hbm, sem)
        store.start()
        store.wait()


def _inverse(pos_flat, word_flat, init):
    n = pos_flat.shape[0]
    chunk = INVERSE_CHUNK
    assert n % chunk == 0
    smem_chunk = pl.BlockSpec((chunk,), lambda c: (c,), memory_space=pltpu.SMEM)
    return pl.pallas_call(
        _inverse_body,
        grid=(n // chunk,),
        in_specs=[smem_chunk, smem_chunk, pl.BlockSpec(memory_space=pl.ANY)],
        out_specs=pl.BlockSpec(memory_space=pl.ANY),
        out_shape=jax.ShapeDtypeStruct(init.shape, I32),
        scratch_shapes=[pltpu.SMEM(init.shape, I32), pltpu.SemaphoreType.DMA],
        compiler_params=_cparams(1),
        name="moe_inverse",
    )(pos_flat, word_flat, init)


def _experts_body(te_ref, nv_ref, inv_next, inv_prev, w1_ref, b1_ref, w2_ref, b2_ref, tok_hbm,
                  out_hbm, xs_buf, y_buf, act_buf, w1b, w2b, gsem, ssem, *, n_tok, n_out):
    j = pl.program_id(0)
    nv = nv_ref[0]
    d_ff, d = w2b.shape
    half = d // 2
    xp, yp = half // LANES, d // LANES
    tm = xs_buf.shape[1] // xp
    slot = lax.rem(j, 2)
    other = 1 - slot
    token_mask = (1 << ROUTE_TOKEN_BITS) - 1

    def gather_row(word, r, buf_slot):
        tok = pl.multiple_of((word & token_mask) * xp, xp)
        return pltpu.make_async_copy(tok_hbm.at[pl.ds(tok, xp)],
                                     xs_buf.at[buf_slot, pl.ds(r * xp, xp)], gsem)

    def scatter_row(word, r, buf_slot):
        dst = pl.multiple_of(lax.shift_right_logical(word, ROUTE_TOKEN_BITS) * yp, yp)
        return pltpu.make_async_copy(y_buf.at[buf_slot, pl.ds(r * yp, yp)],
                                     out_hbm.at[pl.ds(dst, yp)], ssem.at[buf_slot])

    def spare_word(r):
        return _wrap_i32((n_out + r) << ROUTE_TOKEN_BITS)

    def wait_gather(buf_slot):
        pltpu.make_async_copy(tok_hbm.at[pl.ds(0, tm * xp)], xs_buf.at[buf_slot], gsem).wait()

    def wait_scatter(buf_slot):
        pltpu.make_async_copy(y_buf.at[buf_slot], out_hbm.at[pl.ds(0, tm * yp)],
                              ssem.at[buf_slot]).wait()

    @pl.when(j == 0)
    def _():
        y_buf[...] = jnp.zeros(y_buf.shape, F32)
        for r in range(tm):
            gather_row(inv_prev[0, 0, r], r, 0).start()
            scatter_row(spare_word(r), r, 0).start()

    @pl.when(j < nv)
    def _():
        wait_gather(slot)

        @pl.when((j == 0) | (te_ref[j] != te_ref[jnp.maximum(j - 1, 0)]))
        def _():
            for c in range(0, d, LANES):
                w1b[c:c + LANES, :] = w1_ref[0, c:c + LANES, :].astype(BF16)
            for c in range(0, d_ff, LANES):
                w2b[c:c + LANES, :] = w2_ref[0, c:c + LANES, :].astype(BF16)

        has_prev = j > 0

        def gather_rows(r0, r1):
            for r in range(r0, r1):
                gather_row(inv_next[0, 0, r], r, other).start(priority=1)

        def scatter_rows(r0, r1):
            for r in range(r0, r1):
                word = jnp.where(has_prev, inv_prev[0, 0, r], spare_word(r))
                scatter_row(word, r, other).start()

        xs_now = xs_buf.at[slot]
        bits = jnp.concatenate(
            [xs_now[pl.ds(c, tm, stride=xp), :] for c in range(xp)], axis=-1)
        x_lo = pltpu.bitcast(bits << 16, F32).astype(BF16)
        x_hi = pltpu.bitcast(bits & jnp.uint32(0xFFFF0000), F32).astype(BF16)
        y_now = y_buf.at[slot]
        ch = EXPERT_CHUNK

        def proj1(c0):
            return (jnp.dot(x_lo, w1b[:half, c0:c0 + ch], preferred_element_type=F32)
                    + jnp.dot(x_hi, w1b[half:, c0:c0 + ch], preferred_element_type=F32)
                    + b1_ref[0, :, c0:c0 + ch])

        rows_1 = tm // (d_ff // ch)
        for n, c in enumerate(range(0, d_ff, ch)):
            gate = jnp.minimum(proj1(c), SWIGLU_LIMIT)
            lin = jnp.clip(proj1(d_ff + c), -SWIGLU_LIMIT, SWIGLU_LIMIT)
            act_buf[:, c:c + ch] = (gate * jax.nn.sigmoid(SWIGLU_ALPHA * gate)
                                    * (lin + 1.0)).astype(BF16)
            gather_rows(n * rows_1, (n + 1) * rows_1)
        wait_scatter(slot)
        rows_2 = tm // (d // ch)
        for n, c in enumerate(range(0, d, ch)):
            y_cols = (jnp.dot(act_buf[...], w2b[:, c:c + ch], preferred_element_type=F32)
                      + b2_ref[0, :, c:c + ch])
            for g in range(ch // LANES):
                y_now[pl.ds(c // LANES + g, tm, stride=yp), :] = (
                    y_cols[:, g * LANES:(g + 1) * LANES])
            scatter_rows(n * rows_2, (n + 1) * rows_2)

    @pl.when(j == nv)
    def _():
        wait_gather(slot)
        wait_scatter(slot)
        for r in range(tm):
            scatter_row(inv_prev[0, 0, r], r, other).start()
        wait_scatter(other)


def _experts(tile_expert, n_valid, inv3, tokens, w1, b1, w2, b2, n_tok, n_out):
    nt, _, tm = inv3.shape
    d_ff, d = w2.shape[1:]
    half = d // 2
    xp, yp = half // LANES, d // LANES
    body = functools.partial(_experts_body, n_tok=n_tok, n_out=n_out)
    smem_tile = lambda index_map: pl.BlockSpec((1, 1, tm), index_map, memory_space=pltpu.SMEM)
    grid_spec = pltpu.PrefetchScalarGridSpec(
        num_scalar_prefetch=2,
        grid=(nt,),
        in_specs=[
            smem_tile(lambda j, te, nv: (jnp.minimum(j + 1, nt - 1), 0, 0)),
            smem_tile(lambda j, te, nv: (jnp.maximum(j - 1, 0), 0, 0)),
            pl.BlockSpec((1, d, 2 * d_ff), lambda j, te, nv: (te[j], 0, 0)),
            pl.BlockSpec((1, 1, 2 * d_ff), lambda j, te, nv: (te[j], 0, 0)),
            pl.BlockSpec((1, d_ff, d), lambda j, te, nv: (te[j], 0, 0)),
            pl.BlockSpec((1, 1, d), lambda j, te, nv: (te[j], 0, 0)),
            pl.BlockSpec(memory_space=pl.ANY),
        ],
        out_specs=pl.BlockSpec(memory_space=pl.ANY),
        scratch_shapes=[
            pltpu.VMEM((2, tm * xp, LANES), U32),
            pltpu.VMEM((2, tm * yp, LANES), F32),
            pltpu.VMEM((tm, d_ff), BF16),
            pltpu.VMEM((d, 2 * d_ff), BF16),
            pltpu.VMEM((d_ff, d), BF16),
            pltpu.SemaphoreType.DMA,
            pltpu.SemaphoreType.DMA((2,)),
        ],
    )
    return pl.pallas_call(
        body,
        grid_spec=grid_spec,
        out_shape=jax.ShapeDtypeStruct(((n_out + tm) * yp, LANES), F32),
        compiler_params=_cparams(1),
        name="moe_experts",
    )(tile_expert, n_valid, inv3, inv3, w1, b1, w2, b2, tokens)


def _combine_body(tw_ref, x1_ref, gf_ref, rows_ref, y_ref):
    tt, d = x1_ref.shape
    yp = d // LANES
    tw = tw_ref[...]
    cols = []
    for c in range(yp):
        acc = None
        for k in range(TOP_K):
            piece = rows_ref[pl.ds(k * tt * yp + c, tt, stride=yp), :]
            term = tw[:, k:k + 1] * piece
            acc = term if acc is None else acc + term
        cols.append(acc)
    y_ref[...] = _rms(x1_ref[...] + jnp.concatenate(cols, axis=-1), gf_ref[...])


def _combine(tw, x1, g_final, expert_rows, first_tile):
    t, d = x1.shape
    tt = COMBINE_TILE
    return pl.pallas_call(
        _combine_body,
        grid=(t // tt,),
        in_specs=[
            pl.BlockSpec((tt, TOP_K), lambda i: (i, 0)),
            pl.BlockSpec((tt, d), lambda i: (i, 0)),
            pl.BlockSpec((1, d), lambda i: (0, 0)),
            pl.BlockSpec((TOP_K * tt * (d // LANES), LANES), lambda i: (first_tile + i, 0)),
        ],
        out_specs=pl.BlockSpec((tt, d), lambda i: (i, 0)),
        out_shape=jax.ShapeDtypeStruct((t, d), F32),
        compiler_params=_cparams(1),
        name="moe_combine",
    )(tw, x1, g_final, expert_rows)


def kernel(x_prompt, x_sample, cache_k, cache_v, state_conv, page_table, g_mix, w_in, b_in,
           w_dw, b_dw, ln_g, ln_b, w_conv_out, lam_q1, lam_k1, lam_q2, lam_k2, subln_g,
           w_attn_out, w_o, g_ffn, w_router, b_router, w_moe1, b_moe1, w_moe2, b_moe2, g_final):
    bsz, seq, d = x_prompt.shape
    dec_b, dec_s, _ = x_sample.shape
    depth = g_mix.shape[0]
    c_conv = w_dw.shape[2]
    attn_w = N_HEADS * HEAD_W
    n_pages = page_table.shape[1]
    past_len = n_pages * PAGE_SIZE
    t_p, t_s = bsz * seq, dec_b * dec_s
    t_all = t_p + t_s
    d_ff = w_moe2.shape[2]
    assert depth == 1, "the combine kernel fuses the final norm, so only one layer is supported"
    assert seq % ROW_TILE == 0 and t_s % ROW_TILE == 0 and seq % CONV_ROWS == 0
    assert seq % ATTN_BLOCK == 0 and t_all % POS_TILE == 0 and dec_b % 8 == 0
    assert t_p % COMBINE_TILE == 0 and t_s % COMBINE_TILE == 0

    tab_p = _rope_tables(jnp.arange(seq))
    tab_s = _rope_tables(jnp.tile(past_len + jnp.arange(dec_s), dec_b))
    n_rows = t_all * TOP_K + N_EXPERTS * MOE_TILE
    n_tiles = n_rows // MOE_TILE
    row2 = lambda v: v.reshape(1, -1)

    hp, hs = x_prompt.reshape(t_p, d), x_sample.reshape(t_s, d)
    outs = [[] for _ in range(6)]
    for l in range(depth):
        lam_init = 0.8 - 0.6 * math.exp(-0.3 * l)
        lams = (row2(lam_q1[l]), row2(lam_k1[l]), row2(lam_q2[l]), row2(lam_k2[l]))
        subg = row2(subln_g[l])
        w_in_bf = w_in[l].astype(BF16)
        wc, wa, wo = (w_conv_out[l].astype(BF16), w_attn_out[l].astype(BF16),
                      w_o[l].astype(BF16))
        wr = w_router[l].astype(BF16)
        b1 = b_moe1[l].reshape(N_EXPERTS, 1, 2 * d_ff)
        b2 = b_moe2[l].reshape(N_EXPERTS, 1, d)
        proj_args = (row2(g_mix[l]), w_in_bf, row2(b_in[l]))
        post_args = (row2(ln_g[l]), row2(ln_b[l]), wc, wa, wo, row2(g_ffn[l]), wr,
                     row2(b_router[l]))

        a_p, q_p, k_p, v_p, gate_p = _in_proj(hp, *proj_args, tab_p, seq // ROW_TILE,
                                              c_conv, attn_w)
        a_p3 = a_p.reshape(bsz, seq, c_conv)
        conv_p = _conv_prompt(a_p3, w_dw[l], row2(b_dw[l]))
        o_p = _attn_prompt(q_p.reshape(bsz, seq, attn_w), k_p.reshape(bsz, seq, attn_w),
                           v_p.reshape(bsz, seq, attn_w), lams, subg, lam_init)
        tokens = jnp.zeros(((t_all + PAD_TOKENS) * (d // 2 // LANES), LANES), U32)
        x1_p, tokens, ids_p, tw_p = _post(hp, conv_p.reshape(t_p, c_conv),
                                          o_p.reshape(t_p, attn_w), gate_p, *post_args,
                                          tokens, 0)

        a_s, q_s, k_s, v_s, gate_s = _in_proj(hs, *proj_args, tab_s, 1, c_conv, attn_w)
        a_s3 = a_s.reshape(dec_b, dec_s, c_conv)
        conv_s = _conv_decode(state_conv[l], a_s3, w_dw[l], row2(b_dw[l]))
        pool = cache_k.shape[1]
        o_s = _attn_decode(page_table, q_s.astype(F32).reshape(dec_b, dec_s, attn_w),
                           k_s.reshape(dec_b, dec_s, attn_w), v_s.reshape(dec_b, dec_s, attn_w),
                           cache_k[l].reshape(pool * PAGE_SIZE * N_HEADS, HEAD_W),
                           cache_v[l].reshape(pool * PAGE_SIZE * N_HEADS, HEAD_W),
                           lams, subg, lam_init)
        x1_s, tokens, ids_s, tw_s = _post(hs, conv_s.reshape(t_s, c_conv),
                                          o_s.reshape(t_s, attn_w).astype(BF16), gate_s,
                                          *post_args, tokens, t_p)

        ids = jnp.concatenate([ids_p, ids_s], axis=0)
        pos, word, te = _positions(ids, n_tiles)
        tile_expert = jnp.minimum(te[:, 0], N_EXPERTS - 1)
        n_valid = jnp.sum((te[:, 0] < N_EXPERTS).astype(I32)).reshape(1)
        slot_idx = jnp.arange(n_rows, dtype=U32)
        n_out = t_all * TOP_K
        unused = (((n_out + slot_idx % MOE_TILE) << ROUTE_TOKEN_BITS)
                  | (t_all + slot_idx % PAD_TOKENS))
        inv = _inverse(pos.reshape(-1), word.reshape(-1), lax.bitcast_convert_type(unused, I32))
        expert_rows = _experts(tile_expert, n_valid, inv.reshape(n_tiles, 1, MOE_TILE), tokens,
                               w_moe1[l], b1, w_moe2[l], b2, t_all, n_out)
        gf = row2(g_final)
        hp = _combine(tw_p, x1_p, gf, expert_rows, 0)
        hs = _combine(tw_s, x1_s, gf, expert_rows, t_p // COMBINE_TILE)

        outs[0].append(k_p.reshape(bsz, seq, N_HEADS, HEAD_W))
        outs[1].append(v_p.reshape(bsz, seq, N_HEADS, HEAD_W))
        outs[2].append(a_p3[:, seq - (CONV_WIDTH - 1):])
        outs[3].append(k_s.reshape(dec_b, dec_s, N_HEADS, HEAD_W))
        outs[4].append(v_s.reshape(dec_b, dec_s, N_HEADS, HEAD_W))
        outs[5].append(jnp.concatenate([state_conv[l], a_s3], axis=1)[:, -(CONV_WIDTH - 1):])

    y_prompt = hp.reshape(bsz, seq, d)
    y_sample = hs.reshape(dec_b, dec_s, d)
    return (y_prompt, y_sample) + tuple(jnp.stack(o) for o in outs)
```

```python
import functools
import math

import jax
import jax.numpy as jnp
from jax import lax
from jax.experimental import pallas as pl
from jax.experimental.pallas import tpu as pltpu

F32 = jnp.float32
BF16 = jnp.bfloat16
I32 = jnp.int32
U32 = jnp.uint32

N_HEADS = 8
HEAD_DIM = 64
HEAD_W = 2 * HEAD_DIM
ROT_DIM = HEAD_DIM // 4
ROT_HALF = ROT_DIM // 2
ROPE_THETA = 500000.0
CONV_WIDTH = 31
CONV_HALO = 32
N_EXPERTS = 32
TOP_K = 4
SWIGLU_ALPHA = 1.702
SWIGLU_LIMIT = 7.0
RMS_EPS = 1e-5
LN_EPS = 1e-5
PAGE_SIZE = 128
LANES = 128
NEG_BIG = -1e30

ROW_TILE = 512
ATTN_BLOCK = 512
ATTN_SUB = 128
CONV_ROWS = 1024
CONV_CHUNK = 64
MOE_TILE = 256
EXPERT_CHUNK = 256
POS_TILE = 512
COMBINE_TILE = 256
ROUTE_TOKEN_BITS = 15
INVERSE_CHUNK = 11 * 1024
PAD_TOKENS = 64
VMEM_LIMIT = 56 * 1024 * 1024


def _cparams(n_axes, vmem=VMEM_LIMIT):
    return pltpu.CompilerParams(dimension_semantics=("arbitrary",) * n_axes,
                                vmem_limit_bytes=vmem)


def _rms(x, g):
    return x * lax.rsqrt(jnp.mean(x * x, axis=-1, keepdims=True) + RMS_EPS) * g


def _columns(cols):
    rows = cols[0].shape[0]
    lane = lax.broadcasted_iota(I32, (rows, len(cols)), 1)
    out = jnp.zeros((rows, len(cols)), cols[0].dtype)
    for k, col in enumerate(cols):
        out = jnp.where(lane == k, col, out)
    return out


def _lam(lq1, lk1, lq2, lk2, lam_init):
    s1 = jnp.sum(lq1[...] * lk1[...], axis=-1, keepdims=True)
    s2 = jnp.sum(lq2[...] * lk2[...], axis=-1, keepdims=True)
    return jnp.exp(s1) - jnp.exp(s2) + lam_init


def _rope_tables(pos):
    inv = jnp.power(jnp.float32(ROPE_THETA),
                    -jnp.arange(ROT_HALF, dtype=F32) * (2.0 / ROT_DIM))
    ang = pos.astype(F32)[:, None] * inv[None, :]
    cos, sin = jnp.cos(ang), jnp.sin(ang)
    n = pos.shape[0]
    rest = HEAD_DIM - ROT_DIM
    zh = jnp.zeros((n, ROT_HALF), F32)
    c64 = jnp.concatenate([cos, cos, jnp.ones((n, rest), F32)], axis=-1)
    sa64 = jnp.concatenate([-sin, zh, jnp.zeros((n, rest), F32)], axis=-1)
    sb64 = jnp.concatenate([zh, sin, jnp.zeros((n, rest), F32)], axis=-1)
    tile = lambda t: jnp.concatenate([t, t], axis=-1)
    return tile(c64), tile(sa64), tile(sb64)


def _in_proj_body(x_ref, g_ref, w_ref, b_ref, cos_ref, sa_ref, sb_ref,
                  a_ref, q_ref, k_ref, v_ref, gate_ref, *, c_conv, attn_w, d_model):
    h = _rms(x_ref[...], g_ref[...]).astype(BF16)
    cos, sa, sb = cos_ref[...], sa_ref[...], sb_ref[...]
    ch = 512

    def proj(c0):
        return (jnp.dot(h, w_ref[:, c0:c0 + ch], preferred_element_type=F32)
                + b_ref[:, c0:c0 + ch])

    def rope(z):
        outs = []
        for j in range(ch // LANES):
            zj = z[:, j * LANES:(j + 1) * LANES]
            outs.append(zj * cos + pltpu.roll(zj, LANES - ROT_HALF, 1) * sa
                        + pltpu.roll(zj, ROT_HALF, 1) * sb)
        return jnp.concatenate(outs, axis=-1)

    for c in range(0, c_conv, ch):
        a_ref[:, c:c + ch] = proj(c) * jax.nn.sigmoid(proj(c_conv + c))
    base = 2 * c_conv
    for c in range(0, attn_w, ch):
        q_ref[:, c:c + ch] = (rope(proj(base + c)) * (HEAD_DIM ** -0.5)).astype(BF16)
    base += attn_w
    for c in range(0, attn_w, ch):
        k_ref[:, c:c + ch] = rope(proj(base + c))
    base += attn_w
    for c in range(0, attn_w, ch):
        v_ref[:, c:c + ch] = proj(base + c)
    base += attn_w
    for c in range(0, 2 * d_model, ch):
        gate_ref[:, c:c + ch] = jax.nn.sigmoid(proj(base + c))


def _in_proj(x, g_mix, w_in_bf, b_in, tables, n_tab_blocks, c_conv, attn_w):
    t, d = x.shape
    n_in = w_in_bf.shape[1]
    tm = ROW_TILE
    row = lambda i: (i, 0)
    const = lambda i: (0, 0)
    tab = lambda i: (i % n_tab_blocks, 0)
    body = functools.partial(_in_proj_body, c_conv=c_conv, attn_w=attn_w, d_model=d)
    return pl.pallas_call(
        body,
        grid=(t // tm,),
        in_specs=[
            pl.BlockSpec((tm, d), row),
            pl.BlockSpec((1, d), const),
            pl.BlockSpec((d, n_in), const, pipeline_mode=pl.Buffered(1)),
            pl.BlockSpec((1, n_in), const),
            pl.BlockSpec((tm, LANES), tab),
            pl.BlockSpec((tm, LANES), tab),
            pl.BlockSpec((tm, LANES), tab),
        ],
        out_specs=[
            pl.BlockSpec((tm, c_conv), row),
            pl.BlockSpec((tm, attn_w), row),
            pl.BlockSpec((tm, attn_w), row),
            pl.BlockSpec((tm, attn_w), row),
            pl.BlockSpec((tm, 2 * d), row),
        ],
        out_shape=[
            jax.ShapeDtypeStruct((t, c_conv), F32),
            jax.ShapeDtypeStruct((t, attn_w), BF16),
            jax.ShapeDtypeStruct((t, attn_w), F32),
            jax.ShapeDtypeStruct((t, attn_w), F32),
            jax.ShapeDtypeStruct((t, 2 * d), F32),
        ],
        compiler_params=_cparams(1),
        name="in_proj",
    )(x, g_mix, w_in_bf, b_in, *tables)


def _conv_prompt_body(a_ref, halo_ref, w_ref, b_ref, o_ref, ext_ref, *, rows):
    i = pl.program_id(2)
    keep = jnp.where(i > 0, 1.0, 0.0).astype(F32)
    ext_ref[0:CONV_HALO, :] = halo_ref[0] * keep
    ext_ref[CONV_HALO:, :] = a_ref[0]
    w = w_ref[...]
    bias = jnp.broadcast_to(b_ref[...], (CONV_CHUNK, LANES))
    off = CONV_HALO - (CONV_WIDTH - 1)
    for r0 in range(0, rows, CONV_CHUNK):
        acc = bias
        for j in range(CONV_WIDTH):
            acc = acc + w[j:j + 1, :] * ext_ref[r0 + off + j:r0 + off + j + CONV_CHUNK, :]
        o_ref[0, r0:r0 + CONV_CHUNK, :] = acc


def _conv_prompt(a3, w_dw, b_dw):
    bsz, seq, c = a3.shape
    rows = CONV_ROWS
    hb = rows // CONV_HALO
    body = functools.partial(_conv_prompt_body, rows=rows)
    return pl.pallas_call(
        body,
        grid=(bsz, c // LANES, seq // rows),
        in_specs=[
            pl.BlockSpec((1, rows, LANES), lambda b, g, i: (b, i, g)),
            pl.BlockSpec((1, CONV_HALO, LANES),
                         lambda b, g, i: (b, jnp.maximum(i * hb - 1, 0), g)),
            pl.BlockSpec((CONV_WIDTH, LANES), lambda b, g, i: (0, g)),
            pl.BlockSpec((1, LANES), lambda b, g, i: (0, g)),
        ],
        out_specs=pl.BlockSpec((1, rows, LANES), lambda b, g, i: (b, i, g)),
        out_shape=jax.ShapeDtypeStruct((bsz, seq, c), F32),
        scratch_shapes=[pltpu.VMEM((rows + CONV_HALO, LANES), F32)],
        compiler_params=_cparams(3),
        name="conv_prompt",
    )(a3, a3, w_dw, b_dw)


def _conv_decode_body(state_ref, a_ref, w_ref, b_ref, o_ref, ext_ref, *, n_state, n_new):
    ext_ref[:, 0:n_state, :] = state_ref[...]
    ext_ref[:, n_state:n_state + n_new, :] = a_ref[...]
    w = w_ref[...]
    for t in range(n_new):
        win = ext_ref[:, t:t + CONV_WIDTH, :]
        o_ref[:, t:t + 1, :] = (jnp.sum(win * w[None], axis=1, keepdims=True)
                                + b_ref[...][None])


def _conv_decode(state, a3, w_dw, b_dw):
    bsz, n_state, c = state.shape
    n_new = a3.shape[1]
    bb = 8
    body = functools.partial(_conv_decode_body, n_state=n_state, n_new=n_new)
    return pl.pallas_call(
        body,
        grid=(bsz // bb,),
        in_specs=[
            pl.BlockSpec((bb, n_state, c), lambda i: (i, 0, 0)),
            pl.BlockSpec((bb, n_new, c), lambda i: (i, 0, 0)),
            pl.BlockSpec((CONV_WIDTH, c), lambda i: (0, 0)),
            pl.BlockSpec((1, c), lambda i: (0, 0)),
        ],
        out_specs=pl.BlockSpec((bb, n_new, c), lambda i: (i, 0, 0)),
        out_shape=jax.ShapeDtypeStruct((bsz, n_new, c), F32),
        scratch_shapes=[pltpu.VMEM((bb, n_state + n_new + 6, c), F32)],
        compiler_params=_cparams(1),
        name="conv_decode",
    )(state, a3, w_dw, b_dw)


def _attn_prompt_body(lq1, lk1, lq2, lk2, subg_ref, q_ref, k_ref, v_ref, o_ref,
                      kt_ref, vb_ref, acc1, acc2, m1, l1, m2, l2, *, blk, sub, n_blk, lam_init):
    i = pl.program_id(2)

    def reset_stats():
        for m_ref, l_ref, acc_ref in ((m1, l1, acc1), (m2, l2, acc2)):
            m_ref[...] = jnp.full(m_ref.shape, NEG_BIG, F32)
            l_ref[...] = jnp.zeros(l_ref.shape, F32)
            acc_ref[...] = jnp.zeros(acc_ref.shape, F32)

    @pl.when(i == 0)
    def _():
        for c in range(n_blk):
            kt_ref[c] = k_ref[0, c * blk:(c + 1) * blk, :].T.astype(BF16)
            vb_ref[c] = v_ref[0, c * blk:(c + 1) * blk, :].astype(BF16)
        reset_stats()

    stats = ((m1, l1, acc1), (m2, l2, acc2))
    n_sub = blk // sub

    def scores(kb, sb, masked):
        r0 = sb * sub
        ncol = r0 + sub if masked else blk
        out = []
        for c in range(2):
            qq = q_ref[0, r0:r0 + sub, c * HEAD_DIM:(c + 1) * HEAD_DIM]
            kk = kt_ref[kb, c * HEAD_DIM:(c + 1) * HEAD_DIM, 0:ncol]
            out.append(jnp.dot(qq, kk, preferred_element_type=F32))
        return out

    def softmax_values(kb, sb, masked, s_pair):
        r0 = sb * sub
        rows = slice(r0, r0 + sub)
        ncol = r0 + sub if masked else blk
        ps, alphas = [], []
        for s, (m_ref, l_ref, _) in zip(s_pair, stats):
            if masked:
                row = lax.broadcasted_iota(I32, (sub, ncol), 0) + r0
                col = lax.broadcasted_iota(I32, (sub, ncol), 1)
                s = jnp.where(col <= row, s, NEG_BIG)
            m_old = m_ref[rows, :]
            m_new = jnp.maximum(m_old, jnp.max(s, axis=-1, keepdims=True))
            alpha = jnp.exp(m_old - m_new)
            p = jnp.exp(s - jnp.concatenate([m_new] * (ncol // LANES), axis=1))
            l_ref[rows, :] = alpha * l_ref[rows, :] + jnp.sum(p, axis=-1, keepdims=True)
            m_ref[rows, :] = m_new
            ps.append(p.astype(BF16))
            alphas.append(alpha)
        pv = jnp.dot(jnp.concatenate(ps, axis=0), vb_ref[kb, 0:ncol, :],
                     preferred_element_type=F32)
        for c, (alpha, (_, _, acc_ref)) in enumerate(zip(alphas, stats)):
            acc_ref[rows, :] = alpha * acc_ref[rows, :] + pv[c * sub:(c + 1) * sub]

    def run(chains):
        ahead = 2
        pending = {n: scores(*chains[n]) for n in range(min(ahead, len(chains)))}
        for n, chain in enumerate(chains):
            if n + ahead < len(chains):
                pending[n + ahead] = scores(*chains[n + ahead])
            softmax_values(*chain, pending.pop(n))

    def block(kb, masked):
        return [(kb, sb, masked) for sb in range(n_sub)]

    def pair_body(t, carry):
        run(block(2 * t, False) + block(2 * t + 1, False))
        return carry

    lax.fori_loop(0, lax.shift_right_logical(i, 1), pair_body, 0)
    odd = (i & 1) == 1

    def finish():
        lam = _lam(lq1, lk1, lq2, lk2, lam_init)
        o = acc1[...] / l1[...] - lam * (acc2[...] / l2[...])
        o_ref[0] = (_rms(o, subg_ref[...]) * (1.0 - lam_init)).astype(BF16)
        reset_stats()

    @pl.when(odd)
    def _():
        run(block(i - 1, False) + block(i, True))
        finish()

    @pl.when(jnp.logical_not(odd))
    def _():
        run(block(i, True))
        finish()


def _attn_prompt(q3, k3, v3, lams, subln_g, lam_init):
    bsz, seq, _ = q3.shape
    blk = ATTN_BLOCK
    n_blk = seq // blk
    vec = pl.BlockSpec((1, HEAD_DIM), lambda b, h, i: (0, 0))
    body = functools.partial(_attn_prompt_body, blk=blk, sub=ATTN_SUB, n_blk=n_blk,
                             lam_init=lam_init)
    return pl.pallas_call(
        body,
        grid=(bsz, N_HEADS, n_blk),
        in_specs=[
            vec, vec, vec, vec,
            pl.BlockSpec((1, HEAD_W), lambda b, h, i: (0, 0)),
            pl.BlockSpec((1, blk, HEAD_W), lambda b, h, i: (b, i, h)),
            pl.BlockSpec((1, seq, HEAD_W), lambda b, h, i: (b, 0, h)),
            pl.BlockSpec((1, seq, HEAD_W), lambda b, h, i: (b, 0, h)),
        ],
        out_specs=pl.BlockSpec((1, blk, HEAD_W), lambda b, h, i: (b, i, h)),
        out_shape=jax.ShapeDtypeStruct(q3.shape, BF16),
        scratch_shapes=[
            pltpu.VMEM((n_blk, HEAD_W, blk), BF16),
            pltpu.VMEM((n_blk, blk, HEAD_W), BF16),
            pltpu.VMEM((blk, HEAD_W), F32),
            pltpu.VMEM((blk, HEAD_W), F32),
            pltpu.VMEM((blk, LANES), F32),
            pltpu.VMEM((blk, LANES), F32),
            pltpu.VMEM((blk, LANES), F32),
            pltpu.VMEM((blk, LANES), F32),
        ],
        compiler_params=_cparams(3),
        name="attn_prompt",
    )(*lams, subln_g, q3, k3, v3)


NEW_PAD = 16
Q_PAD = 8
DECODE_PAGE_GROUP = 4


def _attn_decode_body(pt_ref, lq1, lk1, lq2, lk2, subg_ref, q_ref, kn_ref, vn_ref, ck_hbm,
                      cv_hbm, o_ref, kbuf, vbuf, s_ref, stage, sem, *, n_pages, n_new,
                      lam_init):
    b = pl.program_id(0)
    buf = lax.rem(b, 2)
    n_past = n_pages * PAGE_SIZE
    width = stage.shape[1]
    page_rows = PAGE_SIZE * N_HEADS

    def fetch(seq, half):
        def one_page(j, carry):
            src = pl.ds(pl.multiple_of(pt_ref[seq, j] * page_rows, page_rows), page_rows)
            dst = pl.ds(pl.multiple_of(j * page_rows, page_rows), page_rows)
            pltpu.make_async_copy(ck_hbm.at[src], kbuf.at[half, dst], sem.at[0, half]).start()
            pltpu.make_async_copy(cv_hbm.at[src], vbuf.at[half, dst], sem.at[1, half]).start()
            return carry
        lax.fori_loop(0, n_pages, one_page, 0)

    def wait_pages(hbm, dst_buf, which):
        pltpu.make_async_copy(hbm.at[pl.ds(0, n_pages * page_rows)], dst_buf.at[buf],
                              sem.at[which, buf]).wait()

    @pl.when(b == 0)
    def _():
        fetch(0, 0)

    @pl.when(b + 1 < pl.num_programs(0))
    def _():
        fetch(b + 1, 1 - buf)

    k_now = kbuf.at[buf]
    v_now = vbuf.at[buf]

    def padded_rows(rows_f32):
        stage[...] = jnp.zeros(stage.shape, F32)
        stage[0:n_new, :] = rows_f32
        return stage[...].astype(BF16)

    q16 = padded_rows(q_ref[0])
    sel_r = lax.broadcasted_iota(I32, (NEW_PAD, LANES), 0)
    sel_c = lax.broadcasted_iota(I32, (NEW_PAD, LANES), 1)
    sel = (sel_c % Q_PAD == sel_r).astype(BF16)
    qrep = lax.dot_general(q16, sel, (((0,), (0,)), ((), ())),
                           preferred_element_type=F32)
    rr = lax.broadcasted_iota(I32, (width, LANES), 0)
    cc = lax.broadcasted_iota(I32, (width, LANES), 1)
    qblk = jnp.where(rr // HEAD_DIM == cc // Q_PAD, qrep, 0.0).astype(BF16)

    def head_major(pages_ref, j):
        return jnp.concatenate(
            [pages_ref[pl.ds(j * page_rows + h, PAGE_SIZE, stride=N_HEADS), :].astype(BF16)
             for h in range(N_HEADS)], axis=-1)

    groups = [range(j0, min(j0 + DECODE_PAGE_GROUP, n_pages))
              for j0 in range(0, n_pages, DECODE_PAGE_GROUP)]
    row_slices = [slice(g[0] * PAGE_SIZE, (g[-1] + 1) * PAGE_SIZE) for g in groups]

    wait_pages(ck_hbm, kbuf, 0)
    for g, rows in zip(groups, row_slices):
        keys = jnp.concatenate([head_major(k_now, j) for j in g], axis=0)
        s_ref[rows, :] = jnp.dot(keys, qblk, preferred_element_type=F32)
    s_new = jnp.dot(padded_rows(kn_ref[0]), qblk, preferred_element_type=F32)
    new_idx = lax.broadcasted_iota(I32, (NEW_PAD, LANES), 0)
    slot = lax.broadcasted_iota(I32, (NEW_PAD, LANES), 1) % Q_PAD
    s_new = jnp.where((new_idx <= slot) & (new_idx < n_new), s_new, NEG_BIG)
    m = jnp.maximum(jnp.max(s_ref[0:n_past, :], axis=0, keepdims=True),
                    jnp.max(s_new, axis=0, keepdims=True))

    contract0 = (((0,), (0,)), ((), ()))
    e_new = jnp.exp(s_new - m)
    denom = jnp.sum(e_new, axis=0, keepdims=True)
    acc = lax.dot_general(e_new.astype(BF16), padded_rows(vn_ref[0]), contract0,
                          preferred_element_type=F32)
    wait_pages(cv_hbm, vbuf, 1)
    for g, rows in zip(groups, row_slices):
        e = jnp.exp(s_ref[rows, :] - m)
        denom = denom + jnp.sum(e, axis=0, keepdims=True)
        vals = jnp.concatenate([head_major(v_now, j) for j in g], axis=0)
        acc = acc + lax.dot_general(e.astype(BF16), vals, contract0,
                                    preferred_element_type=F32)

    r_i = lax.broadcasted_iota(I32, (LANES, LANES), 0)
    c_i = lax.broadcasted_iota(I32, (LANES, LANES), 1)
    denom_rows = jnp.sum(jnp.where(r_i == c_i, jnp.broadcast_to(denom, (LANES, LANES)), 0.0),
                         axis=1, keepdims=True)
    o_norm = acc / denom_rows
    lam = _lam(lq1, lk1, lq2, lk2, lam_init)
    outs = []
    for h in range(N_HEADS):
        r0 = h * 2 * Q_PAD
        cols = slice(h * HEAD_W, (h + 1) * HEAD_W)
        oh = o_norm[r0:r0 + Q_PAD, cols] - lam * o_norm[r0 + Q_PAD:r0 + 2 * Q_PAD, cols]
        outs.append(_rms(oh, subg_ref[...]) * (1.0 - lam_init))
    o_ref[0] = jnp.concatenate(outs, axis=-1)[:n_new]


def _attn_decode(page_table, q3, kn3, vn3, cache_k, cache_v, lams, subln_g, lam_init):
    bsz, n_new, width = q3.shape
    n_pages = page_table.shape[1]
    vec = pl.BlockSpec((1, HEAD_DIM), lambda b, pt: (0, 0))
    per_b = pl.BlockSpec((1, n_new, width), lambda b, pt: (b, 0, 0))

    body = functools.partial(_attn_decode_body, n_pages=n_pages, n_new=n_new,
                             lam_init=lam_init)
    hbm = pl.BlockSpec(memory_space=pl.ANY)
    past_rows = n_pages * PAGE_SIZE * N_HEADS
    grid_spec = pltpu.PrefetchScalarGridSpec(
        num_scalar_prefetch=1,
        grid=(bsz,),
        in_specs=[vec, vec, vec, vec, pl.BlockSpec((1, HEAD_W), lambda b, pt: (0, 0)),
                  per_b, per_b, per_b, hbm, hbm],
        out_specs=per_b,
        scratch_shapes=[pltpu.VMEM((2, past_rows, HEAD_W), F32),
                        pltpu.VMEM((2, past_rows, HEAD_W), F32),
                        pltpu.VMEM((n_pages * PAGE_SIZE, LANES), F32),
                        pltpu.VMEM((NEW_PAD, width), F32),
                        pltpu.SemaphoreType.DMA((2, 2))],
    )
    return pl.pallas_call(
        body,
        grid_spec=grid_spec,
        out_shape=jax.ShapeDtypeStruct(q3.shape, F32),
        compiler_params=_cparams(1),
        name="attn_decode",
    )(page_table, *lams, subln_g, q3, kn3, vn3, cache_k, cache_v)


def _post_body(x_ref, conv_ref, o_ref, gate_ref, lng, lnb, wc, wa, wo, gffn, wr, br, *rest,
               d_model):
    x1_ref, hp_ref, ids_ref, tw_ref = rest[-4:]
    n_groups = 2
    rows_per = x_ref.shape[0] // n_groups
    groups = [slice(g * rows_per, (g + 1) * rows_per) for g in range(n_groups)]
    attn_outs = [jnp.dot(o_ref[r, :], wa[...], preferred_element_type=F32) for r in groups]
    conv_outs = []
    for r in groups:
        c = conv_ref[r, :]
        mu = jnp.mean(c, axis=-1, keepdims=True)
        xc = c - mu
        cn = (xc * lax.rsqrt(jnp.mean(xc * xc, axis=-1, keepdims=True) + LN_EPS) * lng[...]
              + lnb[...])
        cact = (cn * jax.nn.sigmoid(cn)).astype(BF16)
        conv_outs.append(jnp.dot(cact, wc[...], preferred_element_type=F32))
    hbs = []
    for r, conv_out, attn_out in zip(groups, conv_outs, attn_outs):
        merged = (gate_ref[r, :d_model] * conv_out
                  + gate_ref[r, d_model:] * attn_out).astype(BF16)
        x1 = x_ref[r, :] + jnp.dot(merged, wo[...], preferred_element_type=F32)
        x1_ref[r, :] = x1
        hbs.append(_rms(x1, gffn[...]).astype(BF16))
    for r, hb in zip(groups, hbs):
        logits = jnp.dot(hb, wr[...], preferred_element_type=F32) + br[...]
        lane = lax.broadcasted_iota(I32, logits.shape, 1)
        vals, ids = [], []
        cur = logits
        for _ in range(TOP_K):
            mx = jnp.max(cur, axis=-1, keepdims=True)
            idx = jnp.min(jnp.where(cur == mx, lane, N_EXPERTS), axis=-1, keepdims=True)
            vals.append(mx)
            ids.append(idx)
            cur = jnp.where(lane == idx, -jnp.inf, cur)
        es = [jnp.exp(v - vals[0]) for v in vals]
        den = es[0] + es[1] + es[2] + es[3]
        tw_ref[r, :] = _columns([e / den for e in es])
        ids_ref[r, :] = _columns(ids)

        bits = pltpu.bitcast(hb.astype(F32), U32)
        half = d_model // 2
        words = (bits[:, :half] >> 16) | (bits[:, half:] & jnp.uint32(0xFFFF0000))
        pieces = half // LANES
        for c in range(pieces):
            hp_ref[pl.ds(r.start * pieces + c, rows_per, stride=pieces), :] = (
                words[:, c * LANES:(c + 1) * LANES])


def _post(x, conv, o, gate, ln_g, ln_b, wc, wa, wo, g_ffn, wr, br, tokens, first_token):
    t, d = x.shape
    tm = ROW_TILE
    first_tile = first_token // tm
    row = lambda i: (i, 0)
    const = lambda i: (0, 0)
    mat = pl.BlockSpec((d, d), const)
    vec = pl.BlockSpec((1, d), const)
    body = functools.partial(_post_body, d_model=d)
    in_specs = [
        pl.BlockSpec((tm, d), row), pl.BlockSpec((tm, d), row), pl.BlockSpec((tm, d), row),
        pl.BlockSpec((tm, 2 * d), row),
        vec, vec, mat, mat, mat, vec,
        pl.BlockSpec((d, N_EXPERTS), const), pl.BlockSpec((1, N_EXPERTS), const),
    ]
    in_specs.append(pl.BlockSpec(memory_space=pl.ANY))
    args = [x, conv, o, gate, ln_g, ln_b, wc, wa, wo, g_ffn, wr, br, tokens]
    return pl.pallas_call(
        body,
        grid=(t // tm,),
        in_specs=in_specs,
        out_specs=[
            pl.BlockSpec((tm, d), row),
            pl.BlockSpec((tm * (d // 2 // LANES), LANES), lambda i: (first_tile + i, 0)),
            pl.BlockSpec((tm, TOP_K), row), pl.BlockSpec((tm, TOP_K), row),
        ],
        out_shape=[
            jax.ShapeDtypeStruct((t, d), F32),
            jax.ShapeDtypeStruct(tokens.shape, U32),
            jax.ShapeDtypeStruct((t, TOP_K), I32), jax.ShapeDtypeStruct((t, TOP_K), F32),
        ],
        input_output_aliases={len(args) - 1: 1},
        compiler_params=_cparams(1),
        name="post",
    )(*args)


def _lane_cumsum(x):
    lane = lax.broadcasted_iota(I32, x.shape, 1)
    s = 1
    while s < LANES:
        x = x + jnp.where(lane >= s, pltpu.roll(x, s, 1), 0.0)
        s *= 2
    return x


def _wrap_i32(value):
    return (value + 2 ** 31) % 2 ** 32 - 2 ** 31


def _route_word(token, slot):
    dst = ((token // COMBINE_TILE) * (TOP_K * COMBINE_TILE) + slot * COMBINE_TILE
           + token % COMBINE_TILE)
    return jnp.left_shift(dst, ROUTE_TOKEN_BITS) | token


def _positions_body(ids_ref, pos_ref, word_ref, te_ref, count_ref, start_ref, *, n_tiles_pad):
    p = pl.program_id(0)
    i = pl.program_id(1)
    ids = ids_ref[...]
    tt = ids.shape[0]
    lane = lax.broadcasted_iota(I32, (tt, LANES), 1)
    onehots = [ids[:, k:k + 1] == lane for k in range(TOP_K)]
    tile_counts = [jnp.sum(oh.astype(F32), axis=0, keepdims=True) for oh in onehots]
    tile_total = tile_counts[0] + tile_counts[1] + tile_counts[2] + tile_counts[3]

    @pl.when((p == 0) & (i == 0))
    def _():
        count_ref[...] = jnp.zeros(count_ref.shape, F32)

    @pl.when((p == 1) & (i == 0))
    def _():
        counts = count_ref[...]
        padded = jnp.ceil(counts * (1.0 / MOE_TILE)) * MOE_TILE
        ends = _lane_cumsum(padded)
        start_ref[...] = ends - padded
        count_ref[...] = jnp.zeros(count_ref.shape, F32)
        tile_start = (lax.broadcasted_iota(I32, (n_tiles_pad, LANES), 0) * MOE_TILE).astype(F32)
        elane = lax.broadcasted_iota(I32, (n_tiles_pad, LANES), 1)
        done = (ends[0:1, :] <= tile_start) & (elane < N_EXPERTS)
        n_done = jnp.sum(done.astype(F32), axis=-1, keepdims=True)
        te_ref[...] = jnp.broadcast_to(n_done, (n_tiles_pad, LANES)).astype(I32)

    @pl.when(p == 1)
    def _():
        r = lax.broadcasted_iota(I32, (tt, tt), 0)
        c = lax.broadcasted_iota(I32, (tt, tt), 1)
        earlier = (c < r).astype(BF16)
        run = start_ref[0:1, :] + count_ref[0:1, :]
        cols = []
        for k in range(TOP_K):
            within = jnp.dot(earlier, onehots[k].astype(BF16), preferred_element_type=F32)
            cols.append(jnp.sum(jnp.where(onehots[k], within + run, 0.0),
                                axis=-1, keepdims=True))
            run = run + tile_counts[k]
        pos_ref[...] = _columns(cols).astype(I32)
        token = lax.broadcasted_iota(I32, (tt, TOP_K), 0) + i * tt
        slot = lax.broadcasted_iota(I32, (tt, TOP_K), 1)
        word_ref[...] = _route_word(token, slot)

    count_ref[...] = count_ref[...] + tile_total


def _positions(ids, n_tiles_pad):
    t = ids.shape[0]
    tt = POS_TILE
    body = functools.partial(_positions_body, n_tiles_pad=n_tiles_pad)
    return pl.pallas_call(
        body,
        grid=(2, t // tt),
        in_specs=[pl.BlockSpec((tt, TOP_K), lambda p, i: (i, 0))],
        out_specs=[
            pl.BlockSpec((tt, TOP_K), lambda p, i: (i * p, 0)),
            pl.BlockSpec((tt, TOP_K), lambda p, i: (i * p, 0)),
            pl.BlockSpec((n_tiles_pad, LANES), lambda p, i: (0, 0)),
        ],
        out_shape=[
            jax.ShapeDtypeStruct((t, TOP_K), I32),
            jax.ShapeDtypeStruct((t, TOP_K), I32),
            jax.ShapeDtypeStruct((n_tiles_pad, LANES), I32),
        ],
        scratch_shapes=[pltpu.VMEM((8, LANES), F32), pltpu.VMEM((8, LANES), F32)],
        compiler_params=_cparams(2),
        name="moe_positions",
    )(ids)


def _inverse_body(pos_ref, word_ref, init_hbm, inv_hbm, inv_smem, sem):
    c = pl.program_id(0)

    @pl.when(c == 0)
    def _():
        load = pltpu.make_async_copy(init_hbm, inv_smem, sem)
        load.start()
        load.wait()

    def place(a, carry):
        inv_smem[pos_ref[a]] = word_ref[a]
        return carry

    lax.fori_loop(0, pos_ref.shape[0], place, 0, unroll=32)

    @pl.when(c == pl.num_programs(0) - 1)
    def _():
        store = pltpu.make_async_copy(inv_smem, inv_hbm, sem)
        store.start()
        store.wait()


def _inverse(pos_flat, word_flat, init):
    n = pos_flat.shape[0]
    chunk = INVERSE_CHUNK
    assert n % chunk == 0
    smem_chunk = pl.BlockSpec((chunk,), lambda c: (c,), memory_space=pltpu.SMEM)
    return pl.pallas_call(
        _inverse_body,
        grid=(n // chunk,),
        in_specs=[smem_chunk, smem_chunk, pl.BlockSpec(memory_space=pl.ANY)],
        out_specs=pl.BlockSpec(memory_space=pl.ANY),
        out_shape=jax.ShapeDtypeStruct(init.shape, I32),
        scratch_shapes=[pltpu.SMEM(init.shape, I32), pltpu.SemaphoreType.DMA],
        compiler_params=_cparams(1),
        name="moe_inverse",
    )(pos_flat, word_flat, init)


def _experts_body(te_ref, nv_ref, grp_ref, nxt_ref, inv_next, inv_prev, b1_ref, b2_ref,
                  w1_hbm, w2_hbm, tok_hbm, out_hbm, xs_buf, y_buf, act_buf, w1f, w2f, w1b, w2b,
                  gsem, ssem, wsem, *, n_tok, n_out):
    j = pl.program_id(0)
    nv = nv_ref[0]

    def weight_copies(expert, wslot):
        return (pltpu.make_async_copy(w1_hbm.at[expert], w1f.at[wslot], wsem.at[wslot]),
                pltpu.make_async_copy(w2_hbm.at[expert], w2f.at[wslot], wsem.at[wslot]))
    d_ff, d = w2b.shape
    half = d // 2
    xp, yp = half // LANES, d // LANES
    tm = xs_buf.shape[1] // xp
    slot = lax.rem(j, 2)
    other = 1 - slot
    token_mask = (1 << ROUTE_TOKEN_BITS) - 1

    def gather_row(word, r, buf_slot):
        tok = pl.multiple_of((word & token_mask) * xp, xp)
        return pltpu.make_async_copy(tok_hbm.at[pl.ds(tok, xp)],
                                     xs_buf.at[buf_slot, pl.ds(r * xp, xp)], gsem)

    def scatter_row(word, r, buf_slot):
        dst = pl.multiple_of(lax.shift_right_logical(word, ROUTE_TOKEN_BITS) * yp, yp)
        return pltpu.make_async_copy(y_buf.at[buf_slot, pl.ds(r * yp, yp)],
                                     out_hbm.at[pl.ds(dst, yp)], ssem.at[buf_slot])

    def spare_word(r):
        return _wrap_i32((n_out + r) << ROUTE_TOKEN_BITS)

    def wait_gather(buf_slot):
        pltpu.make_async_copy(tok_hbm.at[pl.ds(0, tm * xp)], xs_buf.at[buf_slot], gsem).wait()

    def wait_scatter(buf_slot):
        pltpu.make_async_copy(y_buf.at[buf_slot], out_hbm.at[pl.ds(0, tm * yp)],
                              ssem.at[buf_slot]).wait()

    @pl.when(j == 0)
    def _():
        for copy in weight_copies(te_ref[0], 0):
            copy.start()
        y_buf[...] = jnp.zeros(y_buf.shape, F32)
        for r in range(tm):
            gather_row(inv_prev[0, 0, r], r, 0).start()
            scatter_row(spare_word(r), r, 0).start()

    @pl.when(j < nv)
    def _():
        wait_gather(slot)

        @pl.when((j == 0) | (te_ref[j] != te_ref[jnp.maximum(j - 1, 0)]))
        def _():
            wslot = lax.rem(grp_ref[j], 2)
            for copy in weight_copies(te_ref[j], wslot):
                copy.wait()

            @pl.when(nxt_ref[j] >= 0)
            def _():
                for copy in weight_copies(nxt_ref[j], 1 - wslot):
                    copy.start()

            for c in range(0, d, LANES):
                w1b[c:c + LANES, :] = w1f[wslot, c:c + LANES, :].astype(BF16)
            for c in range(0, d_ff, LANES):
                w2b[c:c + LANES, :] = w2f[wslot, c:c + LANES, :].astype(BF16)

        has_prev = j > 0

        def gather_rows(r0, r1):
            for r in range(r0, r1):
                gather_row(inv_next[0, 0, r], r, other).start(priority=1)

        def scatter_rows(r0, r1):
            for r in range(r0, r1):
                word = jnp.where(has_prev, inv_prev[0, 0, r], spare_word(r))
                scatter_row(word, r, other).start()

        xs_now = xs_buf.at[slot]
        bits = jnp.concatenate(
            [xs_now[pl.ds(c, tm, stride=xp), :] for c in range(xp)], axis=-1)
        x_lo = pltpu.bitcast(bits << 16, F32).astype(BF16)
        x_hi = pltpu.bitcast(bits & jnp.uint32(0xFFFF0000), F32).astype(BF16)
        y_now = y_buf.at[slot]
        ch = EXPERT_CHUNK

        def proj1(c0):
            return (jnp.dot(x_lo, w1b[:half, c0:c0 + ch], preferred_element_type=F32)
                    + jnp.dot(x_hi, w1b[half:, c0:c0 + ch], preferred_element_type=F32)
                    + b1_ref[0, :, c0:c0 + ch])

        rows_1 = tm // (d_ff // ch)
        for n, c in enumerate(range(0, d_ff, ch)):
            gate = jnp.minimum(proj1(c), SWIGLU_LIMIT)
            lin = jnp.clip(proj1(d_ff + c), -SWIGLU_LIMIT, SWIGLU_LIMIT)
            act_buf[:, c:c + ch] = (gate * jax.nn.sigmoid(SWIGLU_ALPHA * gate)
                                    * (lin + 1.0)).astype(BF16)
            gather_rows(n * rows_1, (n + 1) * rows_1)
        wait_scatter(slot)
        rows_2 = tm // (d // ch)
        for n, c in enumerate(range(0, d, ch)):
            y_cols = (jnp.dot(act_buf[...], w2b[:, c:c + ch], preferred_element_type=F32)
                      + b2_ref[0, :, c:c + ch])
            for g in range(ch // LANES):
                y_now[pl.ds(c // LANES + g, tm, stride=yp), :] = (
                    y_cols[:, g * LANES:(g + 1) * LANES])
            scatter_rows(n * rows_2, (n + 1) * rows_2)

    @pl.when(j == nv)
    def _():
        wait_gather(slot)
        wait_scatter(slot)
        for r in range(tm):
            scatter_row(inv_prev[0, 0, r], r, other).start()
        wait_scatter(other)


def _experts(tile_expert, n_valid, inv3, tokens, w1, b1, w2, b2, n_tok, n_out):
    nt, _, tm = inv3.shape
    d_ff, d = w2.shape[1:]
    half = d // 2
    xp, yp = half // LANES, d // LANES
    body = functools.partial(_experts_body, n_tok=n_tok, n_out=n_out)
    smem_tile = lambda index_map: pl.BlockSpec((1, 1, tm), index_map, memory_space=pltpu.SMEM)
    starts = jnp.concatenate([jnp.ones((1,), I32),
                              (tile_expert[1:] != tile_expert[:-1]).astype(I32)])
    group = jnp.cumsum(starts) - 1
    after = jnp.searchsorted(tile_expert, tile_expert, side="right")
    next_expert = jnp.where(after < n_valid[0],
                            tile_expert[jnp.minimum(after, nt - 1)], -1).astype(I32)
    hbm = pl.BlockSpec(memory_space=pl.ANY)
    grid_spec = pltpu.PrefetchScalarGridSpec(
        num_scalar_prefetch=4,
        grid=(nt,),
        in_specs=[
            smem_tile(lambda j, te, nv, grp, nxt: (jnp.minimum(j + 1, nt - 1), 0, 0)),
            smem_tile(lambda j, te, nv, grp, nxt: (jnp.maximum(j - 1, 0), 0, 0)),
            pl.BlockSpec((1, 1, 2 * d_ff), lambda j, te, nv, grp, nxt: (te[j], 0, 0)),
            pl.BlockSpec((1, 1, d), lambda j, te, nv, grp, nxt: (te[j], 0, 0)),
            hbm, hbm, hbm,
        ],
        out_specs=hbm,
        scratch_shapes=[
            pltpu.VMEM((2, tm * xp, LANES), U32),
            pltpu.VMEM((2, tm * yp, LANES), F32),
            pltpu.VMEM((tm, d_ff), BF16),
            pltpu.VMEM((2, d, 2 * d_ff), F32),
            pltpu.VMEM((2, d_ff, d), F32),
            pltpu.VMEM((d, 2 * d_ff), BF16),
            pltpu.VMEM((d_ff, d), BF16),
            pltpu.SemaphoreType.DMA,
            pltpu.SemaphoreType.DMA((2,)),
            pltpu.SemaphoreType.DMA((2,)),
        ],
    )
    return pl.pallas_call(
        body,
        grid_spec=grid_spec,
        out_shape=jax.ShapeDtypeStruct(((n_out + tm) * yp, LANES), F32),
        compiler_params=_cparams(1),
        name="moe_experts",
    )(tile_expert, n_valid, group.astype(I32), next_expert, inv3, inv3, b1, b2, w1, w2, tokens)


def _combine_body(tw_ref, x1_ref, gf_ref, rows_ref, y_ref):
    tt, d = x1_ref.shape
    yp = d // LANES
    tw = tw_ref[...]
    cols = []
    for c in range(yp):
        acc = None
        for k in range(TOP_K):
            piece = rows_ref[pl.ds(k * tt * yp + c, tt, stride=yp), :]
            term = tw[:, k:k + 1] * piece
            acc = term if acc is None else acc + term
        cols.append(acc)
    y_ref[...] = _rms(x1_ref[...] + jnp.concatenate(cols, axis=-1), gf_ref[...])


def _combine(tw, x1, g_final, expert_rows, first_tile):
    t, d = x1.shape
    tt = COMBINE_TILE
    return pl.pallas_call(
        _combine_body,
        grid=(t // tt,),
        in_specs=[
            pl.BlockSpec((tt, TOP_K), lambda i: (i, 0)),
            pl.BlockSpec((tt, d), lambda i: (i, 0)),
            pl.BlockSpec((1, d), lambda i: (0, 0)),
            pl.BlockSpec((TOP_K * tt * (d // LANES), LANES), lambda i: (first_tile + i, 0)),
        ],
        out_specs=pl.BlockSpec((tt, d), lambda i: (i, 0)),
        out_shape=jax.ShapeDtypeStruct((t, d), F32),
        compiler_params=_cparams(1),
        name="moe_combine",
    )(tw, x1, g_final, expert_rows)


def kernel(x_prompt, x_sample, cache_k, cache_v, state_conv, page_table, g_mix, w_in, b_in,
           w_dw, b_dw, ln_g, ln_b, w_conv_out, lam_q1, lam_k1, lam_q2, lam_k2, subln_g,
           w_attn_out, w_o, g_ffn, w_router, b_router, w_moe1, b_moe1, w_moe2, b_moe2, g_final):
    bsz, seq, d = x_prompt.shape
    dec_b, dec_s, _ = x_sample.shape
    depth = g_mix.shape[0]
    c_conv = w_dw.shape[2]
    attn_w = N_HEADS * HEAD_W
    n_pages = page_table.shape[1]
    past_len = n_pages * PAGE_SIZE
    t_p, t_s = bsz * seq, dec_b * dec_s
    t_all = t_p + t_s
    d_ff = w_moe2.shape[2]
    assert depth == 1, "the combine kernel fuses the final norm, so only one layer is supported"
    assert seq % ROW_TILE == 0 and t_s % ROW_TILE == 0 and seq % CONV_ROWS == 0
    assert seq % ATTN_BLOCK == 0 and t_all % POS_TILE == 0 and dec_b % 8 == 0
    assert t_p % COMBINE_TILE == 0 and t_s % COMBINE_TILE == 0

    tab_p = _rope_tables(jnp.arange(seq))
    tab_s = _rope_tables(jnp.tile(past_len + jnp.arange(dec_s), dec_b))
    n_rows = t_all * TOP_K + N_EXPERTS * MOE_TILE
    n_tiles = n_rows // MOE_TILE
    row2 = lambda v: v.reshape(1, -1)

    hp, hs = x_prompt.reshape(t_p, d), x_sample.reshape(t_s, d)
    outs = [[] for _ in range(6)]
    for l in range(depth):
        lam_init = 0.8 - 0.6 * math.exp(-0.3 * l)
        lams = (row2(lam_q1[l]), row2(lam_k1[l]), row2(lam_q2[l]), row2(lam_k2[l]))
        subg = row2(subln_g[l])
        w_in_bf = w_in[l].astype(BF16)
        wc, wa, wo = (w_conv_out[l].astype(BF16), w_attn_out[l].astype(BF16),
                      w_o[l].astype(BF16))
        wr = w_router[l].astype(BF16)
        b1 = b_moe1[l].reshape(N_EXPERTS, 1, 2 * d_ff)
        b2 = b_moe2[l].reshape(N_EXPERTS, 1, d)
        proj_args = (row2(g_mix[l]), w_in_bf, row2(b_in[l]))
        post_args = (row2(ln_g[l]), row2(ln_b[l]), wc, wa, wo, row2(g_ffn[l]), wr,
                     row2(b_router[l]))

        a_p, q_p, k_p, v_p, gate_p = _in_proj(hp, *proj_args, tab_p, seq // ROW_TILE,
                                              c_conv, attn_w)
        a_p3 = a_p.reshape(bsz, seq, c_conv)
        conv_p = _conv_prompt(a_p3, w_dw[l], row2(b_dw[l]))
        o_p = _attn_prompt(q_p.reshape(bsz, seq, attn_w), k_p.reshape(bsz, seq, attn_w),
                           v_p.reshape(bsz, seq, attn_w), lams, subg, lam_init)
        tokens = jnp.zeros(((t_all + PAD_TOKENS) * (d // 2 // LANES), LANES), U32)
        x1_p, tokens, ids_p, tw_p = _post(hp, conv_p.reshape(t_p, c_conv),
                                          o_p.reshape(t_p, attn_w), gate_p, *post_args,
                                          tokens, 0)

        a_s, q_s, k_s, v_s, gate_s = _in_proj(hs, *proj_args, tab_s, 1, c_conv, attn_w)
        a_s3 = a_s.reshape(dec_b, dec_s, c_conv)
        conv_s = _conv_decode(state_conv[l], a_s3, w_dw[l], row2(b_dw[l]))
        pool = cache_k.shape[1]
        o_s = _attn_decode(page_table, q_s.astype(F32).reshape(dec_b, dec_s, attn_w),
                           k_s.reshape(dec_b, dec_s, attn_w), v_s.reshape(dec_b, dec_s, attn_w),
                           cache_k[l].reshape(pool * PAGE_SIZE * N_HEADS, HEAD_W),
                           cache_v[l].reshape(pool * PAGE_SIZE * N_HEADS, HEAD_W),
                           lams, subg, lam_init)
        x1_s, tokens, ids_s, tw_s = _post(hs, conv_s.reshape(t_s, c_conv),
                                          o_s.reshape(t_s, attn_w).astype(BF16), gate_s,
                                          *post_args, tokens, t_p)

        ids = jnp.concatenate([ids_p, ids_s], axis=0)
        pos, word, te = _positions(ids, n_tiles)
        tile_expert = jnp.minimum(te[:, 0], N_EXPERTS - 1)
        n_valid = jnp.sum((te[:, 0] < N_EXPERTS).astype(I32)).reshape(1)
        slot_idx = jnp.arange(n_rows, dtype=U32)
        n_out = t_all * TOP_K
        unused = (((n_out + slot_idx % MOE_TILE) << ROUTE_TOKEN_BITS)
                  | (t_all + slot_idx % PAD_TOKENS))
        inv = _inverse(pos.reshape(-1), word.reshape(-1), lax.bitcast_convert_type(unused, I32))
        expert_rows = _experts(tile_expert, n_valid, inv.reshape(n_tiles, 1, MOE_TILE), tokens,
                               w_moe1[l], b1, w_moe2[l], b2, t_all, n_out)
        gf = row2(g_final)
        hp = _combine(tw_p, x1_p, gf, expert_rows, 0)
        hs = _combine(tw_s, x1_s, gf, expert_rows, t_p // COMBINE_TILE)

        outs[0].append(k_p.reshape(bsz, seq, N_HEADS, HEAD_W))
        outs[1].append(v_p.reshape(bsz, seq, N_HEADS, HEAD_W))
        outs[2].append(a_p3[:, seq - (CONV_WIDTH - 1):])
        outs[3].append(k_s.reshape(dec_b, dec_s, N_HEADS, HEAD_W))
        outs[4].append(v_s.reshape(dec_b, dec_s, N_HEADS, HEAD_W))
        outs[5].append(jnp.concatenate([state_conv[l], a_s3], axis=1)[:, -(CONV_WIDTH - 1):])

    y_prompt = hp.reshape(bsz, seq, d)
    y_sample = hs.reshape(dec_b, dec_s, d)
    return (y_prompt, y_sample) + tuple(jnp.stack(o) for o in outs)
```

```python
import functools
import math

import jax
import jax.numpy as jnp
from jax import lax
from jax.experimental import pallas as pl
from jax.experimental.pallas import tpu as pltpu

F32 = jnp.float32
BF16 = jnp.bfloat16
I32 = jnp.int32
U32 = jnp.uint32

N_HEADS = 8
HEAD_DIM = 64
HEAD_W = 2 * HEAD_DIM
ROT_DIM = HEAD_DIM // 4
ROT_HALF = ROT_DIM // 2
ROPE_THETA = 500000.0
CONV_WIDTH = 31
CONV_HALO = 32
N_EXPERTS = 32
TOP_K = 4
SWIGLU_ALPHA = 1.702
SWIGLU_LIMIT = 7.0
RMS_EPS = 1e-5
LN_EPS = 1e-5
PAGE_SIZE = 128
LANES = 128
NEG_BIG = -1e30

ROW_TILE = 512
ATTN_BLOCK = 512
ATTN_SUB = 128
CONV_ROWS = 1024
CONV_CHUNK = 64
MOE_TILE = 256
EXPERT_CHUNK = 256
POS_TILE = 512
COMBINE_TILE = 256
ROUTE_TOKEN_BITS = 15
INVERSE_CHUNK = 11 * 1024
PAD_TOKENS = 64
VMEM_LIMIT = 56 * 1024 * 1024


def _cparams(n_axes, vmem=VMEM_LIMIT):
    return pltpu.CompilerParams(dimension_semantics=("arbitrary",) * n_axes,
                                vmem_limit_bytes=vmem)


def _rms(x, g):
    return x * lax.rsqrt(jnp.mean(x * x, axis=-1, keepdims=True) + RMS_EPS) * g


def _columns(cols):
    rows = cols[0].shape[0]
    lane = lax.broadcasted_iota(I32, (rows, len(cols)), 1)
    out = jnp.zeros((rows, len(cols)), cols[0].dtype)
    for k, col in enumerate(cols):
        out = jnp.where(lane == k, col, out)
    return out


def _lam(lq1, lk1, lq2, lk2, lam_init):
    s1 = jnp.sum(lq1[...] * lk1[...], axis=-1, keepdims=True)
    s2 = jnp.sum(lq2[...] * lk2[...], axis=-1, keepdims=True)
    return jnp.exp(s1) - jnp.exp(s2) + lam_init


def _rope_tables(pos):
    inv = jnp.power(jnp.float32(ROPE_THETA),
                    -jnp.arange(ROT_HALF, dtype=F32) * (2.0 / ROT_DIM))
    ang = pos.astype(F32)[:, None] * inv[None, :]
    cos, sin = jnp.cos(ang), jnp.sin(ang)
    n = pos.shape[0]
    rest = HEAD_DIM - ROT_DIM
    zh = jnp.zeros((n, ROT_HALF), F32)
    c64 = jnp.concatenate([cos, cos, jnp.ones((n, rest), F32)], axis=-1)
    sa64 = jnp.concatenate([-sin, zh, jnp.zeros((n, rest), F32)], axis=-1)
    sb64 = jnp.concatenate([zh, sin, jnp.zeros((n, rest), F32)], axis=-1)
    tile = lambda t: jnp.concatenate([t, t], axis=-1)
    return tile(c64), tile(sa64), tile(sb64)


def _in_proj_body(x_ref, g_ref, w_ref, b_ref, cos_ref, sa_ref, sb_ref,
                  a_ref, q_ref, k_ref, v_ref, gate_ref, *, c_conv, attn_w, d_model):
    h = _rms(x_ref[...], g_ref[...]).astype(BF16)
    cos, sa, sb = cos_ref[...], sa_ref[...], sb_ref[...]
    ch = 512

    def proj(c0):
        return (jnp.dot(h, w_ref[:, c0:c0 + ch], preferred_element_type=F32)
                + b_ref[:, c0:c0 + ch])

    def rope(z):
        outs = []
        for j in range(ch // LANES):
            zj = z[:, j * LANES:(j + 1) * LANES]
            outs.append(zj * cos + pltpu.roll(zj, LANES - ROT_HALF, 1) * sa
                        + pltpu.roll(zj, ROT_HALF, 1) * sb)
        return jnp.concatenate(outs, axis=-1)

    for c in range(0, c_conv, ch):
        a_ref[:, c:c + ch] = proj(c) * jax.nn.sigmoid(proj(c_conv + c))
    base = 2 * c_conv
    for c in range(0, attn_w, ch):
        q_ref[:, c:c + ch] = (rope(proj(base + c)) * (HEAD_DIM ** -0.5)).astype(BF16)
    base += attn_w
    for c in range(0, attn_w, ch):
        k_ref[:, c:c + ch] = rope(proj(base + c))
    base += attn_w
    for c in range(0, attn_w, ch):
        v_ref[:, c:c + ch] = proj(base + c)
    base += attn_w
    for c in range(0, 2 * d_model, ch):
        gate_ref[:, c:c + ch] = jax.nn.sigmoid(proj(base + c))


def _in_proj(x, g_mix, w_in_bf, b_in, tables, n_tab_blocks, c_conv, attn_w):
    t, d = x.shape
    n_in = w_in_bf.shape[1]
    tm = ROW_TILE
    row = lambda i: (i, 0)
    const = lambda i: (0, 0)
    tab = lambda i: (i % n_tab_blocks, 0)
    body = functools.partial(_in_proj_body, c_conv=c_conv, attn_w=attn_w, d_model=d)
    return pl.pallas_call(
        body,
        grid=(t // tm,),
        in_specs=[
            pl.BlockSpec((tm, d), row),
            pl.BlockSpec((1, d), const),
            pl.BlockSpec((d, n_in), const, pipeline_mode=pl.Buffered(1)),
            pl.BlockSpec((1, n_in), const),
            pl.BlockSpec((tm, LANES), tab),
            pl.BlockSpec((tm, LANES), tab),
            pl.BlockSpec((tm, LANES), tab),
        ],
        out_specs=[
            pl.BlockSpec((tm, c_conv), row),
            pl.BlockSpec((tm, attn_w), row),
            pl.BlockSpec((tm, attn_w), row),
            pl.BlockSpec((tm, attn_w), row),
            pl.BlockSpec((tm, 2 * d), row),
        ],
        out_shape=[
            jax.ShapeDtypeStruct((t, c_conv), F32),
            jax.ShapeDtypeStruct((t, attn_w), BF16),
            jax.ShapeDtypeStruct((t, attn_w), F32),
            jax.ShapeDtypeStruct((t, attn_w), F32),
            jax.ShapeDtypeStruct((t, 2 * d), F32),
        ],
        compiler_params=_cparams(1),
        name="in_proj",
    )(x, g_mix, w_in_bf, b_in, *tables)


def _conv_prompt_body(a_ref, halo_ref, w_ref, b_ref, o_ref, ext_ref, *, rows):
    i = pl.program_id(2)
    keep = jnp.where(i > 0, 1.0, 0.0).astype(F32)
    ext_ref[0:CONV_HALO, :] = halo_ref[0] * keep
    ext_ref[CONV_HALO:, :] = a_ref[0]
    w = w_ref[...]
    bias = jnp.broadcast_to(b_ref[...], (CONV_CHUNK, LANES))
    off = CONV_HALO - (CONV_WIDTH - 1)
    for r0 in range(0, rows, CONV_CHUNK):
        acc = bias
        for j in range(CONV_WIDTH):
            acc = acc + w[j:j + 1, :] * ext_ref[r0 + off + j:r0 + off + j + CONV_CHUNK, :]
        o_ref[0, r0:r0 + CONV_CHUNK, :] = acc


def _conv_prompt(a3, w_dw, b_dw):
    bsz, seq, c = a3.shape
    rows = CONV_ROWS
    hb = rows // CONV_HALO
    body = functools.partial(_conv_prompt_body, rows=rows)
    return pl.pallas_call(
        body,
        grid=(bsz, c // LANES, seq // rows),
        in_specs=[
            pl.BlockSpec((1, rows, LANES), lambda b, g, i: (b, i, g)),
            pl.BlockSpec((1, CONV_HALO, LANES),
                         lambda b, g, i: (b, jnp.maximum(i * hb - 1, 0), g)),
            pl.BlockSpec((CONV_WIDTH, LANES), lambda b, g, i: (0, g)),
            pl.BlockSpec((1, LANES), lambda b, g, i: (0, g)),
        ],
        out_specs=pl.BlockSpec((1, rows, LANES), lambda b, g, i: (b, i, g)),
        out_shape=jax.ShapeDtypeStruct((bsz, seq, c), F32),
        scratch_shapes=[pltpu.VMEM((rows + CONV_HALO, LANES), F32)],
        compiler_params=_cparams(3),
        name="conv_prompt",
    )(a3, a3, w_dw, b_dw)


def _conv_decode_body(state_ref, a_ref, w_ref, b_ref, o_ref, ext_ref, *, n_state, n_new):
    ext_ref[:, 0:n_state, :] = state_ref[...]
    ext_ref[:, n_state:n_state + n_new, :] = a_ref[...]
    w = w_ref[...]
    for t in range(n_new):
        win = ext_ref[:, t:t + CONV_WIDTH, :]
        o_ref[:, t:t + 1, :] = (jnp.sum(win * w[None], axis=1, keepdims=True)
                                + b_ref[...][None])


def _conv_decode(state, a3, w_dw, b_dw):
    bsz, n_state, c = state.shape
    n_new = a3.shape[1]
    bb = 8
    body = functools.partial(_conv_decode_body, n_state=n_state, n_new=n_new)
    return pl.pallas_call(
        body,
        grid=(bsz // bb,),
        in_specs=[
            pl.BlockSpec((bb, n_state, c), lambda i: (i, 0, 0)),
            pl.BlockSpec((bb, n_new, c), lambda i: (i, 0, 0)),
            pl.BlockSpec((CONV_WIDTH, c), lambda i: (0, 0)),
            pl.BlockSpec((1, c), lambda i: (0, 0)),
        ],
        out_specs=pl.BlockSpec((bb, n_new, c), lambda i: (i, 0, 0)),
        out_shape=jax.ShapeDtypeStruct((bsz, n_new, c), F32),
        scratch_shapes=[pltpu.VMEM((bb, n_state + n_new + 6, c), F32)],
        compiler_params=_cparams(1),
        name="conv_decode",
    )(state, a3, w_dw, b_dw)


def _attn_prompt_body(lq1, lk1, lq2, lk2, subg_ref, q_ref, k_ref, v_ref, o_ref,
                      kt_ref, vb_ref, acc1, acc2, m1, l1, m2, l2, *, blk, sub, n_blk, lam_init):
    i = pl.program_id(2)

    def reset_stats():
        for m_ref, l_ref, acc_ref in ((m1, l1, acc1), (m2, l2, acc2)):
            m_ref[...] = jnp.full(m_ref.shape, NEG_BIG, F32)
            l_ref[...] = jnp.zeros(l_ref.shape, F32)
            acc_ref[...] = jnp.zeros(acc_ref.shape, F32)

    @pl.when(i == 0)
    def _():
        for c in range(n_blk):
            kt_ref[c] = k_ref[0, c * blk:(c + 1) * blk, :].T.astype(BF16)
            vb_ref[c] = v_ref[0, c * blk:(c + 1) * blk, :].astype(BF16)
        reset_stats()

    stats = ((m1, l1, acc1), (m2, l2, acc2))
    n_sub = blk // sub

    def scores(kb, sb, masked):
        r0 = sb * sub
        ncol = r0 + sub if masked else blk
        out = []
        for c in range(2):
            qq = q_ref[0, r0:r0 + sub, c * HEAD_DIM:(c + 1) * HEAD_DIM]
            kk = kt_ref[kb, c * HEAD_DIM:(c + 1) * HEAD_DIM, 0:ncol]
            out.append(jnp.dot(qq, kk, preferred_element_type=F32))
        return out

    def softmax_values(kb, sb, masked, s_pair):
        r0 = sb * sub
        rows = slice(r0, r0 + sub)
        ncol = r0 + sub if masked else blk
        ps, alphas = [], []
        for s, (m_ref, l_ref, _) in zip(s_pair, stats):
            if masked:
                row = lax.broadcasted_iota(I32, (sub, ncol), 0) + r0
                col = lax.broadcasted_iota(I32, (sub, ncol), 1)
                s = jnp.where(col <= row, s, NEG_BIG)
            m_old = m_ref[rows, :]
            m_new = jnp.maximum(m_old, jnp.max(s, axis=-1, keepdims=True))
            alpha = jnp.exp(m_old - m_new)
            p = jnp.exp(s - jnp.concatenate([m_new] * (ncol // LANES), axis=1))
            l_ref[rows, :] = alpha * l_ref[rows, :] + jnp.sum(p, axis=-1, keepdims=True)
            m_ref[rows, :] = m_new
            ps.append(p.astype(BF16))
            alphas.append(alpha)
        pv = jnp.dot(jnp.concatenate(ps, axis=0), vb_ref[kb, 0:ncol, :],
                     preferred_element_type=F32)
        for c, (alpha, (_, _, acc_ref)) in enumerate(zip(alphas, stats)):
            acc_ref[rows, :] = alpha * acc_ref[rows, :] + pv[c * sub:(c + 1) * sub]

    def run(chains):
        ahead = 2
        pending = {n: scores(*chains[n]) for n in range(min(ahead, len(chains)))}
        for n, chain in enumerate(chains):
            if n + ahead < len(chains):
                pending[n + ahead] = scores(*chains[n + ahead])
            softmax_values(*chain, pending.pop(n))

    def block(kb, masked):
        return [(kb, sb, masked) for sb in range(n_sub)]

    def pair_body(t, carry):
        run(block(2 * t, False) + block(2 * t + 1, False))
        return carry

    lax.fori_loop(0, lax.shift_right_logical(i, 1), pair_body, 0)
    odd = (i & 1) == 1

    def finish():
        lam = _lam(lq1, lk1, lq2, lk2, lam_init)
        o = acc1[...] / l1[...] - lam * (acc2[...] / l2[...])
        o_ref[0] = (_rms(o, subg_ref[...]) * (1.0 - lam_init)).astype(BF16)
        reset_stats()

    @pl.when(odd)
    def _():
        run(block(i - 1, False) + block(i, True))
        finish()

    @pl.when(jnp.logical_not(odd))
    def _():
        run(block(i, True))
        finish()


def _attn_prompt(q3, k3, v3, lams, subln_g, lam_init):
    bsz, seq, _ = q3.shape
    blk = ATTN_BLOCK
    n_blk = seq // blk
    vec = pl.BlockSpec((1, HEAD_DIM), lambda b, h, i: (0, 0))
    body = functools.partial(_attn_prompt_body, blk=blk, sub=ATTN_SUB, n_blk=n_blk,
                             lam_init=lam_init)
    return pl.pallas_call(
        body,
        grid=(bsz, N_HEADS, n_blk),
        in_specs=[
            vec, vec, vec, vec,
            pl.BlockSpec((1, HEAD_W), lambda b, h, i: (0, 0)),
            pl.BlockSpec((1, blk, HEAD_W), lambda b, h, i: (b, i, h)),
            pl.BlockSpec((1, seq, HEAD_W), lambda b, h, i: (b, 0, h)),
            pl.BlockSpec((1, seq, HEAD_W), lambda b, h, i: (b, 0, h)),
        ],
        out_specs=pl.BlockSpec((1, blk, HEAD_W), lambda b, h, i: (b, i, h)),
        out_shape=jax.ShapeDtypeStruct(q3.shape, BF16),
        scratch_shapes=[
            pltpu.VMEM((n_blk, HEAD_W, blk), BF16),
            pltpu.VMEM((n_blk, blk, HEAD_W), BF16),
            pltpu.VMEM((blk, HEAD_W), F32),
            pltpu.VMEM((blk, HEAD_W), F32),
            pltpu.VMEM((blk, LANES), F32),
            pltpu.VMEM((blk, LANES), F32),
            pltpu.VMEM((blk, LANES), F32),
            pltpu.VMEM((blk, LANES), F32),
        ],
        compiler_params=_cparams(3),
        name="attn_prompt",
    )(*lams, subln_g, q3, k3, v3)


NEW_PAD = 16
Q_PAD = 8
DECODE_PAGE_GROUP = 4


def _attn_decode_body(pt_ref, lq1, lk1, lq2, lk2, subg_ref, q_ref, kn_ref, vn_ref, ck_hbm,
                      cv_hbm, o_ref, kbuf, vbuf, s_ref, stage, sem, *, n_pages, n_new,
                      lam_init):
    b = pl.program_id(0)
    buf = lax.rem(b, 2)
    n_past = n_pages * PAGE_SIZE
    width = stage.shape[1]
    page_rows = PAGE_SIZE * N_HEADS

    def fetch(seq, half):
        def one_page(j, carry):
            src = pl.ds(pl.multiple_of(pt_ref[seq, j] * page_rows, page_rows), page_rows)
            dst = pl.ds(pl.multiple_of(j * page_rows, page_rows), page_rows)
            pltpu.make_async_copy(ck_hbm.at[src], kbuf.at[half, dst], sem.at[0, half]).start()
            pltpu.make_async_copy(cv_hbm.at[src], vbuf.at[half, dst], sem.at[1, half]).start()
            return carry
        lax.fori_loop(0, n_pages, one_page, 0)

    def wait_pages(hbm, dst_buf, which):
        pltpu.make_async_copy(hbm.at[pl.ds(0, n_pages * page_rows)], dst_buf.at[buf],
                              sem.at[which, buf]).wait()

    @pl.when(b == 0)
    def _():
        fetch(0, 0)

    @pl.when(b + 1 < pl.num_programs(0))
    def _():
        fetch(b + 1, 1 - buf)

    k_now = kbuf.at[buf]
    v_now = vbuf.at[buf]

    def padded_rows(rows_f32):
        stage[...] = jnp.zeros(stage.shape, F32)
        stage[0:n_new, :] = rows_f32
        return stage[...].astype(BF16)

    q16 = padded_rows(q_ref[0])
    sel_r = lax.broadcasted_iota(I32, (NEW_PAD, LANES), 0)
    sel_c = lax.broadcasted_iota(I32, (NEW_PAD, LANES), 1)
    sel = (sel_c % Q_PAD == sel_r).astype(BF16)
    qrep = lax.dot_general(q16, sel, (((0,), (0,)), ((), ())),
                           preferred_element_type=F32)
    rr = lax.broadcasted_iota(I32, (width, LANES), 0)
    cc = lax.broadcasted_iota(I32, (width, LANES), 1)
    qblk = jnp.where(rr // HEAD_DIM == cc // Q_PAD, qrep, 0.0).astype(BF16)

    def head_major(pages_ref, j):
        return jnp.concatenate(
            [pages_ref[pl.ds(j * page_rows + h, PAGE_SIZE, stride=N_HEADS), :].astype(BF16)
             for h in range(N_HEADS)], axis=-1)

    groups = [range(j0, min(j0 + DECODE_PAGE_GROUP, n_pages))
              for j0 in range(0, n_pages, DECODE_PAGE_GROUP)]
    row_slices = [slice(g[0] * PAGE_SIZE, (g[-1] + 1) * PAGE_SIZE) for g in groups]

    wait_pages(ck_hbm, kbuf, 0)
    for g, rows in zip(groups, row_slices):
        keys = jnp.concatenate([head_major(k_now, j) for j in g], axis=0)
        s_ref[rows, :] = jnp.dot(keys, qblk, preferred_element_type=F32)
    s_new = jnp.dot(padded_rows(kn_ref[0]), qblk, preferred_element_type=F32)
    new_idx = lax.broadcasted_iota(I32, (NEW_PAD, LANES), 0)
    slot = lax.broadcasted_iota(I32, (NEW_PAD, LANES), 1) % Q_PAD
    s_new = jnp.where((new_idx <= slot) & (new_idx < n_new), s_new, NEG_BIG)
    m = jnp.maximum(jnp.max(s_ref[0:n_past, :], axis=0, keepdims=True),
                    jnp.max(s_new, axis=0, keepdims=True))

    contract0 = (((0,), (0,)), ((), ()))
    e_new = jnp.exp(s_new - m)
    denom = jnp.sum(e_new, axis=0, keepdims=True)
    acc = lax.dot_general(e_new.astype(BF16), padded_rows(vn_ref[0]), contract0,
                          preferred_element_type=F32)
    wait_pages(cv_hbm, vbuf, 1)
    for g, rows in zip(groups, row_slices):
        e = jnp.exp(s_ref[rows, :] - m)
        denom = denom + jnp.sum(e, axis=0, keepdims=True)
        vals = jnp.concatenate([head_major(v_now, j) for j in g], axis=0)
        acc = acc + lax.dot_general(e.astype(BF16), vals, contract0,
                                    preferred_element_type=F32)

    r_i = lax.broadcasted_iota(I32, (LANES, LANES), 0)
    c_i = lax.broadcasted_iota(I32, (LANES, LANES), 1)
    denom_rows = jnp.sum(jnp.where(r_i == c_i, jnp.broadcast_to(denom, (LANES, LANES)), 0.0),
                         axis=1, keepdims=True)
    o_norm = acc / denom_rows
    lam = _lam(lq1, lk1, lq2, lk2, lam_init)
    outs = []
    for h in range(N_HEADS):
        r0 = h * 2 * Q_PAD
        cols = slice(h * HEAD_W, (h + 1) * HEAD_W)
        oh = o_norm[r0:r0 + Q_PAD, cols] - lam * o_norm[r0 + Q_PAD:r0 + 2 * Q_PAD, cols]
        outs.append(_rms(oh, subg_ref[...]) * (1.0 - lam_init))
    o_ref[0] = jnp.concatenate(outs, axis=-1)[:n_new]


def _attn_decode(page_table, q3, kn3, vn3, cache_k, cache_v, lams, subln_g, lam_init):
    bsz, n_new, width = q3.shape
    n_pages = page_table.shape[1]
    vec = pl.BlockSpec((1, HEAD_DIM), lambda b, pt: (0, 0))
    per_b = pl.BlockSpec((1, n_new, width), lambda b, pt: (b, 0, 0))

    body = functools.partial(_attn_decode_body, n_pages=n_pages, n_new=n_new,
                             lam_init=lam_init)
    hbm = pl.BlockSpec(memory_space=pl.ANY)
    past_rows = n_pages * PAGE_SIZE * N_HEADS
    grid_spec = pltpu.PrefetchScalarGridSpec(
        num_scalar_prefetch=1,
        grid=(bsz,),
        in_specs=[vec, vec, vec, vec, pl.BlockSpec((1, HEAD_W), lambda b, pt: (0, 0)),
                  per_b, per_b, per_b, hbm, hbm],
        out_specs=per_b,
        scratch_shapes=[pltpu.VMEM((2, past_rows, HEAD_W), F32),
                        pltpu.VMEM((2, past_rows, HEAD_W), F32),
                        pltpu.VMEM((n_pages * PAGE_SIZE, LANES), F32),
                        pltpu.VMEM((NEW_PAD, width), F32),
                        pltpu.SemaphoreType.DMA((2, 2))],
    )
    return pl.pallas_call(
        body,
        grid_spec=grid_spec,
        out_shape=jax.ShapeDtypeStruct(q3.shape, F32),
        compiler_params=_cparams(1),
        name="attn_decode",
    )(page_table, *lams, subln_g, q3, kn3, vn3, cache_k, cache_v)


def _post_body(x_ref, conv_ref, o_ref, gate_ref, lng, lnb, wc, wa, wo, gffn, wr, br, *rest,
               d_model):
    x1_ref, hp_ref, ids_ref, tw_ref = rest[-4:]
    n_groups = 2
    rows_per = x_ref.shape[0] // n_groups
    groups = [slice(g * rows_per, (g + 1) * rows_per) for g in range(n_groups)]
    attn_outs = [jnp.dot(o_ref[r, :], wa[...], preferred_element_type=F32) for r in groups]
    conv_outs = []
    for r in groups:
        c = conv_ref[r, :]
        mu = jnp.mean(c, axis=-1, keepdims=True)
        xc = c - mu
        cn = (xc * lax.rsqrt(jnp.mean(xc * xc, axis=-1, keepdims=True) + LN_EPS) * lng[...]
              + lnb[...])
        cact = (cn * jax.nn.sigmoid(cn)).astype(BF16)
        conv_outs.append(jnp.dot(cact, wc[...], preferred_element_type=F32))
    hbs = []
    for r, conv_out, attn_out in zip(groups, conv_outs, attn_outs):
        merged = (gate_ref[r, :d_model] * conv_out
                  + gate_ref[r, d_model:] * attn_out).astype(BF16)
        x1 = x_ref[r, :] + jnp.dot(merged, wo[...], preferred_element_type=F32)
        x1_ref[r, :] = x1
        hbs.append(_rms(x1, gffn[...]).astype(BF16))
    for r, hb in zip(groups, hbs):
        logits = jnp.dot(hb, wr[...], preferred_element_type=F32) + br[...]
        lane = lax.broadcasted_iota(I32, logits.shape, 1)
        vals, ids = [], []
        cur = logits
        for _ in range(TOP_K):
            mx = jnp.max(cur, axis=-1, keepdims=True)
            idx = jnp.min(jnp.where(cur == mx, lane, N_EXPERTS), axis=-1, keepdims=True)
            vals.append(mx)
            ids.append(idx)
            cur = jnp.where(lane == idx, -jnp.inf, cur)
        es = [jnp.exp(v - vals[0]) for v in vals]
        den = es[0] + es[1] + es[2] + es[3]
        tw_ref[r, :] = _columns([e / den for e in es])
        ids_ref[r, :] = _columns(ids)

        bits = pltpu.bitcast(hb.astype(F32), U32)
        half = d_model // 2
        words = (bits[:, :half] >> 16) | (bits[:, half:] & jnp.uint32(0xFFFF0000))
        pieces = half // LANES
        for c in range(pieces):
            hp_ref[pl.ds(r.start * pieces + c, rows_per, stride=pieces), :] = (
                words[:, c * LANES:(c + 1) * LANES])


def _post(x, conv, o, gate, ln_g, ln_b, wc, wa, wo, g_ffn, wr, br, tokens, first_token):
    t, d = x.shape
    tm = ROW_TILE
    first_tile = first_token // tm
    row = lambda i: (i, 0)
    const = lambda i: (0, 0)
    mat = pl.BlockSpec((d, d), const)
    vec = pl.BlockSpec((1, d), const)
    body = functools.partial(_post_body, d_model=d)
    in_specs = [
        pl.BlockSpec((tm, d), row), pl.BlockSpec((tm, d), row), pl.BlockSpec((tm, d), row),
        pl.BlockSpec((tm, 2 * d), row),
        vec, vec, mat, mat, mat, vec,
        pl.BlockSpec((d, N_EXPERTS), const), pl.BlockSpec((1, N_EXPERTS), const),
    ]
    in_specs.append(pl.BlockSpec(memory_space=pl.ANY))
    args = [x, conv, o, gate, ln_g, ln_b, wc, wa, wo, g_ffn, wr, br, tokens]
    return pl.pallas_call(
        body,
        grid=(t // tm,),
        in_specs=in_specs,
        out_specs=[
            pl.BlockSpec((tm, d), row),
            pl.BlockSpec((tm * (d // 2 // LANES), LANES), lambda i: (first_tile + i, 0)),
            pl.BlockSpec((tm, TOP_K), row), pl.BlockSpec((tm, TOP_K), row),
        ],
        out_shape=[
            jax.ShapeDtypeStruct((t, d), F32),
            jax.ShapeDtypeStruct(tokens.shape, U32),
            jax.ShapeDtypeStruct((t, TOP_K), I32), jax.ShapeDtypeStruct((t, TOP_K), F32),
        ],
        input_output_aliases={len(args) - 1: 1},
        compiler_params=_cparams(1),
        name="post",
    )(*args)


def _lane_cumsum(x):
    lane = lax.broadcasted_iota(I32, x.shape, 1)
    s = 1
    while s < LANES:
        x = x + jnp.where(lane >= s, pltpu.roll(x, s, 1), 0.0)
        s *= 2
    return x


def _wrap_i32(value):
    return (value + 2 ** 31) % 2 ** 32 - 2 ** 31


def _route_word(token, slot):
    dst = ((token // COMBINE_TILE) * (TOP_K * COMBINE_TILE) + slot * COMBINE_TILE
           + token % COMBINE_TILE)
    return jnp.left_shift(dst, ROUTE_TOKEN_BITS) | token


def _positions_body(ids_ref, pos_ref, word_ref, te_ref, count_ref, start_ref, *, n_tiles_pad):
    p = pl.program_id(0)
    i = pl.program_id(1)
    ids = ids_ref[...]
    tt = ids.shape[0]
    lane = lax.broadcasted_iota(I32, (tt, LANES), 1)
    onehots = [ids[:, k:k + 1] == lane for k in range(TOP_K)]
    tile_counts = [jnp.sum(oh.astype(F32), axis=0, keepdims=True) for oh in onehots]
    tile_total = tile_counts[0] + tile_counts[1] + tile_counts[2] + tile_counts[3]

    @pl.when((p == 0) & (i == 0))
    def _():
        count_ref[...] = jnp.zeros(count_ref.shape, F32)

    @pl.when((p == 1) & (i == 0))
    def _():
        counts = count_ref[...]
        padded = jnp.ceil(counts * (1.0 / MOE_TILE)) * MOE_TILE
        ends = _lane_cumsum(padded)
        start_ref[...] = ends - padded
        count_ref[...] = jnp.zeros(count_ref.shape, F32)
        tile_start = (lax.broadcasted_iota(I32, (n_tiles_pad, LANES), 0) * MOE_TILE).astype(F32)
        elane = lax.broadcasted_iota(I32, (n_tiles_pad, LANES), 1)
        done = (ends[0:1, :] <= tile_start) & (elane < N_EXPERTS)
        n_done = jnp.sum(done.astype(F32), axis=-1, keepdims=True)
        te_ref[...] = jnp.broadcast_to(n_done, (n_tiles_pad, LANES)).astype(I32)

    @pl.when(p == 1)
    def _():
        r = lax.broadcasted_iota(I32, (tt, tt), 0)
        c = lax.broadcasted_iota(I32, (tt, tt), 1)
        earlier = (c < r).astype(BF16)
        run = start_ref[0:1, :] + count_ref[0:1, :]
        cols = []
        for k in range(TOP_K):
            within = jnp.dot(earlier, onehots[k].astype(BF16), preferred_element_type=F32)
            cols.append(jnp.sum(jnp.where(onehots[k], within + run, 0.0),
                                axis=-1, keepdims=True))
            run = run + tile_counts[k]
        pos_ref[...] = _columns(cols).astype(I32)
        token = lax.broadcasted_iota(I32, (tt, TOP_K), 0) + i * tt
        slot = lax.broadcasted_iota(I32, (tt, TOP_K), 1)
        word_ref[...] = _route_word(token, slot)

    count_ref[...] = count_ref[...] + tile_total


def _positions(ids, n_tiles_pad):
    t = ids.shape[0]
    tt = POS_TILE
    body = functools.partial(_positions_body, n_tiles_pad=n_tiles_pad)
    return pl.pallas_call(
        body,
        grid=(2, t // tt),
        in_specs=[pl.BlockSpec((tt, TOP_K), lambda p, i: (i, 0))],
        out_specs=[
            pl.BlockSpec((tt, TOP_K), lambda p, i: (i * p, 0)),
            pl.BlockSpec((tt, TOP_K), lambda p, i: (i * p, 0)),
            pl.BlockSpec((n_tiles_pad, LANES), lambda p, i: (0, 0)),
        ],
        out_shape=[
            jax.ShapeDtypeStruct((t, TOP_K), I32),
            jax.ShapeDtypeStruct((t, TOP_K), I32),
            jax.ShapeDtypeStruct((n_tiles_pad, LANES), I32),
        ],
        scratch_shapes=[pltpu.VMEM((8, LANES), F32), pltpu.VMEM((8, LANES), F32)],
        compiler_params=_cparams(2),
        name="moe_positions",
    )(ids)


def _inverse_body(pos_ref, word_ref, init_hbm, inv_hbm, inv_smem, sem):
    c = pl.program_id(0)

    @pl.when(c == 0)
    def _():
        load = pltpu.make_async_copy(init_hbm, inv_smem, sem)
        load.start()
        load.wait()

    def place(a, carry):
        inv_smem[pos_ref[a]] = word_ref[a]
        return carry

    lax.fori_loop(0, pos_ref.shape[0], place, 0, unroll=32)

    @pl.when(c == pl.num_programs(0) - 1)
    def _():
        store = pltpu.make_async_copy(inv_smem, inv_hbm, sem)
        store.start()
        store.wait()


def _inverse(pos_flat, word_flat, init):
    n = pos_flat.shape[0]
    chunk = INVERSE_CHUNK
    assert n % chunk == 0
    smem_chunk = pl.BlockSpec((chunk,), lambda c: (c,), memory_space=pltpu.SMEM)
    return pl.pallas_call(
        _inverse_body,
        grid=(n // chunk,),
        in_specs=[smem_chunk, smem_chunk, pl.BlockSpec(memory_space=pl.ANY)],
        out_specs=pl.BlockSpec(memory_space=pl.ANY),
        out_shape=jax.ShapeDtypeStruct(init.shape, I32),
        scratch_shapes=[pltpu.SMEM(init.shape, I32), pltpu.SemaphoreType.DMA],
        compiler_params=_cparams(1),
        name="moe_inverse",
    )(pos_flat, word_flat, init)


def _experts_body(te_ref, nv_ref, grp_ref, nxt_ref, inv_next, inv_prev, b1_ref, b2_ref,
                  w1_hbm, w2_hbm, tok_hbm, out_hbm, xs_buf, y_buf, act_buf, w1f, w2f, w1b, w2b,
                  gsem, ssem, wsem, *, n_tok, n_out):
    j = pl.program_id(0)
    nv = nv_ref[0]

    def weight_copies(expert, wslot):
        return (pltpu.make_async_copy(w1_hbm.at[expert], w1f.at[wslot], wsem.at[wslot]),
                pltpu.make_async_copy(w2_hbm.at[expert], w2f.at[wslot], wsem.at[wslot]))
    d_ff, d = w2b.shape
    half = d // 2
    xp, yp = half // LANES, d // LANES
    tm = xs_buf.shape[1] // xp
    slot = lax.rem(j, 2)
    other = 1 - slot
    token_mask = (1 << ROUTE_TOKEN_BITS) - 1

    def gather_row(word, r, buf_slot):
        tok = pl.multiple_of((word & token_mask) * xp, xp)
        return pltpu.make_async_copy(tok_hbm.at[pl.ds(tok, xp)],
                                     xs_buf.at[buf_slot, pl.ds(r * xp, xp)], gsem)

    def scatter_row(word, r, buf_slot):
        dst = pl.multiple_of(lax.shift_right_logical(word, ROUTE_TOKEN_BITS) * yp, yp)
        return pltpu.make_async_copy(y_buf.at[buf_slot, pl.ds(r * yp, yp)],
                                     out_hbm.at[pl.ds(dst, yp)], ssem.at[buf_slot])

    def spare_word(r):
        return _wrap_i32((n_out + r) << ROUTE_TOKEN_BITS)

    def wait_gather(buf_slot):
        pltpu.make_async_copy(tok_hbm.at[pl.ds(0, tm * xp)], xs_buf.at[buf_slot], gsem).wait()

    def wait_scatter(buf_slot):
        pltpu.make_async_copy(y_buf.at[buf_slot], out_hbm.at[pl.ds(0, tm * yp)],
                              ssem.at[buf_slot]).wait()

    @pl.when(j == 0)
    def _():
        for copy in weight_copies(te_ref[0], 0):
            copy.start()
        y_buf[...] = jnp.zeros(y_buf.shape, F32)
        for r in range(tm):
            gather_row(inv_prev[0, 0, r], r, 0).start()
            scatter_row(spare_word(r), r, 0).start()

    @pl.when(j < nv)
    def _():
        wait_gather(slot)

        @pl.when((j == 0) | (te_ref[j] != te_ref[jnp.maximum(j - 1, 0)]))
        def _():
            wslot = lax.rem(grp_ref[j], 2)
            for copy in weight_copies(te_ref[j], wslot):
                copy.wait()

            @pl.when(nxt_ref[j] >= 0)
            def _():
                for copy in weight_copies(nxt_ref[j], 1 - wslot):
                    copy.start()

            for c in range(0, d, LANES):
                w1b[c:c + LANES, :] = w1f[wslot, c:c + LANES, :].astype(BF16)
            for c in range(0, d_ff, LANES):
                w2b[c:c + LANES, :] = w2f[wslot, c:c + LANES, :].astype(BF16)

        has_prev = j > 0

        def gather_rows(r0, r1):
            for r in range(r0, r1):
                gather_row(inv_next[0, 0, r], r, other).start(priority=1)

        def scatter_rows(r0, r1):
            for r in range(r0, r1):
                word = jnp.where(has_prev, inv_prev[0, 0, r], spare_word(r))
                scatter_row(word, r, other).start()

        xs_now = xs_buf.at[slot]
        bits = jnp.concatenate(
            [xs_now[pl.ds(c, tm, stride=xp), :] for c in range(xp)], axis=-1)
        x_lo = pltpu.bitcast(bits << 16, F32).astype(BF16)
        x_hi = pltpu.bitcast(bits & jnp.uint32(0xFFFF0000), F32).astype(BF16)
        y_now = y_buf.at[slot]
        ch = EXPERT_CHUNK

        def proj1(c0):
            return (jnp.dot(x_lo, w1b[:half, c0:c0 + ch], preferred_element_type=F32)
                    + jnp.dot(x_hi, w1b[half:, c0:c0 + ch], preferred_element_type=F32)
                    + b1_ref[0, :, c0:c0 + ch])

        rows_1 = tm // (d_ff // ch)
        for n, c in enumerate(range(0, d_ff, ch)):
            gate = jnp.minimum(proj1(c), SWIGLU_LIMIT)
            lin = jnp.clip(proj1(d_ff + c), -SWIGLU_LIMIT, SWIGLU_LIMIT)
            act_buf[:, c:c + ch] = (gate * jax.nn.sigmoid(SWIGLU_ALPHA * gate)
                                    * (lin + 1.0)).astype(BF16)
            gather_rows(n * rows_1, (n + 1) * rows_1)
        wait_scatter(slot)
        rows_2 = tm // (d // ch)
        for n, c in enumerate(range(0, d, ch)):
            y_cols = (jnp.dot(act_buf[...], w2b[:, c:c + ch], preferred_element_type=F32)
                      + b2_ref[0, :, c:c + ch])
            for g in range(ch // LANES):
                y_now[pl.ds(c // LANES + g, tm, stride=yp), :] = (
                    y_cols[:, g * LANES:(g + 1) * LANES])
            scatter_rows(n * rows_2, (n + 1) * rows_2)

    @pl.when(j == nv)
    def _():
        wait_gather(slot)
        wait_scatter(slot)
        for r in range(tm):
            scatter_row(inv_prev[0, 0, r], r, other).start()
        wait_scatter(other)


def _experts(tile_expert, n_valid, inv3, tokens, w1, b1, w2, b2, n_tok, n_out):
    nt, _, tm = inv3.shape
    d_ff, d = w2.shape[1:]
    half = d // 2
    xp, yp = half // LANES, d // LANES
    body = functools.partial(_experts_body, n_tok=n_tok, n_out=n_out)
    smem_tile = lambda index_map: pl.BlockSpec((1, 1, tm), index_map, memory_space=pltpu.SMEM)
    starts = jnp.concatenate([jnp.ones((1,), I32),
                              (tile_expert[1:] != tile_expert[:-1]).astype(I32)])
    group = jnp.cumsum(starts) - 1
    later = ((tile_expert[None, :] > tile_expert[:, None])
             & (jnp.arange(nt)[None, :] < n_valid[0]))
    next_expert = jnp.min(jnp.where(later, tile_expert[None, :], N_EXPERTS), axis=1)
    next_expert = jnp.where(next_expert < N_EXPERTS, next_expert, -1).astype(I32)
    hbm = pl.BlockSpec(memory_space=pl.ANY)
    grid_spec = pltpu.PrefetchScalarGridSpec(
        num_scalar_prefetch=4,
        grid=(nt,),
        in_specs=[
            smem_tile(lambda j, te, nv, grp, nxt: (jnp.minimum(j + 1, nt - 1), 0, 0)),
            smem_tile(lambda j, te, nv, grp, nxt: (jnp.maximum(j - 1, 0), 0, 0)),
            pl.BlockSpec((1, 1, 2 * d_ff), lambda j, te, nv, grp, nxt: (te[j], 0, 0)),
            pl.BlockSpec((1, 1, d), lambda j, te, nv, grp, nxt: (te[j], 0, 0)),
            hbm, hbm, hbm,
        ],
        out_specs=hbm,
        scratch_shapes=[
            pltpu.VMEM((2, tm * xp, LANES), U32),
            pltpu.VMEM((2, tm * yp, LANES), F32),
            pltpu.VMEM((tm, d_ff), BF16),
            pltpu.VMEM((2, d, 2 * d_ff), F32),
            pltpu.VMEM((2, d_ff, d), F32),
            pltpu.VMEM((d, 2 * d_ff), BF16),
            pltpu.VMEM((d_ff, d), BF16),
            pltpu.SemaphoreType.DMA,
            pltpu.SemaphoreType.DMA((2,)),
            pltpu.SemaphoreType.DMA((2,)),
        ],
    )
    return pl.pallas_call(
        body,
        grid_spec=grid_spec,
        out_shape=jax.ShapeDtypeStruct(((n_out + tm) * yp, LANES), F32),
        compiler_params=_cparams(1),
        name="moe_experts",
    )(tile_expert, n_valid, group.astype(I32), next_expert, inv3, inv3, b1, b2, w1, w2, tokens)


def _combine_body(tw_ref, x1_ref, gf_ref, rows_ref, y_ref):
    tt, d = x1_ref.shape
    yp = d // LANES
    tw = tw_ref[...]
    cols = []
    for c in range(yp):
        acc = None
        for k in range(TOP_K):
            piece = rows_ref[pl.ds(k * tt * yp + c, tt, stride=yp), :]
            term = tw[:, k:k + 1] * piece
            acc = term if acc is None else acc + term
        cols.append(acc)
    y_ref[...] = _rms(x1_ref[...] + jnp.concatenate(cols, axis=-1), gf_ref[...])


def _combine(tw, x1, g_final, expert_rows, first_tile):
    t, d = x1.shape
    tt = COMBINE_TILE
    return pl.pallas_call(
        _combine_body,
        grid=(t // tt,),
        in_specs=[
            pl.BlockSpec((tt, TOP_K), lambda i: (i, 0)),
            pl.BlockSpec((tt, d), lambda i: (i, 0)),
            pl.BlockSpec((1, d), lambda i: (0, 0)),
            pl.BlockSpec((TOP_K * tt * (d // LANES), LANES), lambda i: (first_tile + i, 0)),
        ],
        out_specs=pl.BlockSpec((tt, d), lambda i: (i, 0)),
        out_shape=jax.ShapeDtypeStruct((t, d), F32),
        compiler_params=_cparams(1),
        name="moe_combine",
    )(tw, x1, g_final, expert_rows)


def kernel(x_prompt, x_sample, cache_k, cache_v, state_conv, page_table, g_mix, w_in, b_in,
           w_dw, b_dw, ln_g, ln_b, w_conv_out, lam_q1, lam_k1, lam_q2, lam_k2, subln_g,
           w_attn_out, w_o, g_ffn, w_router, b_router, w_moe1, b_moe1, w_moe2, b_moe2, g_final):
    bsz, seq, d = x_prompt.shape
    dec_b, dec_s, _ = x_sample.shape
    depth = g_mix.shape[0]
    c_conv = w_dw.shape[2]
    attn_w = N_HEADS * HEAD_W
    n_pages = page_table.shape[1]
    past_len = n_pages * PAGE_SIZE
    t_p, t_s = bsz * seq, dec_b * dec_s
    t_all = t_p + t_s
    d_ff = w_moe2.shape[2]
    assert depth == 1, "the combine kernel fuses the final norm, so only one layer is supported"
    assert seq % ROW_TILE == 0 and t_s % ROW_TILE == 0 and seq % CONV_ROWS == 0
    assert seq % ATTN_BLOCK == 0 and t_all % POS_TILE == 0 and dec_b % 8 == 0
    assert t_p % COMBINE_TILE == 0 and t_s % COMBINE_TILE == 0

    tab_p = _rope_tables(jnp.arange(seq))
    tab_s = _rope_tables(jnp.tile(past_len + jnp.arange(dec_s), dec_b))
    n_rows = t_all * TOP_K + N_EXPERTS * MOE_TILE
    n_tiles = n_rows // MOE_TILE
    row2 = lambda v: v.reshape(1, -1)

    hp, hs = x_prompt.reshape(t_p, d), x_sample.reshape(t_s, d)
    outs = [[] for _ in range(6)]
    for l in range(depth):
        lam_init = 0.8 - 0.6 * math.exp(-0.3 * l)
        lams = (row2(lam_q1[l]), row2(lam_k1[l]), row2(lam_q2[l]), row2(lam_k2[l]))
        subg = row2(subln_g[l])
        w_in_bf = w_in[l].astype(BF16)
        wc, wa, wo = (w_conv_out[l].astype(BF16), w_attn_out[l].astype(BF16),
                      w_o[l].astype(BF16))
        wr = w_router[l].astype(BF16)
        b1 = b_moe1[l].reshape(N_EXPERTS, 1, 2 * d_ff)
        b2 = b_moe2[l].reshape(N_EXPERTS, 1, d)
        proj_args = (row2(g_mix[l]), w_in_bf, row2(b_in[l]))
        post_args = (row2(ln_g[l]), row2(ln_b[l]), wc, wa, wo, row2(g_ffn[l]), wr,
                     row2(b_router[l]))

        a_p, q_p, k_p, v_p, gate_p = _in_proj(hp, *proj_args, tab_p, seq // ROW_TILE,
                                              c_conv, attn_w)
        a_p3 = a_p.reshape(bsz, seq, c_conv)
        conv_p = _conv_prompt(a_p3, w_dw[l], row2(b_dw[l]))
        o_p = _attn_prompt(q_p.reshape(bsz, seq, attn_w), k_p.reshape(bsz, seq, attn_w),
                           v_p.reshape(bsz, seq, attn_w), lams, subg, lam_init)
        tokens = jnp.zeros(((t_all + PAD_TOKENS) * (d // 2 // LANES), LANES), U32)
        x1_p, tokens, ids_p, tw_p = _post(hp, conv_p.reshape(t_p, c_conv),
                                          o_p.reshape(t_p, attn_w), gate_p, *post_args,
                                          tokens, 0)

        a_s, q_s, k_s, v_s, gate_s = _in_proj(hs, *proj_args, tab_s, 1, c_conv, attn_w)
        a_s3 = a_s.reshape(dec_b, dec_s, c_conv)
        conv_s = _conv_decode(state_conv[l], a_s3, w_dw[l], row2(b_dw[l]))
        pool = cache_k.shape[1]
        o_s = _attn_decode(page_table, q_s.astype(F32).reshape(dec_b, dec_s, attn_w),
                           k_s.reshape(dec_b, dec_s, attn_w), v_s.reshape(dec_b, dec_s, attn_w),
                           cache_k[l].reshape(pool * PAGE_SIZE * N_HEADS, HEAD_W),
                           cache_v[l].reshape(pool * PAGE_SIZE * N_HEADS, HEAD_W),
                           lams, subg, lam_init)
        x1_s, tokens, ids_s, tw_s = _post(hs, conv_s.reshape(t_s, c_conv),
                                          o_s.reshape(t_s, attn_w).astype(BF16), gate_s,
                                          *post_args, tokens, t_p)

        ids = jnp.concatenate([ids_p, ids_s], axis=0)
        pos, word, te = _positions(ids, n_tiles)
        tile_expert = jnp.minimum(te[:, 0], N_EXPERTS - 1)
        n_valid = jnp.sum((te[:, 0] < N_EXPERTS).astype(I32)).reshape(1)
        slot_idx = jnp.arange(n_rows, dtype=U32)
        n_out = t_all * TOP_K
        unused = (((n_out + slot_idx % MOE_TILE) << ROUTE_TOKEN_BITS)
                  | (t_all + slot_idx % PAD_TOKENS))
        inv = _inverse(pos.reshape(-1), word.reshape(-1), lax.bitcast_convert_type(unused, I32))
        expert_rows = _experts(tile_expert, n_valid, inv.reshape(n_tiles, 1, MOE_TILE), tokens,
                               w_moe1[l], b1, w_moe2[l], b2, t_all, n_out)
        gf = row2(g_final)
        hp = _combine(tw_p, x1_p, gf, expert_rows, 0)
        hs = _combine(tw_s, x1_s, gf, expert_rows, t_p // COMBINE_TILE)

        outs[0].append(k_p.reshape(bsz, seq, N_HEADS, HEAD_W))
        outs[1].append(v_p.reshape(bsz, seq, N_HEADS, HEAD_W))
        outs[2].append(a_p3[:, seq - (CONV_WIDTH - 1):])
        outs[3].append(k_s.reshape(dec_b, dec_s, N_HEADS, HEAD_W))
        outs[4].append(v_s.reshape(dec_b, dec_s, N_HEADS, HEAD_W))
        outs[5].append(jnp.concatenate([state_conv[l], a_s3], axis=1)[:, -(CONV_WIDTH - 1):])

    y_prompt = hp.reshape(bsz, seq, d)
    y_sample = hs.reshape(dec_b, dec_s, d)
    return (y_prompt, y_sample) + tuple(jnp.stack(o) for o in outs)
```

```python
import functools
import math

import jax
import jax.numpy as jnp
from jax import lax
from jax.experimental import pallas as pl
from jax.experimental.pallas import tpu as pltpu

F32 = jnp.float32
BF16 = jnp.bfloat16
I32 = jnp.int32
U32 = jnp.uint32

N_HEADS = 8
HEAD_DIM = 64
HEAD_W = 2 * HEAD_DIM
ROT_DIM = HEAD_DIM // 4
ROT_HALF = ROT_DIM // 2
ROPE_THETA = 500000.0
CONV_WIDTH = 31
CONV_HALO = 32
N_EXPERTS = 32
TOP_K = 4
SWIGLU_ALPHA = 1.702
SWIGLU_LIMIT = 7.0
RMS_EPS = 1e-5
LN_EPS = 1e-5
PAGE_SIZE = 128
LANES = 128
NEG_BIG = -1e30

ROW_TILE = 512
ATTN_BLOCK = 1024
ATTN_SUB = 128
CONV_ROWS = 1024
CONV_CHUNK = 64
MOE_TILE = 256
EXPERT_CHUNK = 256
POS_TILE = 512
COMBINE_TILE = 256
ROUTE_TOKEN_BITS = 15
INVERSE_CHUNK = 11 * 1024
PAD_TOKENS = 64
VMEM_LIMIT = 56 * 1024 * 1024


def _cparams(n_axes, vmem=VMEM_LIMIT):
    return pltpu.CompilerParams(dimension_semantics=("arbitrary",) * n_axes,
                                vmem_limit_bytes=vmem)


def _rms(x, g):
    return x * lax.rsqrt(jnp.mean(x * x, axis=-1, keepdims=True) + RMS_EPS) * g


def _columns(cols):
    rows = cols[0].shape[0]
    lane = lax.broadcasted_iota(I32, (rows, len(cols)), 1)
    out = jnp.zeros((rows, len(cols)), cols[0].dtype)
    for k, col in enumerate(cols):
        out = jnp.where(lane == k, col, out)
    return out


def _lam(lq1, lk1, lq2, lk2, lam_init):
    s1 = jnp.sum(lq1[...] * lk1[...], axis=-1, keepdims=True)
    s2 = jnp.sum(lq2[...] * lk2[...], axis=-1, keepdims=True)
    return jnp.exp(s1) - jnp.exp(s2) + lam_init


def _rope_tables(pos):
    inv = jnp.power(jnp.float32(ROPE_THETA),
                    -jnp.arange(ROT_HALF, dtype=F32) * (2.0 / ROT_DIM))
    ang = pos.astype(F32)[:, None] * inv[None, :]
    cos, sin = jnp.cos(ang), jnp.sin(ang)
    n = pos.shape[0]
    rest = HEAD_DIM - ROT_DIM
    zh = jnp.zeros((n, ROT_HALF), F32)
    c64 = jnp.concatenate([cos, cos, jnp.ones((n, rest), F32)], axis=-1)
    sa64 = jnp.concatenate([-sin, zh, jnp.zeros((n, rest), F32)], axis=-1)
    sb64 = jnp.concatenate([zh, sin, jnp.zeros((n, rest), F32)], axis=-1)
    tile = lambda t: jnp.concatenate([t, t], axis=-1)
    return tile(c64), tile(sa64), tile(sb64)


def _in_proj_body(x_ref, g_ref, w_ref, b_ref, cos_ref, sa_ref, sb_ref,
                  a_ref, q_ref, k_ref, v_ref, gate_ref, *, c_conv, attn_w, d_model):
    h = _rms(x_ref[...], g_ref[...]).astype(BF16)
    cos, sa, sb = cos_ref[...], sa_ref[...], sb_ref[...]
    ch = 512

    def proj(c0):
        return (jnp.dot(h, w_ref[:, c0:c0 + ch], preferred_element_type=F32)
                + b_ref[:, c0:c0 + ch])

    def rope(z):
        outs = []
        for j in range(ch // LANES):
            zj = z[:, j * LANES:(j + 1) * LANES]
            outs.append(zj * cos + pltpu.roll(zj, LANES - ROT_HALF, 1) * sa
                        + pltpu.roll(zj, ROT_HALF, 1) * sb)
        return jnp.concatenate(outs, axis=-1)

    for c in range(0, c_conv, ch):
        a_ref[:, c:c + ch] = proj(c) * jax.nn.sigmoid(proj(c_conv + c))
    base = 2 * c_conv
    for c in range(0, attn_w, ch):
        q_ref[:, c:c + ch] = (rope(proj(base + c)) * (HEAD_DIM ** -0.5)).astype(BF16)
    base += attn_w
    for c in range(0, attn_w, ch):
        k_ref[:, c:c + ch] = rope(proj(base + c))
    base += attn_w
    for c in range(0, attn_w, ch):
        v_ref[:, c:c + ch] = proj(base + c)
    base += attn_w
    for c in range(0, 2 * d_model, ch):
        gate_ref[:, c:c + ch] = jax.nn.sigmoid(proj(base + c))


def _in_proj(x, g_mix, w_in_bf, b_in, tables, n_tab_blocks, c_conv, attn_w):
    t, d = x.shape
    n_in = w_in_bf.shape[1]
    tm = ROW_TILE
    row = lambda i: (i, 0)
    const = lambda i: (0, 0)
    tab = lambda i: (i % n_tab_blocks, 0)
    body = functools.partial(_in_proj_body, c_conv=c_conv, attn_w=attn_w, d_model=d)
    return pl.pallas_call(
        body,
        grid=(t // tm,),
        in_specs=[
            pl.BlockSpec((tm, d), row),
            pl.BlockSpec((1, d), const),
            pl.BlockSpec((d, n_in), const, pipeline_mode=pl.Buffered(1)),
            pl.BlockSpec((1, n_in), const),
            pl.BlockSpec((tm, LANES), tab),
            pl.BlockSpec((tm, LANES), tab),
            pl.BlockSpec((tm, LANES), tab),
        ],
        out_specs=[
            pl.BlockSpec((tm, c_conv), row),
            pl.BlockSpec((tm, attn_w), row),
            pl.BlockSpec((tm, attn_w), row),
            pl.BlockSpec((tm, attn_w), row),
            pl.BlockSpec((tm, 2 * d), row),
        ],
        out_shape=[
            jax.ShapeDtypeStruct((t, c_conv), F32),
            jax.ShapeDtypeStruct((t, attn_w), BF16),
            jax.ShapeDtypeStruct((t, attn_w), F32),
            jax.ShapeDtypeStruct((t, attn_w), F32),
            jax.ShapeDtypeStruct((t, 2 * d), F32),
        ],
        compiler_params=_cparams(1),
        name="in_proj",
    )(x, g_mix, w_in_bf, b_in, *tables)


def _conv_prompt_body(a_ref, halo_ref, w_ref, b_ref, o_ref, ext_ref, *, rows):
    i = pl.program_id(2)
    keep = jnp.where(i > 0, 1.0, 0.0).astype(F32)
    ext_ref[0:CONV_HALO, :] = halo_ref[0] * keep
    ext_ref[CONV_HALO:, :] = a_ref[0]
    w = w_ref[...]
    bias = jnp.broadcast_to(b_ref[...], (CONV_CHUNK, LANES))
    off = CONV_HALO - (CONV_WIDTH - 1)
    for r0 in range(0, rows, CONV_CHUNK):
        acc = bias
        for j in range(CONV_WIDTH):
            acc = acc + w[j:j + 1, :] * ext_ref[r0 + off + j:r0 + off + j + CONV_CHUNK, :]
        o_ref[0, r0:r0 + CONV_CHUNK, :] = acc


def _conv_prompt(a3, w_dw, b_dw):
    bsz, seq, c = a3.shape
    rows = CONV_ROWS
    hb = rows // CONV_HALO
    body = functools.partial(_conv_prompt_body, rows=rows)
    return pl.pallas_call(
        body,
        grid=(bsz, c // LANES, seq // rows),
        in_specs=[
            pl.BlockSpec((1, rows, LANES), lambda b, g, i: (b, i, g)),
            pl.BlockSpec((1, CONV_HALO, LANES),
                         lambda b, g, i: (b, jnp.maximum(i * hb - 1, 0), g)),
            pl.BlockSpec((CONV_WIDTH, LANES), lambda b, g, i: (0, g)),
            pl.BlockSpec((1, LANES), lambda b, g, i: (0, g)),
        ],
        out_specs=pl.BlockSpec((1, rows, LANES), lambda b, g, i: (b, i, g)),
        out_shape=jax.ShapeDtypeStruct((bsz, seq, c), F32),
        scratch_shapes=[pltpu.VMEM((rows + CONV_HALO, LANES), F32)],
        compiler_params=_cparams(3),
        name="conv_prompt",
    )(a3, a3, w_dw, b_dw)


def _conv_decode_body(state_ref, a_ref, w_ref, b_ref, o_ref, ext_ref, *, n_state, n_new):
    ext_ref[:, 0:n_state, :] = state_ref[...]
    ext_ref[:, n_state:n_state + n_new, :] = a_ref[...]
    w = w_ref[...]
    for t in range(n_new):
        win = ext_ref[:, t:t + CONV_WIDTH, :]
        o_ref[:, t:t + 1, :] = (jnp.sum(win * w[None], axis=1, keepdims=True)
                                + b_ref[...][None])


def _conv_decode(state, a3, w_dw, b_dw):
    bsz, n_state, c = state.shape
    n_new = a3.shape[1]
    bb = 8
    body = functools.partial(_conv_decode_body, n_state=n_state, n_new=n_new)
    return pl.pallas_call(
        body,
        grid=(bsz // bb,),
        in_specs=[
            pl.BlockSpec((bb, n_state, c), lambda i: (i, 0, 0)),
            pl.BlockSpec((bb, n_new, c), lambda i: (i, 0, 0)),
            pl.BlockSpec((CONV_WIDTH, c), lambda i: (0, 0)),
            pl.BlockSpec((1, c), lambda i: (0, 0)),
        ],
        out_specs=pl.BlockSpec((bb, n_new, c), lambda i: (i, 0, 0)),
        out_shape=jax.ShapeDtypeStruct((bsz, n_new, c), F32),
        scratch_shapes=[pltpu.VMEM((bb, n_state + n_new + 6, c), F32)],
        compiler_params=_cparams(1),
        name="conv_decode",
    )(state, a3, w_dw, b_dw)


def _attn_prompt_body(lq1, lk1, lq2, lk2, subg_ref, q_ref, k_ref, v_ref, o_ref,
                      kt_ref, vb_ref, acc1, acc2, m1, l1, m2, l2, *, blk, sub, n_blk, lam_init):
    i = pl.program_id(2)

    def reset_stats():
        for m_ref, l_ref, acc_ref in ((m1, l1, acc1), (m2, l2, acc2)):
            m_ref[...] = jnp.full(m_ref.shape, NEG_BIG, F32)
            l_ref[...] = jnp.zeros(l_ref.shape, F32)
            acc_ref[...] = jnp.zeros(acc_ref.shape, F32)

    @pl.when(i == 0)
    def _():
        for c in range(n_blk):
            kt_ref[c] = k_ref[0, c * blk:(c + 1) * blk, :].T.astype(BF16)
            vb_ref[c] = v_ref[0, c * blk:(c + 1) * blk, :].astype(BF16)
        reset_stats()

    stats = ((m1, l1, acc1), (m2, l2, acc2))
    n_sub = blk // sub

    def scores(kb, sb, masked):
        r0 = sb * sub
        ncol = r0 + sub if masked else blk
        out = []
        for c in range(2):
            qq = q_ref[0, r0:r0 + sub, c * HEAD_DIM:(c + 1) * HEAD_DIM]
            kk = kt_ref[kb, c * HEAD_DIM:(c + 1) * HEAD_DIM, 0:ncol]
            out.append(jnp.dot(qq, kk, preferred_element_type=F32))
        return out

    def softmax_values(kb, sb, masked, s_pair):
        r0 = sb * sub
        rows = slice(r0, r0 + sub)
        ncol = r0 + sub if masked else blk
        ps, alphas = [], []
        for s, (m_ref, l_ref, _) in zip(s_pair, stats):
            if masked:
                row = lax.broadcasted_iota(I32, (sub, ncol), 0) + r0
                col = lax.broadcasted_iota(I32, (sub, ncol), 1)
                s = jnp.where(col <= row, s, NEG_BIG)
            m_old = m_ref[rows, :]
            m_new = jnp.maximum(m_old, jnp.max(s, axis=-1, keepdims=True))
            alpha = jnp.exp(m_old - m_new)
            p = jnp.exp(s - jnp.concatenate([m_new] * (ncol // LANES), axis=1))
            l_ref[rows, :] = alpha * l_ref[rows, :] + jnp.sum(p, axis=-1, keepdims=True)
            m_ref[rows, :] = m_new
            ps.append(p.astype(BF16))
            alphas.append(alpha)
        pv = jnp.dot(jnp.concatenate(ps, axis=0), vb_ref[kb, 0:ncol, :],
                     preferred_element_type=F32)
        for c, (alpha, (_, _, acc_ref)) in enumerate(zip(alphas, stats)):
            acc_ref[rows, :] = alpha * acc_ref[rows, :] + pv[c * sub:(c + 1) * sub]

    def run(chains):
        ahead = 2
        pending = {n: scores(*chains[n]) for n in range(min(ahead, len(chains)))}
        for n, chain in enumerate(chains):
            if n + ahead < len(chains):
                pending[n + ahead] = scores(*chains[n + ahead])
            softmax_values(*chain, pending.pop(n))

    def block(kb, masked):
        return [(kb, sb, masked) for sb in range(n_sub)]

    def pair_body(t, carry):
        run(block(2 * t, False) + block(2 * t + 1, False))
        return carry

    lax.fori_loop(0, lax.shift_right_logical(i, 1), pair_body, 0)
    odd = (i & 1) == 1

    def finish():
        lam = _lam(lq1, lk1, lq2, lk2, lam_init)
        o = acc1[...] / l1[...] - lam * (acc2[...] / l2[...])
        o_ref[0] = (_rms(o, subg_ref[...]) * (1.0 - lam_init)).astype(BF16)
        reset_stats()

    @pl.when(odd)
    def _():
        run(block(i - 1, False) + block(i, True))
        finish()

    @pl.when(jnp.logical_not(odd))
    def _():
        run(block(i, True))
        finish()


def _attn_prompt(q3, k3, v3, lams, subln_g, lam_init):
    bsz, seq, _ = q3.shape
    blk = ATTN_BLOCK
    n_blk = seq // blk
    vec = pl.BlockSpec((1, HEAD_DIM), lambda b, h, i: (0, 0))
    body = functools.partial(_attn_prompt_body, blk=blk, sub=ATTN_SUB, n_blk=n_blk,
                             lam_init=lam_init)
    return pl.pallas_call(
        body,
        grid=(bsz, N_HEADS, n_blk),
        in_specs=[
            vec, vec, vec, vec,
            pl.BlockSpec((1, HEAD_W), lambda b, h, i: (0, 0)),
            pl.BlockSpec((1, blk, HEAD_W), lambda b, h, i: (b, i, h)),
            pl.BlockSpec((1, seq, HEAD_W), lambda b, h, i: (b, 0, h)),
            pl.BlockSpec((1, seq, HEAD_W), lambda b, h, i: (b, 0, h)),
        ],
        out_specs=pl.BlockSpec((1, blk, HEAD_W), lambda b, h, i: (b, i, h)),
        out_shape=jax.ShapeDtypeStruct(q3.shape, BF16),
        scratch_shapes=[
            pltpu.VMEM((n_blk, HEAD_W, blk), BF16),
            pltpu.VMEM((n_blk, blk, HEAD_W), BF16),
            pltpu.VMEM((blk, HEAD_W), F32),
            pltpu.VMEM((blk, HEAD_W), F32),
            pltpu.VMEM((blk, LANES), F32),
            pltpu.VMEM((blk, LANES), F32),
            pltpu.VMEM((blk, LANES), F32),
            pltpu.VMEM((blk, LANES), F32),
        ],
        compiler_params=_cparams(3),
        name="attn_prompt",
    )(*lams, subln_g, q3, k3, v3)


NEW_PAD = 16
Q_PAD = 8
DECODE_PAGE_GROUP = 4


def _attn_decode_body(pt_ref, lq1, lk1, lq2, lk2, subg_ref, q_ref, kn_ref, vn_ref, ck_hbm,
                      cv_hbm, o_ref, kbuf, vbuf, s_ref, stage, sem, *, n_pages, n_new,
                      lam_init):
    b = pl.program_id(0)
    buf = lax.rem(b, 2)
    n_past = n_pages * PAGE_SIZE
    width = stage.shape[1]
    page_rows = PAGE_SIZE * N_HEADS

    def fetch(seq, half):
        def one_page(j, carry):
            src = pl.ds(pl.multiple_of(pt_ref[seq, j] * page_rows, page_rows), page_rows)
            dst = pl.ds(pl.multiple_of(j * page_rows, page_rows), page_rows)
            pltpu.make_async_copy(ck_hbm.at[src], kbuf.at[half, dst], sem.at[0, half]).start()
            pltpu.make_async_copy(cv_hbm.at[src], vbuf.at[half, dst], sem.at[1, half]).start()
            return carry
        lax.fori_loop(0, n_pages, one_page, 0)

    def wait_pages(hbm, dst_buf, which):
        pltpu.make_async_copy(hbm.at[pl.ds(0, n_pages * page_rows)], dst_buf.at[buf],
                              sem.at[which, buf]).wait()

    @pl.when(b == 0)
    def _():
        fetch(0, 0)

    @pl.when(b + 1 < pl.num_programs(0))
    def _():
        fetch(b + 1, 1 - buf)

    k_now = kbuf.at[buf]
    v_now = vbuf.at[buf]

    def padded_rows(rows_f32):
        stage[...] = jnp.zeros(stage.shape, F32)
        stage[0:n_new, :] = rows_f32
        return stage[...].astype(BF16)

    q16 = padded_rows(q_ref[0])
    sel_r = lax.broadcasted_iota(I32, (NEW_PAD, LANES), 0)
    sel_c = lax.broadcasted_iota(I32, (NEW_PAD, LANES), 1)
    sel = (sel_c % Q_PAD == sel_r).astype(BF16)
    qrep = lax.dot_general(q16, sel, (((0,), (0,)), ((), ())),
                           preferred_element_type=F32)
    rr = lax.broadcasted_iota(I32, (width, LANES), 0)
    cc = lax.broadcasted_iota(I32, (width, LANES), 1)
    qblk = jnp.where(rr // HEAD_DIM == cc // Q_PAD, qrep, 0.0).astype(BF16)

    def head_major(pages_ref, j):
        return jnp.concatenate(
            [pages_ref[pl.ds(j * page_rows + h, PAGE_SIZE, stride=N_HEADS), :].astype(BF16)
             for h in range(N_HEADS)], axis=-1)

    groups = [range(j0, min(j0 + DECODE_PAGE_GROUP, n_pages))
              for j0 in range(0, n_pages, DECODE_PAGE_GROUP)]
    row_slices = [slice(g[0] * PAGE_SIZE, (g[-1] + 1) * PAGE_SIZE) for g in groups]

    wait_pages(ck_hbm, kbuf, 0)
    for g, rows in zip(groups, row_slices):
        keys = jnp.concatenate([head_major(k_now, j) for j in g], axis=0)
        s_ref[rows, :] = jnp.dot(keys, qblk, preferred_element_type=F32)
    s_new = jnp.dot(padded_rows(kn_ref[0]), qblk, preferred_element_type=F32)
    new_idx = lax.broadcasted_iota(I32, (NEW_PAD, LANES), 0)
    slot = lax.broadcasted_iota(I32, (NEW_PAD, LANES), 1) % Q_PAD
    s_new = jnp.where((new_idx <= slot) & (new_idx < n_new), s_new, NEG_BIG)
    m = jnp.maximum(jnp.max(s_ref[0:n_past, :], axis=0, keepdims=True),
                    jnp.max(s_new, axis=0, keepdims=True))

    contract0 = (((0,), (0,)), ((), ()))
    e_new = jnp.exp(s_new - m)
    denom = jnp.sum(e_new, axis=0, keepdims=True)
    acc = lax.dot_general(e_new.astype(BF16), padded_rows(vn_ref[0]), contract0,
                          preferred_element_type=F32)
    wait_pages(cv_hbm, vbuf, 1)
    for g, rows in zip(groups, row_slices):
        e = jnp.exp(s_ref[rows, :] - m)
        denom = denom + jnp.sum(e, axis=0, keepdims=True)
        vals = jnp.concatenate([head_major(v_now, j) for j in g], axis=0)
        acc = acc + lax.dot_general(e.astype(BF16), vals, contract0,
                                    preferred_element_type=F32)

    r_i = lax.broadcasted_iota(I32, (LANES, LANES), 0)
    c_i = lax.broadcasted_iota(I32, (LANES, LANES), 1)
    denom_rows = jnp.sum(jnp.where(r_i == c_i, jnp.broadcast_to(denom, (LANES, LANES)), 0.0),
                         axis=1, keepdims=True)
    o_norm = acc / denom_rows
    lam = _lam(lq1, lk1, lq2, lk2, lam_init)
    outs = []
    for h in range(N_HEADS):
        r0 = h * 2 * Q_PAD
        cols = slice(h * HEAD_W, (h + 1) * HEAD_W)
        oh = o_norm[r0:r0 + Q_PAD, cols] - lam * o_norm[r0 + Q_PAD:r0 + 2 * Q_PAD, cols]
        outs.append(_rms(oh, subg_ref[...]) * (1.0 - lam_init))
    o_ref[0] = jnp.concatenate(outs, axis=-1)[:n_new]


def _attn_decode(page_table, q3, kn3, vn3, cache_k, cache_v, lams, subln_g, lam_init):
    bsz, n_new, width = q3.shape
    n_pages = page_table.shape[1]
    vec = pl.BlockSpec((1, HEAD_DIM), lambda b, pt: (0, 0))
    per_b = pl.BlockSpec((1, n_new, width), lambda b, pt: (b, 0, 0))

    body = functools.partial(_attn_decode_body, n_pages=n_pages, n_new=n_new,
                             lam_init=lam_init)
    hbm = pl.BlockSpec(memory_space=pl.ANY)
    past_rows = n_pages * PAGE_SIZE * N_HEADS
    grid_spec = pltpu.PrefetchScalarGridSpec(
        num_scalar_prefetch=1,
        grid=(bsz,),
        in_specs=[vec, vec, vec, vec, pl.BlockSpec((1, HEAD_W), lambda b, pt: (0, 0)),
                  per_b, per_b, per_b, hbm, hbm],
        out_specs=per_b,
        scratch_shapes=[pltpu.VMEM((2, past_rows, HEAD_W), F32),
                        pltpu.VMEM((2, past_rows, HEAD_W), F32),
                        pltpu.VMEM((n_pages * PAGE_SIZE, LANES), F32),
                        pltpu.VMEM((NEW_PAD, width), F32),
                        pltpu.SemaphoreType.DMA((2, 2))],
    )
    return pl.pallas_call(
        body,
        grid_spec=grid_spec,
        out_shape=jax.ShapeDtypeStruct(q3.shape, F32),
        compiler_params=_cparams(1),
        name="attn_decode",
    )(page_table, *lams, subln_g, q3, kn3, vn3, cache_k, cache_v)


def _post_body(x_ref, conv_ref, o_ref, gate_ref, lng, lnb, wc, wa, wo, gffn, wr, br, *rest,
               d_model):
    x1_ref, hp_ref, ids_ref, tw_ref = rest[-4:]
    n_groups = 2
    rows_per = x_ref.shape[0] // n_groups
    groups = [slice(g * rows_per, (g + 1) * rows_per) for g in range(n_groups)]
    attn_outs = [jnp.dot(o_ref[r, :], wa[...], preferred_element_type=F32) for r in groups]
    conv_outs = []
    for r in groups:
        c = conv_ref[r, :]
        mu = jnp.mean(c, axis=-1, keepdims=True)
        xc = c - mu
        cn = (xc * lax.rsqrt(jnp.mean(xc * xc, axis=-1, keepdims=True) + LN_EPS) * lng[...]
              + lnb[...])
        cact = (cn * jax.nn.sigmoid(cn)).astype(BF16)
        conv_outs.append(jnp.dot(cact, wc[...], preferred_element_type=F32))
    hbs = []
    for r, conv_out, attn_out in zip(groups, conv_outs, attn_outs):
        merged = (gate_ref[r, :d_model] * conv_out
                  + gate_ref[r, d_model:] * attn_out).astype(BF16)
        x1 = x_ref[r, :] + jnp.dot(merged, wo[...], preferred_element_type=F32)
        x1_ref[r, :] = x1
        hbs.append(_rms(x1, gffn[...]).astype(BF16))
    for r, hb in zip(groups, hbs):
        logits = jnp.dot(hb, wr[...], preferred_element_type=F32) + br[...]
        lane = lax.broadcasted_iota(I32, logits.shape, 1)
        vals, ids = [], []
        cur = logits
        for _ in range(TOP_K):
            mx = jnp.max(cur, axis=-1, keepdims=True)
            idx = jnp.min(jnp.where(cur == mx, lane, N_EXPERTS), axis=-1, keepdims=True)
            vals.append(mx)
            ids.append(idx)
            cur = jnp.where(lane == idx, -jnp.inf, cur)
        es = [jnp.exp(v - vals[0]) for v in vals]
        den = es[0] + es[1] + es[2] + es[3]
        tw_ref[r, :] = _columns([e / den for e in es])
        ids_ref[r, :] = _columns(ids)

        bits = pltpu.bitcast(hb.astype(F32), U32)
        half = d_model // 2
        words = (bits[:, :half] >> 16) | (bits[:, half:] & jnp.uint32(0xFFFF0000))
        pieces = half // LANES
        for c in range(pieces):
            hp_ref[pl.ds(r.start * pieces + c, rows_per, stride=pieces), :] = (
                words[:, c * LANES:(c + 1) * LANES])


def _post(x, conv, o, gate, ln_g, ln_b, wc, wa, wo, g_ffn, wr, br, tokens, first_token):
    t, d = x.shape
    tm = ROW_TILE
    first_tile = first_token // tm
    row = lambda i: (i, 0)
    const = lambda i: (0, 0)
    mat = pl.BlockSpec((d, d), const)
    vec = pl.BlockSpec((1, d), const)
    body = functools.partial(_post_body, d_model=d)
    in_specs = [
        pl.BlockSpec((tm, d), row), pl.BlockSpec((tm, d), row), pl.BlockSpec((tm, d), row),
        pl.BlockSpec((tm, 2 * d), row),
        vec, vec, mat, mat, mat, vec,
        pl.BlockSpec((d, N_EXPERTS), const), pl.BlockSpec((1, N_EXPERTS), const),
    ]
    in_specs.append(pl.BlockSpec(memory_space=pl.ANY))
    args = [x, conv, o, gate, ln_g, ln_b, wc, wa, wo, g_ffn, wr, br, tokens]
    return pl.pallas_call(
        body,
        grid=(t // tm,),
        in_specs=in_specs,
        out_specs=[
            pl.BlockSpec((tm, d), row),
            pl.BlockSpec((tm * (d // 2 // LANES), LANES), lambda i: (first_tile + i, 0)),
            pl.BlockSpec((tm, TOP_K), row), pl.BlockSpec((tm, TOP_K), row),
        ],
        out_shape=[
            jax.ShapeDtypeStruct((t, d), F32),
            jax.ShapeDtypeStruct(tokens.shape, U32),
            jax.ShapeDtypeStruct((t, TOP_K), I32), jax.ShapeDtypeStruct((t, TOP_K), F32),
        ],
        input_output_aliases={len(args) - 1: 1},
        compiler_params=_cparams(1),
        name="post",
    )(*args)


def _lane_cumsum(x):
    lane = lax.broadcasted_iota(I32, x.shape, 1)
    s = 1
    while s < LANES:
        x = x + jnp.where(lane >= s, pltpu.roll(x, s, 1), 0.0)
        s *= 2
    return x


def _wrap_i32(value):
    return (value + 2 ** 31) % 2 ** 32 - 2 ** 31


def _route_word(token, slot):
    dst = ((token // COMBINE_TILE) * (TOP_K * COMBINE_TILE) + slot * COMBINE_TILE
           + token % COMBINE_TILE)
    return jnp.left_shift(dst, ROUTE_TOKEN_BITS) | token


def _positions_body(ids_ref, pos_ref, word_ref, te_ref, count_ref, start_ref, *, n_tiles_pad):
    p = pl.program_id(0)
    i = pl.program_id(1)
    ids = ids_ref[...]
    tt = ids.shape[0]
    lane = lax.broadcasted_iota(I32, (tt, LANES), 1)
    onehots = [ids[:, k:k + 1] == lane for k in range(TOP_K)]
    tile_counts = [jnp.sum(oh.astype(F32), axis=0, keepdims=True) for oh in onehots]
    tile_total = tile_counts[0] + tile_counts[1] + tile_counts[2] + tile_counts[3]

    @pl.when((p == 0) & (i == 0))
    def _():
        count_ref[...] = jnp.zeros(count_ref.shape, F32)

    @pl.when((p == 1) & (i == 0))
    def _():
        counts = count_ref[...]
        padded = jnp.ceil(counts * (1.0 / MOE_TILE)) * MOE_TILE
        ends = _lane_cumsum(padded)
        start_ref[...] = ends - padded
        count_ref[...] = jnp.zeros(count_ref.shape, F32)
        tile_start = (lax.broadcasted_iota(I32, (n_tiles_pad, LANES), 0) * MOE_TILE).astype(F32)
        elane = lax.broadcasted_iota(I32, (n_tiles_pad, LANES), 1)
        done = (ends[0:1, :] <= tile_start) & (elane < N_EXPERTS)
        n_done = jnp.sum(done.astype(F32), axis=-1, keepdims=True)
        te_ref[...] = jnp.broadcast_to(n_done, (n_tiles_pad, LANES)).astype(I32)

    @pl.when(p == 1)
    def _():
        r = lax.broadcasted_iota(I32, (tt, tt), 0)
        c = lax.broadcasted_iota(I32, (tt, tt), 1)
        earlier = (c < r).astype(BF16)
        run = start_ref[0:1, :] + count_ref[0:1, :]
        cols = []
        for k in range(TOP_K):
            within = jnp.dot(earlier, onehots[k].astype(BF16), preferred_element_type=F32)
            cols.append(jnp.sum(jnp.where(onehots[k], within + run, 0.0),
                                axis=-1, keepdims=True))
            run = run + tile_counts[k]
        pos_ref[...] = _columns(cols).astype(I32)
        token = lax.broadcasted_iota(I32, (tt, TOP_K), 0) + i * tt
        slot = lax.broadcasted_iota(I32, (tt, TOP_K), 1)
        word_ref[...] = _route_word(token, slot)

    count_ref[...] = count_ref[...] + tile_total


def _positions(ids, n_tiles_pad):
    t = ids.shape[0]
    tt = POS_TILE
    body = functools.partial(_positions_body, n_tiles_pad=n_tiles_pad)
    return pl.pallas_call(
        body,
        grid=(2, t // tt),
        in_specs=[pl.BlockSpec((tt, TOP_K), lambda p, i: (i, 0))],
        out_specs=[
            pl.BlockSpec((tt, TOP_K), lambda p, i: (i * p, 0)),
            pl.BlockSpec((tt, TOP_K), lambda p, i: (i * p, 0)),
            pl.BlockSpec((n_tiles_pad, LANES), lambda p, i: (0, 0)),
        ],
        out_shape=[
            jax.ShapeDtypeStruct((t, TOP_K), I32),
            jax.ShapeDtypeStruct((t, TOP_K), I32),
            jax.ShapeDtypeStruct((n_tiles_pad, LANES), I32),
        ],
        scratch_shapes=[pltpu.VMEM((8, LANES), F32), pltpu.VMEM((8, LANES), F32)],
        compiler_params=_cparams(2),
        name="moe_positions",
    )(ids)


def _inverse_body(pos_ref, word_ref, init_hbm, inv_hbm, inv_smem, sem):
    c = pl.program_id(0)

    @pl.when(c == 0)
    def _():
        load = pltpu.make_async_copy(init_hbm, inv_smem, sem)
        load.start()
        load.wait()

    def place(a, carry):
        inv_smem[pos_ref[a]] = word_ref[a]
        return carry

    lax.fori_loop(0, pos_ref.shape[0], place, 0, unroll=32)

    @pl.when(c == pl.num_programs(0) - 1)
    def _():
        store = pltpu.make_async_copy(inv_smem, inv_hbm, sem)
        store.start()
        store.wait()


def _inverse(pos_flat, word_flat, init):
    n = pos_flat.shape[0]
    chunk = INVERSE_CHUNK
    assert n % chunk == 0
    smem_chunk = pl.BlockSpec((chunk,), lambda c: (c,), memory_space=pltpu.SMEM)
    return pl.pallas_call(
        _inverse_body,
        grid=(n // chunk,),
        in_specs=[smem_chunk, smem_chunk, pl.BlockSpec(memory_space=pl.ANY)],
        out_specs=pl.BlockSpec(memory_space=pl.ANY),
        out_shape=jax.ShapeDtypeStruct(init.shape, I32),
        scratch_shapes=[pltpu.SMEM(init.shape, I32), pltpu.SemaphoreType.DMA],
        compiler_params=_cparams(1),
        name="moe_inverse",
    )(pos_flat, word_flat, init)


def _experts_body(te_ref, nv_ref, grp_ref, nxt_ref, inv_next, inv_prev, b1_ref, b2_ref,
                  w1_hbm, w2_hbm, tok_hbm, out_hbm, xs_buf, y_buf, act_buf, w1f, w2f, w1b, w2b,
                  gsem, ssem, wsem, *, n_tok, n_out):
    j = pl.program_id(0)
    nv = nv_ref[0]

    def weight_copies(expert, wslot):
        return (pltpu.make_async_copy(w1_hbm.at[expert], w1f.at[wslot], wsem.at[wslot]),
                pltpu.make_async_copy(w2_hbm.at[expert], w2f.at[wslot], wsem.at[wslot]))
    d_ff, d = w2b.shape
    half = d // 2
    xp, yp = half // LANES, d // LANES
    tm = xs_buf.shape[1] // xp
    slot = lax.rem(j, 2)
    other = 1 - slot
    token_mask = (1 << ROUTE_TOKEN_BITS) - 1

    def gather_row(word, r, buf_slot):
        tok = pl.multiple_of((word & token_mask) * xp, xp)
        return pltpu.make_async_copy(tok_hbm.at[pl.ds(tok, xp)],
                                     xs_buf.at[buf_slot, pl.ds(r * xp, xp)], gsem)

    def scatter_row(word, r, buf_slot):
        dst = pl.multiple_of(lax.shift_right_logical(word, ROUTE_TOKEN_BITS) * yp, yp)
        return pltpu.make_async_copy(y_buf.at[buf_slot, pl.ds(r * yp, yp)],
                                     out_hbm.at[pl.ds(dst, yp)], ssem.at[buf_slot])

    def spare_word(r):
        return _wrap_i32((n_out + r) << ROUTE_TOKEN_BITS)

    def wait_gather(buf_slot):
        pltpu.make_async_copy(tok_hbm.at[pl.ds(0, tm * xp)], xs_buf.at[buf_slot], gsem).wait()

    def wait_scatter(buf_slot):
        pltpu.make_async_copy(y_buf.at[buf_slot], out_hbm.at[pl.ds(0, tm * yp)],
                              ssem.at[buf_slot]).wait()

    @pl.when(j == 0)
    def _():
        for copy in weight_copies(te_ref[0], 0):
            copy.start()
        y_buf[...] = jnp.zeros(y_buf.shape, F32)
        for r in range(tm):
            gather_row(inv_prev[0, 0, r], r, 0).start()
            scatter_row(spare_word(r), r, 0).start()

    @pl.when(j < nv)
    def _():
        wait_gather(slot)

        @pl.when((j == 0) | (te_ref[j] != te_ref[jnp.maximum(j - 1, 0)]))
        def _():
            wslot = lax.rem(grp_ref[j], 2)
            for copy in weight_copies(te_ref[j], wslot):
                copy.wait()

            @pl.when(nxt_ref[j] >= 0)
            def _():
                for copy in weight_copies(nxt_ref[j], 1 - wslot):
                    copy.start()

            for c in range(0, d, LANES):
                w1b[c:c + LANES, :] = w1f[wslot, c:c + LANES, :].astype(BF16)
            for c in range(0, d_ff, LANES):
                w2b[c:c + LANES, :] = w2f[wslot, c:c + LANES, :].astype(BF16)

        has_prev = j > 0

        def gather_rows(r0, r1):
            for r in range(r0, r1):
                gather_row(inv_next[0, 0, r], r, other).start(priority=1)

        def scatter_rows(r0, r1):
            for r in range(r0, r1):
                word = jnp.where(has_prev, inv_prev[0, 0, r], spare_word(r))
                scatter_row(word, r, other).start()

        xs_now = xs_buf.at[slot]
        bits = jnp.concatenate(
            [xs_now[pl.ds(c, tm, stride=xp), :] for c in range(xp)], axis=-1)
        x_lo = pltpu.bitcast(bits << 16, F32).astype(BF16)
        x_hi = pltpu.bitcast(bits & jnp.uint32(0xFFFF0000), F32).astype(BF16)
        y_now = y_buf.at[slot]
        ch = EXPERT_CHUNK

        def proj1(c0):
            return (jnp.dot(x_lo, w1b[:half, c0:c0 + ch], preferred_element_type=F32)
                    + jnp.dot(x_hi, w1b[half:, c0:c0 + ch], preferred_element_type=F32)
                    + b1_ref[0, :, c0:c0 + ch])

        rows_1 = tm // (d_ff // ch)
        for n, c in enumerate(range(0, d_ff, ch)):
            gate = jnp.minimum(proj1(c), SWIGLU_LIMIT)
            lin = jnp.clip(proj1(d_ff + c), -SWIGLU_LIMIT, SWIGLU_LIMIT)
            act_buf[:, c:c + ch] = (gate * jax.nn.sigmoid(SWIGLU_ALPHA * gate)
                                    * (lin + 1.0)).astype(BF16)
            gather_rows(n * rows_1, (n + 1) * rows_1)
        wait_scatter(slot)
        rows_2 = tm // (d // ch)
        for n, c in enumerate(range(0, d, ch)):
            y_cols = (jnp.dot(act_buf[...], w2b[:, c:c + ch], preferred_element_type=F32)
                      + b2_ref[0, :, c:c + ch])
            for g in range(ch // LANES):
                y_now[pl.ds(c // LANES + g, tm, stride=yp), :] = (
                    y_cols[:, g * LANES:(g + 1) * LANES])
            scatter_rows(n * rows_2, (n + 1) * rows_2)

    @pl.when(j == nv)
    def _():
        wait_gather(slot)
        wait_scatter(slot)
        for r in range(tm):
            scatter_row(inv_prev[0, 0, r], r, other).start()
        wait_scatter(other)


def _experts(tile_expert, n_valid, inv3, tokens, w1, b1, w2, b2, n_tok, n_out):
    nt, _, tm = inv3.shape
    d_ff, d = w2.shape[1:]
    half = d // 2
    xp, yp = half // LANES, d // LANES
    body = functools.partial(_experts_body, n_tok=n_tok, n_out=n_out)
    smem_tile = lambda index_map: pl.BlockSpec((1, 1, tm), index_map, memory_space=pltpu.SMEM)
    starts = jnp.concatenate([jnp.ones((1,), I32),
                              (tile_expert[1:] != tile_expert[:-1]).astype(I32)])
    group = jnp.cumsum(starts) - 1
    later = ((tile_expert[None, :] > tile_expert[:, None])
             & (jnp.arange(nt)[None, :] < n_valid[0]))
    next_expert = jnp.min(jnp.where(later, tile_expert[None, :], N_EXPERTS), axis=1)
    next_expert = jnp.where(next_expert < N_EXPERTS, next_expert, -1).astype(I32)
    hbm = pl.BlockSpec(memory_space=pl.ANY)
    grid_spec = pltpu.PrefetchScalarGridSpec(
        num_scalar_prefetch=4,
        grid=(nt,),
        in_specs=[
            smem_tile(lambda j, te, nv, grp, nxt: (jnp.minimum(j + 1, nt - 1), 0, 0)),
            smem_tile(lambda j, te, nv, grp, nxt: (jnp.maximum(j - 1, 0), 0, 0)),
            pl.BlockSpec((1, 1, 2 * d_ff), lambda j, te, nv, grp, nxt: (te[j], 0, 0)),
            pl.BlockSpec((1, 1, d), lambda j, te, nv, grp, nxt: (te[j], 0, 0)),
            hbm, hbm, hbm,
        ],
        out_specs=hbm,
        scratch_shapes=[
            pltpu.VMEM((2, tm * xp, LANES), U32),
            pltpu.VMEM((2, tm * yp, LANES), F32),
            pltpu.VMEM((tm, d_ff), BF16),
            pltpu.VMEM((2, d, 2 * d_ff), F32),
            pltpu.VMEM((2, d_ff, d), F32),
            pltpu.VMEM((d, 2 * d_ff), BF16),
            pltpu.VMEM((d_ff, d), BF16),
            pltpu.SemaphoreType.DMA,
            pltpu.SemaphoreType.DMA((2,)),
            pltpu.SemaphoreType.DMA((2,)),
        ],
    )
    return pl.pallas_call(
        body,
        grid_spec=grid_spec,
        out_shape=jax.ShapeDtypeStruct(((n_out + tm) * yp, LANES), F32),
        compiler_params=_cparams(1),
        name="moe_experts",
    )(tile_expert, n_valid, group.astype(I32), next_expert, inv3, inv3, b1, b2, w1, w2, tokens)


def _combine_body(tw_ref, x1_ref, gf_ref, rows_ref, y_ref):
    tt, d = x1_ref.shape
    yp = d // LANES
    tw = tw_ref[...]
    cols = []
    for c in range(yp):
        acc = None
        for k in range(TOP_K):
            piece = rows_ref[pl.ds(k * tt * yp + c, tt, stride=yp), :]
            term = tw[:, k:k + 1] * piece
            acc = term if acc is None else acc + term
        cols.append(acc)
    y_ref[...] = _rms(x1_ref[...] + jnp.concatenate(cols, axis=-1), gf_ref[...])


def _combine(tw, x1, g_final, expert_rows, first_tile):
    t, d = x1.shape
    tt = COMBINE_TILE
    return pl.pallas_call(
        _combine_body,
        grid=(t // tt,),
        in_specs=[
            pl.BlockSpec((tt, TOP_K), lambda i: (i, 0)),
            pl.BlockSpec((tt, d), lambda i: (i, 0)),
            pl.BlockSpec((1, d), lambda i: (0, 0)),
            pl.BlockSpec((TOP_K * tt * (d // LANES), LANES), lambda i: (first_tile + i, 0)),
        ],
        out_specs=pl.BlockSpec((tt, d), lambda i: (i, 0)),
        out_shape=jax.ShapeDtypeStruct((t, d), F32),
        compiler_params=_cparams(1),
        name="moe_combine",
    )(tw, x1, g_final, expert_rows)


def kernel(x_prompt, x_sample, cache_k, cache_v, state_conv, page_table, g_mix, w_in, b_in,
           w_dw, b_dw, ln_g, ln_b, w_conv_out, lam_q1, lam_k1, lam_q2, lam_k2, subln_g,
           w_attn_out, w_o, g_ffn, w_router, b_router, w_moe1, b_moe1, w_moe2, b_moe2, g_final):
    bsz, seq, d = x_prompt.shape
    dec_b, dec_s, _ = x_sample.shape
    depth = g_mix.shape[0]
    c_conv = w_dw.shape[2]
    attn_w = N_HEADS * HEAD_W
    n_pages = page_table.shape[1]
    past_len = n_pages * PAGE_SIZE
    t_p, t_s = bsz * seq, dec_b * dec_s
    t_all = t_p + t_s
    d_ff = w_moe2.shape[2]
    assert depth == 1, "the combine kernel fuses the final norm, so only one layer is supported"
    assert seq % ROW_TILE == 0 and t_s % ROW_TILE == 0 and seq % CONV_ROWS == 0
    assert seq % ATTN_BLOCK == 0 and t_all % POS_TILE == 0 and dec_b % 8 == 0
    assert t_p % COMBINE_TILE == 0 and t_s % COMBINE_TILE == 0

    tab_p = _rope_tables(jnp.arange(seq))
    tab_s = _rope_tables(jnp.tile(past_len + jnp.arange(dec_s), dec_b))
    n_rows = t_all * TOP_K + N_EXPERTS * MOE_TILE
    n_tiles = n_rows // MOE_TILE
    row2 = lambda v: v.reshape(1, -1)

    hp, hs = x_prompt.reshape(t_p, d), x_sample.reshape(t_s, d)
    outs = [[] for _ in range(6)]
    for l in range(depth):
        lam_init = 0.8 - 0.6 * math.exp(-0.3 * l)
        lams = (row2(lam_q1[l]), row2(lam_k1[l]), row2(lam_q2[l]), row2(lam_k2[l]))
        subg = row2(subln_g[l])
        w_in_bf = w_in[l].astype(BF16)
        wc, wa, wo = (w_conv_out[l].astype(BF16), w_attn_out[l].astype(BF16),
                      w_o[l].astype(BF16))
        wr = w_router[l].astype(BF16)
        b1 = b_moe1[l].reshape(N_EXPERTS, 1, 2 * d_ff)
        b2 = b_moe2[l].reshape(N_EXPERTS, 1, d)
        proj_args = (row2(g_mix[l]), w_in_bf, row2(b_in[l]))
        post_args = (row2(ln_g[l]), row2(ln_b[l]), wc, wa, wo, row2(g_ffn[l]), wr,
                     row2(b_router[l]))

        a_p, q_p, k_p, v_p, gate_p = _in_proj(hp, *proj_args, tab_p, seq // ROW_TILE,
                                              c_conv, attn_w)
        a_p3 = a_p.reshape(bsz, seq, c_conv)
        conv_p = _conv_prompt(a_p3, w_dw[l], row2(b_dw[l]))
        o_p = _attn_prompt(q_p.reshape(bsz, seq, attn_w), k_p.reshape(bsz, seq, attn_w),
                           v_p.reshape(bsz, seq, attn_w), lams, subg, lam_init)
        tokens = jnp.zeros(((t_all + PAD_TOKENS) * (d // 2 // LANES), LANES), U32)
        x1_p, tokens, ids_p, tw_p = _post(hp, conv_p.reshape(t_p, c_conv),
                                          o_p.reshape(t_p, attn_w), gate_p, *post_args,
                                          tokens, 0)

        a_s, q_s, k_s, v_s, gate_s = _in_proj(hs, *proj_args, tab_s, 1, c_conv, attn_w)
        a_s3 = a_s.reshape(dec_b, dec_s, c_conv)
        conv_s = _conv_decode(state_conv[l], a_s3, w_dw[l], row2(b_dw[l]))
        pool = cache_k.shape[1]
        o_s = _attn_decode(page_table, q_s.astype(F32).reshape(dec_b, dec_s, attn_w),
                           k_s.reshape(dec_b, dec_s, attn_w), v_s.reshape(dec_b, dec_s, attn_w),
                           cache_k[l].reshape(pool * PAGE_SIZE * N_HEADS, HEAD_W),
                           cache_v[l].reshape(pool * PAGE_SIZE * N_HEADS, HEAD_W),
                           lams, subg, lam_init)
        x1_s, tokens, ids_s, tw_s = _post(hs, conv_s.reshape(t_s, c_conv),
                                          o_s.reshape(t_s, attn_w).astype(BF16), gate_s,
                                          *post_args, tokens, t_p)

        ids = jnp.concatenate([ids_p, ids_s], axis=0)
        pos, word, te = _positions(ids, n_tiles)
        tile_expert = jnp.minimum(te[:, 0], N_EXPERTS - 1)
        n_valid = jnp.sum((te[:, 0] < N_EXPERTS).astype(I32)).reshape(1)
        slot_idx = jnp.arange(n_rows, dtype=U32)
        n_out = t_all * TOP_K
        unused = (((n_out + slot_idx % MOE_TILE) << ROUTE_TOKEN_BITS)
                  | (t_all + slot_idx % PAD_TOKENS))
        inv = _inverse(pos.reshape(-1), word.reshape(-1), lax.bitcast_convert_type(unused, I32))
        expert_rows = _experts(tile_expert, n_valid, inv.reshape(n_tiles, 1, MOE_TILE), tokens,
                               w_moe1[l], b1, w_moe2[l], b2, t_all, n_out)
        gf = row2(g_final)
        hp = _combine(tw_p, x1_p, gf, expert_rows, 0)
        hs = _combine(tw_s, x1_s, gf, expert_rows, t_p // COMBINE_TILE)

        outs[0].append(k_p.reshape(bsz, seq, N_HEADS, HEAD_W))
        outs[1].append(v_p.reshape(bsz, seq, N_HEADS, HEAD_W))
        outs[2].append(a_p3[:, seq - (CONV_WIDTH - 1):])
        outs[3].append(k_s.reshape(dec_b, dec_s, N_HEADS, HEAD_W))
        outs[4].append(v_s.reshape(dec_b, dec_s, N_HEADS, HEAD_W))
        outs[5].append(jnp.concatenate([state_conv[l], a_s3], axis=1)[:, -(CONV_WIDTH - 1):])

    y_prompt = hp.reshape(bsz, seq, d)
    y_sample = hs.reshape(dec_b, dec_s, d)
    return (y_prompt, y_sample) + tuple(jnp.stack(o) for o in outs)
```
